```python
import math
import jax, jax.numpy as jnp
from jax import lax
import numpy as np

D_MODEL = 1024
BATCH = 2
SEQ = 8192
DEPTH = 1

CHUNK = 64
Q_BLOCK = 128
HG_HEADS = 8
HG_DK = 128
HG_DV = D_MODEL // HG_HEADS
HG_KW = HG_HEADS * HG_DK
HG_VW = HG_HEADS * HG_DV
FOX_HEADS = 8
FOX_DH = 128
FOX_W = FOX_HEADS * FOX_DH
MEM_LEN = 256
MEM_HEADS = 4
MEM_DH = D_MODEL // MEM_HEADS
D_FF = 2816
N_BRANCH = 2
EPS = 1e-6
IN_SPLITS = [HG_KW, HG_KW, HG_VW, HG_VW, FOX_W, FOX_W, FOX_W, FOX_HEADS, N_BRANCH * D_MODEL]
IN_COLS = sum(IN_SPLITS)
IN_OFFSETS = list(np.cumsum(IN_SPLITS)[:-1])

kernel_name = "hgrn2_fox_macaron_sandwich_hybrid"


def rmsnorm(x, g):
    x32 = x.astype(jnp.float32)
    y = x32 * lax.rsqrt(jnp.mean(x32 * x32, axis=-1, keepdims=True) + EPS)
    return (y * g.astype(jnp.float32)).astype(x.dtype)


def swiglu(h, w_in, w_down):
    gate, up = jnp.split(h @ w_in, 2, axis=-1)
    return (jax.nn.silu(gate) * up) @ w_down


def hgrn2_chunkwise(q, f_logit, inp, lb):
    B, S = q.shape[0], q.shape[1]
    n = S // CHUNK
    f = lb + (1.0 - lb) * jax.nn.sigmoid(f_logit.astype(jnp.float32))
    logf = jnp.log(f)
    k = 1.0 - f
    q = jax.nn.silu(q.astype(jnp.float32))
    inp = inp.astype(jnp.float32)

    def to_chunks(t):
        return t.reshape(B, n, CHUNK, t.shape[2], t.shape[3]).transpose(1, 0, 3, 2, 4)

    qc, kc, ic = to_chunks(q), to_chunks(k), to_chunks(inp)
    bc = jnp.cumsum(to_chunks(logf), axis=3)
    mask = jnp.tril(jnp.ones((CHUNK, CHUNK), dtype=bool))[:, :, None]

    def step(state, xs):
        qt, kt, it, bt = xs
        diff = bt[:, :, :, None, :] - bt[:, :, None, :, :]
        decay = jnp.exp(jnp.where(mask, diff, -jnp.inf))
        a = jnp.einsum('bhtd,bhsd,bhtsd->bhts', qt, kt, decay)
        o_intra = jnp.einsum('bhts,bhsv->bhtv', a, it)
        o_inter = jnp.einsum('bhtd,bhdv->bhtv', qt * jnp.exp(bt), state)
        b_last = bt[:, :, -1:, :]
        new_state = jnp.exp(b_last[:, :, 0, :])[..., None] * state + jnp.einsum(
            'bhsd,bhsv->bhdv', kt * jnp.exp(b_last - bt), it)
        return new_state, o_intra + o_inter

    s0 = jnp.zeros((B, q.shape[2], HG_DK, HG_DV), jnp.float32)
    _, o = lax.scan(step, s0, (qc, kc, ic, bc))
    return o.transpose(1, 0, 3, 2, 4).reshape(B, S, HG_VW)


def fox_attention(q, k, v, f_logit):
    B, S, H, dh = q.shape
    n_blk = S // Q_BLOCK
    scale = 1.0 / math.sqrt(dh)
    c = jnp.cumsum(jax.nn.log_sigmoid(f_logit.astype(jnp.float32)), axis=1).transpose(0, 2, 1)
    k32 = k.astype(jnp.float32)
    v32 = v.astype(jnp.float32)
    qb = q.astype(jnp.float32).reshape(B, n_blk, Q_BLOCK, H, dh).transpose(1, 0, 3, 2, 4)
    cb = c.reshape(B, H, n_blk, Q_BLOCK).transpose(2, 0, 1, 3)
    kpos = jnp.arange(S)

    def block(args):
        qi, ci, blk = args
        s = jnp.einsum('bhqd,bkhd->bhqk', qi, k32) * scale
        s = s + ci[..., None] - c[:, :, None, :]
        qpos = blk * Q_BLOCK + jnp.arange(Q_BLOCK)
        s = jnp.where(kpos[None, :] <= qpos[:, None], s, -jnp.inf)
        p = jax.nn.softmax(s, axis=-1)
        return jnp.einsum('bhqk,bkhd->bqhd', p, v32)

    o = lax.map(block, (qb, cb, jnp.arange(n_blk)))
    return o.transpose(1, 0, 2, 3, 4).reshape(B, S, H * dh)


def mem_cross_attention(h, mem_n, w_mq, w_mkv, w_mo):
    B, S, _ = h.shape
    q = (h @ w_mq).reshape(B, S, MEM_HEADS, MEM_DH).astype(jnp.float32)
    k, v = jnp.split(mem_n @ w_mkv, 2, axis=-1)
    k = k.reshape(B, MEM_LEN, MEM_HEADS, MEM_DH).astype(jnp.float32)
    v = v.reshape(B, MEM_LEN, MEM_HEADS, MEM_DH).astype(jnp.float32)
    s = jnp.einsum('bqhd,bkhd->bhqk', q, k) * (1.0 / math.sqrt(MEM_DH))
    p = jax.nn.softmax(s, axis=-1)
    o = jnp.einsum('bhqk,bkhd->bqhd', p, v).reshape(B, S, D_MODEL).astype(h.dtype)
    return o @ w_mo


def setup_inputs(seed: int = 0) -> dict:
    key = jax.random.key(seed)
    ks = iter(jax.random.split(key, 40))
    f32 = jnp.float32

    def w(shape, fan_in):
        return jax.random.normal(next(ks), shape, f32) * (fan_in ** -0.5)

    def gain(shape):
        return 1.0 + 0.05 * jax.random.normal(next(ks), shape, f32)

    L, D = DEPTH, D_MODEL
    return {
        "x": jax.random.normal(next(ks), (BATCH, SEQ, D), f32),
        "mem": jax.random.normal(next(ks), (BATCH, MEM_LEN, D), f32),
        "ffn1_pre_g": gain((L, D)),
        "ffn1_w_in": w((L, D, 2 * D_FF), D),
        "ffn1_w_down": w((L, D_FF, D), D_FF),
        "ffn1_post_g": gain((L, D)),
        "mix_pre_g": gain((L, D)),
        "w_in": w((L, D, IN_COLS), D),
        "hg_lb_logits": 0.5 * jax.random.normal(next(ks), (L + 1, HG_HEADS, HG_DK), f32),
        "hg_norm_g": gain((L, HG_VW)),
        "fox_f_bias": 1.0 + 0.5 * jax.random.normal(next(ks), (L, FOX_HEADS), f32),
        "w_branch_a": w((L, HG_VW, D), HG_VW),
        "w_branch_b": w((L, FOX_W, D), FOX_W),
        "b_gate": 0.02 * jax.random.normal(next(ks), (L, N_BRANCH * D), f32),
        "w_out": w((L, D, D), D),
        "mix_post_g": gain((L, D)),
        "mem_pre_g": gain((L, D)),
        "mem_kv_g": gain((L, D)),
        "w_mq": w((L, D, D), D),
        "w_mkv": w((L, D, 2 * D), D),
        "w_mo": w((L, D, D), D),
        "mem_post_g": gain((L, D)),
        "ffn2_pre_g": gain((L, D)),
        "ffn2_w_in": w((L, D, 2 * D_FF), D),
        "ffn2_w_down": w((L, D_FF, D), D_FF),
        "ffn2_post_g": gain((L, D)),
    }


def reference(x, mem, ffn1_pre_g, ffn1_w_in, ffn1_w_down, ffn1_post_g,
              mix_pre_g, w_in, hg_lb_logits, hg_norm_g, fox_f_bias,
              w_branch_a, w_branch_b, b_gate, w_out, mix_post_g,
              mem_pre_g, mem_kv_g, w_mq, w_mkv, w_mo, mem_post_g,
              ffn2_pre_g, ffn2_w_in, ffn2_w_down, ffn2_post_g):
    B, S, D = x.shape
    lb_all = jnp.cumsum(jax.nn.softmax(hg_lb_logits.astype(jnp.float32), axis=0), axis=0)
    for l in range(DEPTH):
        h = rmsnorm(x, ffn1_pre_g[l])
        x = x + 0.5 * rmsnorm(swiglu(h, ffn1_w_in[l], ffn1_w_down[l]), ffn1_post_g[l])

        h = rmsnorm(x, mix_pre_g[l])
        q_a, f_a, i_a, g_a, q_b, k_b, v_b, f_b, gates = jnp.split(h @ w_in[l], IN_OFFSETS, axis=-1)

        o_a = hgrn2_chunkwise(q_a.reshape(B, S, HG_HEADS, HG_DK),
                              f_a.reshape(B, S, HG_HEADS, HG_DK),
                              i_a.reshape(B, S, HG_HEADS, HG_DV), lb_all[l])
        o_a = rmsnorm(o_a.astype(x.dtype), hg_norm_g[l]) * jax.nn.silu(g_a)
        y_a = o_a @ w_branch_a[l]

        o_b = fox_attention(q_b.reshape(B, S, FOX_HEADS, FOX_DH),
                            k_b.reshape(B, S, FOX_HEADS, FOX_DH),
                            v_b.reshape(B, S, FOX_HEADS, FOX_DH),
                            f_b + fox_f_bias[l])
        y_b = o_b.astype(x.dtype) @ w_branch_b[l]

        gate = jax.nn.sigmoid((gates + b_gate[l]).astype(jnp.float32)).reshape(B, S, N_BRANCH, D)
        y = (gate[:, :, 0] * y_a.astype(jnp.float32) + gate[:, :, 1] * y_b.astype(jnp.float32)).astype(x.dtype)
        x = x + rmsnorm(y @ w_out[l], mix_post_g[l])

        h = rmsnorm(x, mem_pre_g[l])
        mem_n = rmsnorm(mem, mem_kv_g[l])
        x = x + rmsnorm(mem_cross_attention(h, mem_n, w_mq[l], w_mkv[l], w_mo[l]), mem_post_g[l])

        h = rmsnorm(x, ffn2_pre_g[l])
        x = x + 0.5 * rmsnorm(swiglu(h, ffn2_w_in[l], ffn2_w_down[l]), ffn2_post_g[l])
    return x
```

```python
import functools
import math

import jax
import jax.numpy as jnp
import numpy as np
from jax import lax
from jax.experimental import pallas as pl
from jax.experimental.pallas import tpu as pltpu

F32 = jnp.float32
BF16 = jnp.bfloat16

D_MODEL = 1024
HG_HEADS = 8
HG_DK = 128
FOX_HEADS = 8
FOX_DH = 128
MEM_LEN = 256
MEM_HEADS = 4
MEM_DH = D_MODEL // MEM_HEADS
D_FF = 2816
EPS = 1e-6
LANES = 128

VMEM_LIMIT = 56 * 1024 * 1024

NT_DIMS = (((1,), (1,)), ((), ()))
TN_DIMS = (((0,), (0,)), ((), ()))


def _params(sem):
    return pltpu.CompilerParams(dimension_semantics=sem, vmem_limit_bytes=VMEM_LIMIT)


def _rms(x, g):
    ms = jnp.mean(x * x, axis=-1, keepdims=True)
    return x * lax.rsqrt(ms + EPS) * g


def _sigmoid(x):
    return 1.0 / (1.0 + jnp.exp(-x))


def _dot(a, b):
    return jnp.dot(a, b, preferred_element_type=F32)


def _norm_matmul_kernel(x_ref, g_ref, w_ref, o_ref, h_ref):
    @pl.when(pl.program_id(1) == 0)
    def _():
        h_ref[...] = _rms(x_ref[...], g_ref[...]).astype(BF16)

    o_ref[...] = _dot(h_ref[...], w_ref[...]).astype(o_ref.dtype)


def _norm_matmul(x, g, w, out_dtype, tm, tn):
    m, d = x.shape
    n = w.shape[1]
    return pl.pallas_call(
        _norm_matmul_kernel,
        grid=(m // tm, n // tn),
        in_specs=[
            pl.BlockSpec((tm, d), lambda i, j: (i, 0)),
            pl.BlockSpec((1, d), lambda i, j: (0, 0)),
            pl.BlockSpec((d, tn), lambda i, j: (0, j)),
        ],
        out_specs=pl.BlockSpec((tm, tn), lambda i, j: (i, j)),
        out_shape=jax.ShapeDtypeStruct((m, n), out_dtype),
        scratch_shapes=[pltpu.VMEM((tm, d), BF16)],
        compiler_params=_params(("parallel", "arbitrary")),
        name="norm_matmul",
    )(x, g.reshape(1, d), w)


def _ffn_kernel(x_ref, pre_ref, wg_ref, wu_ref, wd_ref, post_ref, o_ref, h_ref, acc_ref):
    f = pl.program_id(1)

    @pl.when(f == 0)
    def _():
        h_ref[...] = _rms(x_ref[...], pre_ref[...]).astype(BF16)
        acc_ref[...] = jnp.zeros_like(acc_ref)

    h = h_ref[...]
    gate = _dot(h, wg_ref[...])
    up = _dot(h, wu_ref[...])
    act = (gate * _sigmoid(gate) * up).astype(BF16)
    acc_ref[...] += _dot(act, wd_ref[...])

    @pl.when(f == pl.num_programs(1) - 1)
    def _():
        o_ref[...] = x_ref[...] + 0.5 * _rms(acc_ref[...], post_ref[...])


def _ffn(x, pre_g, w_in, w_down, post_g, tm=1024, tf=256):
    m, d = x.shape
    nf = D_FF // tf
    return pl.pallas_call(
        _ffn_kernel,
        grid=(m // tm, nf),
        in_specs=[
            pl.BlockSpec((tm, d), lambda i, f: (i, 0)),
            pl.BlockSpec((1, d), lambda i, f: (0, 0)),
            pl.BlockSpec((d, tf), lambda i, f: (0, f)),
            pl.BlockSpec((d, tf), lambda i, f: (0, nf + f)),
            pl.BlockSpec((tf, d), lambda i, f: (f, 0)),
            pl.BlockSpec((1, d), lambda i, f: (0, 0)),
        ],
        out_specs=pl.BlockSpec((tm, d), lambda i, f: (i, 0)),
        out_shape=jax.ShapeDtypeStruct((m, d), F32),
        scratch_shapes=[pltpu.VMEM((tm, d), BF16), pltpu.VMEM((tm, d), F32)],
        compiler_params=_params(("parallel", "arbitrary")),
        name="ffn",
    )(x, pre_g.reshape(1, d), w_in, w_in, w_down, post_g.reshape(1, d))


def _split3(x):
    hi = x.astype(BF16)
    r = x - hi.astype(F32)
    mid = r.astype(BF16)
    lo = (r - mid.astype(F32)).astype(BF16)
    return hi, mid, lo


def _fcum_kernel(fb_ref, bias_ref, c_ref, carry_ref):
    @pl.when(pl.program_id(1) == 0)
    def _():
        carry_ref[...] = jnp.zeros_like(carry_ref)

    z = fb_ref[...] + bias_ref[...]
    ls = jnp.minimum(z, 0.0) - jnp.log(1.0 + jnp.exp(-jnp.abs(z)))
    tb = z.shape[0]
    row = lax.broadcasted_iota(jnp.int32, (tb, tb), 0)
    col = lax.broadcasted_iota(jnp.int32, (tb, tb), 1)
    tril = (col <= row).astype(BF16)
    hi, mid, lo = _split3(ls)
    c = _dot(tril, hi) + _dot(tril, mid) + _dot(tril, lo) + carry_ref[...]
    c_ref[...] = c
    carry_ref[...] = c[tb - 1:tb, :]


def _fcum(fb, bias, batch, tb=512):
    m = fb.shape[0]
    ns = m // batch // tb
    return pl.pallas_call(
        _fcum_kernel,
        grid=(batch, ns),
        in_specs=[
            pl.BlockSpec((tb, LANES), lambda b, s: (b * ns + s, 0)),
            pl.BlockSpec((1, LANES), lambda b, s: (0, 0)),
        ],
        out_specs=pl.BlockSpec((tb, LANES), lambda b, s: (b * ns + s, 0)),
        out_shape=jax.ShapeDtypeStruct((m, LANES), F32),
        scratch_shapes=[pltpu.VMEM((1, LANES), F32)],
        compiler_params=_params(("parallel", "arbitrary")),
        name="fox_decay_cumsum",
    )(fb, bias)


def _hgrn_tables(c):
    n_lvl = int(math.log2(c))
    sums = np.zeros((2 + n_lvl, c, c), np.float32)
    t = np.arange(c)[:, None]
    j = np.arange(c)[None, :]
    sums[0] = j <= t
    sums[1] = j > t
    level = np.full((c, c), -2, np.int32)
    level[t == j] = -1
    for l in range(n_lvl):
        h = c >> (l + 1)
        mid = (t // (2 * h)) * (2 * h) + h
        upper = t >= mid
        sums[2 + l] = np.where(upper, (j >= mid) & (j <= t), (j > t) & (j < mid))
        blk_t = t // (2 * h)
        blk_s = j // (2 * h)
        sel = (blk_t == blk_s) & (t >= mid) & (j < mid)
        level[sel & (level == -2)] = l
    return sums.reshape((2 + n_lvl) * c, c), level


def _hgrn_kernel(q_ref, f_ref, i_ref, lb_ref, sums_ref, lvl_ref, o_ref, st_ref, *, chunk, layer):
    @pl.when(pl.program_id(2) == 0)
    def _():
        st_ref[...] = jnp.zeros_like(st_ref)

    c = chunk
    n_lvl = int(math.log2(c))
    ts = q_ref.shape[0]
    logits = lb_ref[...]
    e = jnp.exp(logits - jnp.max(logits, axis=0, keepdims=True))
    lb = jnp.sum(e[0:layer + 1, :], axis=0, keepdims=True) / jnp.sum(e, axis=0, keepdims=True)
    sums = sums_ref[...]
    lvl = lvl_ref[...]

    for ci in range(ts // c):
        rows = pl.ds(ci * c, c)
        qraw = q_ref[rows, :]
        q = qraw * _sigmoid(qraw)
        f = lb + (1.0 - lb) * _sigmoid(f_ref[rows, :])
        logf = jnp.log(f)
        k = 1.0 - f
        v = i_ref[rows, :].astype(BF16)

        hi, mid, lo = _split3(logf)
        dec = jnp.exp(_dot(sums, hi) + _dot(sums, mid) + _dot(sums, lo))
        e_b = dec[0:c]
        e_last = dec[c:2 * c]

        a = lax.dot_general(q.astype(BF16), k.astype(BF16), NT_DIMS, preferred_element_type=F32)
        a = jnp.where(lvl == -1, a, 0.0)
        for l in range(n_lvl):
            e_l = dec[(2 + l) * c:(3 + l) * c]
            p = lax.dot_general((q * e_l).astype(BF16), (k * e_l).astype(BF16), NT_DIMS,
                                preferred_element_type=F32)
            a = jnp.where(lvl == l, p, a)

        st = st_ref[...]
        o = _dot(a.astype(BF16), v)
        o += lax.dot_general((q * e_b).astype(BF16), st.astype(BF16), NT_DIMS,
                             preferred_element_type=F32)
        o_ref[rows, :] = o
        upd = lax.dot_general(v, (k * e_last).astype(BF16), TN_DIMS, preferred_element_type=F32)
        st_ref[...] = st * e_b[c - 1:c, :] + upd


def _hgrn2(proj_a, lb_logits, layer, batch, ts=256, chunk=64):
    m = proj_a.shape[0]
    ns = m // batch // ts
    sums, level = _hgrn_tables(chunk)
    nl = lb_logits.shape[0]
    row = lambda b, h, s: b * ns + s
    return pl.pallas_call(
        functools.partial(_hgrn_kernel, chunk=chunk, layer=layer),
        grid=(batch, HG_HEADS, ns),
        in_specs=[
            pl.BlockSpec((ts, HG_DK), lambda b, h, s: (row(b, h, s), h)),
            pl.BlockSpec((ts, HG_DK), lambda b, h, s: (row(b, h, s), HG_HEADS + h)),
            pl.BlockSpec((ts, HG_DK), lambda b, h, s: (row(b, h, s), 2 * HG_HEADS + h)),
            pl.BlockSpec((nl, HG_DK), lambda b, h, s: (0, h)),
            pl.BlockSpec(sums.shape, lambda b, h, s: (0, 0)),
            pl.BlockSpec(level.shape, lambda b, h, s: (0, 0)),
        ],
        out_specs=pl.BlockSpec((ts, HG_DK), lambda b, h, s: (row(b, h, s), h)),
        out_shape=jax.ShapeDtypeStruct((m, HG_HEADS * HG_DK), F32),
        scratch_shapes=[pltpu.VMEM((HG_DK, HG_DK), F32)],
        compiler_params=_params(("parallel", "parallel", "arbitrary")),
        name="hgrn2",
    )(proj_a, proj_a, proj_a, lb_logits.reshape(nl, HG_HEADS * HG_DK),
      jnp.asarray(sums, BF16), jnp.asarray(level))


def _fox_kernel(q_ref, k_ref, v_ref, cq_ref, ck_ref, o_ref, m_ref, l_ref, acc_ref, *, tk):
    h = pl.program_id(1)
    qi = pl.program_id(2)
    tq = q_ref.shape[0]
    q = q_ref[...]
    lane = lax.broadcasted_iota(jnp.int32, (tq, LANES), 1)
    cq = jnp.sum(jnp.where(lane == h, cq_ref[...], 0.0), axis=1, keepdims=True)

    m_ref[...] = jnp.full_like(m_ref, -jnp.inf)
    l_ref[...] = jnp.zeros_like(l_ref)
    acc_ref[...] = jnp.zeros_like(acc_ref)

    def step(kb, masked):
        cols = pl.ds(pl.multiple_of(kb * tk, tk), tk)
        s = lax.dot_general(q, k_ref[cols, :], NT_DIMS, preferred_element_type=F32)
        s = s + cq - ck_ref[kb]
        if masked:
            r = lax.broadcasted_iota(jnp.int32, (tq, tk), 0)
            cidx = lax.broadcasted_iota(jnp.int32, (tq, tk), 1)
            s = jnp.where(cidx <= r, s, -jnp.inf)
        m_prev = m_ref[...]
        m_new = jnp.maximum(m_prev, jnp.max(s, axis=1, keepdims=True))
        alpha = jnp.exp(m_prev - m_new)
        p = jnp.exp(s - m_new)
        l_ref[...] = alpha * l_ref[...] + jnp.sum(p, axis=1, keepdims=True)
        acc_ref[...] = alpha * acc_ref[...] + _dot(p.astype(BF16), v_ref[cols, :])
        m_ref[...] = m_new

    def body(kb, carry):
        step(kb, False)
        return carry

    lax.fori_loop(0, qi, body, 0)
    step(qi, True)
    o_ref[...] = (acc_ref[...] / l_ref[...]).astype(o_ref.dtype)


def _fox(proj_b, c, c_t, batch, seq, tq=512):
    m = proj_b.shape[0]
    nq = seq // tq
    return pl.pallas_call(
        functools.partial(_fox_kernel, tk=tq),
        grid=(batch, FOX_HEADS, nq),
        in_specs=[
            pl.BlockSpec((tq, FOX_DH), lambda b, h, i: (b * nq + i, h)),
            pl.BlockSpec((seq, FOX_DH), lambda b, h, i: (b, FOX_HEADS + h)),
            pl.BlockSpec((seq, FOX_DH), lambda b, h, i: (b, 2 * FOX_HEADS + h)),
            pl.BlockSpec((tq, LANES), lambda b, h, i: (b * nq + i, 0)),
            pl.BlockSpec((None, nq, 1, tq), lambda b, h, i: (b * FOX_HEADS + h, 0, 0, 0)),
        ],
        out_specs=pl.BlockSpec((tq, FOX_DH), lambda b, h, i: (b * nq + i, h)),
        out_shape=jax.ShapeDtypeStruct((m, FOX_HEADS * FOX_DH), BF16),
        scratch_shapes=[
            pltpu.VMEM((tq, 1), F32),
            pltpu.VMEM((tq, 1), F32),
            pltpu.VMEM((tq, FOX_DH), F32),
        ],
        compiler_params=_params(("parallel", "parallel", "arbitrary")),
        name="fox_attention",
    )(proj_b, proj_b, proj_b, c, c_t.reshape(batch * FOX_HEADS, nq, 1, tq))


def _merge_kernel(x_ref, oa_ref, ga_ref, ob_ref, g0_ref, g1_ref, bg0_ref, bg1_ref, hg_ref,
                  wa_ref, wb_ref, wo_ref, post_ref, o_ref):
    ga = ga_ref[...]
    oa = _rms(oa_ref[...], hg_ref[...]) * (ga * _sigmoid(ga))
    ya = _dot(oa.astype(BF16), wa_ref[...])
    yb = _dot(ob_ref[...], wb_ref[...])
    y = _sigmoid(g0_ref[...] + bg0_ref[...]) * ya + _sigmoid(g1_ref[...] + bg1_ref[...]) * yb
    z = _dot(y.astype(BF16), wo_ref[...])
    o_ref[...] = x_ref[...] + _rms(z, post_ref[...])


def _merge(x, o_a, proj_a, o_b, gates, b_gate, hg_norm_g, w_a, w_b, w_o, post_g, tm=512):
    m, d = x.shape
    row = lambda c: pl.BlockSpec((tm, d), lambda i: (i, c))
    vec = lambda c: pl.BlockSpec((1, d), lambda i: (0, c))
    mat = pl.BlockSpec((d, d), lambda i: (0, 0))
    return pl.pallas_call(
        _merge_kernel,
        grid=(m // tm,),
        in_specs=[row(0), row(0), row(3), row(0), row(0), row(1), vec(0), vec(1), vec(0),
                  mat, mat, mat, vec(0)],
        out_specs=row(0),
        out_shape=jax.ShapeDtypeStruct((m, d), F32),
        compiler_params=_params(("parallel",)),
        name="branch_merge",
    )(x, o_a, proj_a, o_b, gates, gates, b_gate, b_gate, hg_norm_g.reshape(1, d),
      w_a, w_b, w_o, post_g.reshape(1, d))


def _memattn_kernel(x_ref, pre_ref, kv_ref, wq_ref, wo_ref, post_ref, o_ref):
    x = x_ref[...]
    h = _rms(x, pre_ref[...]).astype(BF16)
    q = _dot(h, wq_ref[...]).astype(BF16)
    heads = []
    for hd in range(MEM_HEADS):
        lo = hd * MEM_DH
        kh = kv_ref[:, lo:lo + MEM_DH]
        vh = kv_ref[:, D_MODEL + lo:D_MODEL + lo + MEM_DH]
        s = lax.dot_general(q[:, lo:lo + MEM_DH], kh, NT_DIMS, preferred_element_type=F32)
        p = jnp.exp(s - jnp.max(s, axis=1, keepdims=True))
        p = p / jnp.sum(p, axis=1, keepdims=True)
        heads.append(_dot(p.astype(BF16), vh).astype(BF16))
    o = jnp.concatenate(heads, axis=1)
    o_ref[...] = x + _rms(_dot(o, wo_ref[...]), post_ref[...])


def _memattn(x, pre_g, kv, w_q, w_o, post_g, batch, tm=512):
    m, d = x.shape
    ns = m // batch // tm
    vec = pl.BlockSpec((1, d), lambda b, s: (0, 0))
    mat = pl.BlockSpec((d, d), lambda b, s: (0, 0))
    return pl.pallas_call(
        _memattn_kernel,
        grid=(batch, ns),
        in_specs=[
            pl.BlockSpec((tm, d), lambda b, s: (b * ns + s, 0)),
            vec,
            pl.BlockSpec((MEM_LEN, 2 * d), lambda b, s: (b, 0)),
            mat, mat, vec,
        ],
        out_specs=pl.BlockSpec((tm, d), lambda b, s: (b * ns + s, 0)),
        out_shape=jax.ShapeDtypeStruct((m, d), F32),
        compiler_params=_params(("parallel", "parallel")),
        name="mem_cross_attention",
    )(x, pre_g.reshape(1, d), kv, w_q, w_o, post_g.reshape(1, d))


def kernel(x, mem, ffn1_pre_g, ffn1_w_in, ffn1_w_down, ffn1_post_g, mix_pre_g, w_in, hg_lb_logits, hg_norm_g, fox_f_bias, w_branch_a, w_branch_b, b_gate, w_out, mix_post_g, mem_pre_g, mem_kv_g, w_mq, w_mkv, w_mo, mem_post_g, ffn2_pre_g, ffn2_w_in, ffn2_w_down, ffn2_post_g):
    batch, seq, d = x.shape
    depth = ffn1_w_in.shape[0]
    xf = x.reshape(batch * seq, d)
    memf = mem.reshape(batch * MEM_LEN, d)
    kw = HG_HEADS * HG_DK
    off_b = 4 * kw
    off_f = off_b + 3 * FOX_HEADS * FOX_DH
    off_g = off_f + FOX_HEADS
    for l in range(depth):
        xf = _ffn(xf, ffn1_pre_g[l], ffn1_w_in[l].astype(BF16), ffn1_w_down[l].astype(BF16),
                  ffn1_post_g[l])

        w = w_in[l]
        w_a = w[:, :off_b].astype(BF16)
        q_scale = jnp.concatenate([jnp.full((FOX_HEADS * FOX_DH,), 1.0 / math.sqrt(FOX_DH), F32),
                                   jnp.ones((2 * FOX_HEADS * FOX_DH,), F32)])
        w_b = (w[:, off_b:off_f] * q_scale).astype(BF16)
        w_f = jnp.pad(w[:, off_f:off_g], ((0, 0), (0, LANES - FOX_HEADS))).astype(BF16)
        w_g = w[:, off_g:].astype(BF16)
        proj_a = _norm_matmul(xf, mix_pre_g[l], w_a, F32, 1024, 1024)
        proj_b = _norm_matmul(xf, mix_pre_g[l], w_b, BF16, 1024, 1024)
        fb = _norm_matmul(xf, mix_pre_g[l], w_f, F32, 1024, LANES)
        gates = _norm_matmul(xf, mix_pre_g[l], w_g, F32, 1024, 1024)

        bias = jnp.pad(fox_f_bias[l], (0, LANES - FOX_HEADS)).reshape(1, LANES)
        c = _fcum(fb, bias, batch)
        c_t = c.reshape(batch, seq, LANES)[:, :, :FOX_HEADS].transpose(0, 2, 1)

        o_a = _hgrn2(proj_a, hg_lb_logits, l, batch)
        o_b = _fox(proj_b, c, c_t, batch, seq)

        xf = _merge(xf, o_a, proj_a, o_b, gates, b_gate[l].reshape(1, 2 * d), hg_norm_g[l],
                    w_branch_a[l].astype(BF16), w_branch_b[l].astype(BF16),
                    w_out[l].astype(BF16), mix_post_g[l])

        kv = _norm_matmul(memf, mem_kv_g[l], w_mkv[l].astype(BF16), BF16, MEM_LEN, 1024)
        xf = _memattn(xf, mem_pre_g[l], kv, (w_mq[l] * (1.0 / math.sqrt(MEM_DH))).astype(BF16),
                      w_mo[l].astype(BF16), mem_post_g[l], batch)

        xf = _ffn(xf, ffn2_pre_g[l], ffn2_w_in[l].astype(BF16), ffn2_w_down[l].astype(BF16),
                  ffn2_post_g[l])
    return xf.reshape(batch, seq, d)
```

```python
import functools
import math

import jax
import jax.numpy as jnp
import numpy as np
from jax import lax
from jax.experimental import pallas as pl
from jax.experimental.pallas import tpu as pltpu

F32 = jnp.float32
BF16 = jnp.bfloat16

D_MODEL = 1024
HG_HEADS = 8
HG_DK = 128
FOX_HEADS = 8
FOX_DH = 128
MEM_LEN = 256
MEM_HEADS = 4
MEM_DH = D_MODEL // MEM_HEADS
D_FF = 2816
EPS = 1e-6
LANES = 128

VMEM_LIMIT = 56 * 1024 * 1024

NT_DIMS = (((1,), (1,)), ((), ()))
TN_DIMS = (((0,), (0,)), ((), ()))


def _params(sem):
    return pltpu.CompilerParams(dimension_semantics=sem, vmem_limit_bytes=VMEM_LIMIT)


def _rms(x, g):
    ms = jnp.mean(x * x, axis=-1, keepdims=True)
    return x * lax.rsqrt(ms + EPS) * g


def _sigmoid(x):
    return 1.0 / (1.0 + jnp.exp(-x))


def _dot(a, b):
    return jnp.dot(a, b, preferred_element_type=F32)


def _norm_matmul_kernel(x_ref, g_ref, w_ref, o_ref, h_ref):
    @pl.when(pl.program_id(1) == 0)
    def _():
        h_ref[...] = _rms(x_ref[...], g_ref[...]).astype(BF16)

    o_ref[...] = _dot(h_ref[...], w_ref[...]).astype(o_ref.dtype)


def _norm_matmul(x, g, w, out_dtype, tm, tn):
    m, d = x.shape
    n = w.shape[1]
    return pl.pallas_call(
        _norm_matmul_kernel,
        grid=(m // tm, n // tn),
        in_specs=[
            pl.BlockSpec((tm, d), lambda i, j: (i, 0)),
            pl.BlockSpec((1, d), lambda i, j: (0, 0)),
            pl.BlockSpec((d, tn), lambda i, j: (0, j)),
        ],
        out_specs=pl.BlockSpec((tm, tn), lambda i, j: (i, j)),
        out_shape=jax.ShapeDtypeStruct((m, n), out_dtype),
        scratch_shapes=[pltpu.VMEM((tm, d), BF16)],
        compiler_params=_params(("parallel", "arbitrary")),
        name="norm_matmul",
    )(x, g.reshape(1, d), w)


def _ffn_kernel(x_ref, pre_ref, wg_ref, wu_ref, wd_ref, post_ref, o_ref, h_ref, acc_ref):
    f = pl.program_id(1)

    @pl.when(f == 0)
    def _():
        h_ref[...] = _rms(x_ref[...], pre_ref[...]).astype(BF16)
        acc_ref[...] = jnp.zeros_like(acc_ref)

    h = h_ref[...]
    gate = _dot(h, wg_ref[...])
    up = _dot(h, wu_ref[...])
    act = (gate * _sigmoid(gate) * up).astype(BF16)
    acc_ref[...] += _dot(act, wd_ref[...])

    @pl.when(f == pl.num_programs(1) - 1)
    def _():
        o_ref[...] = x_ref[...] + 0.5 * _rms(acc_ref[...], post_ref[...])


def _ffn(x, pre_g, w_in, w_down, post_g, tm=1024, tf=256):
    m, d = x.shape
    nf = D_FF // tf
    return pl.pallas_call(
        _ffn_kernel,
        grid=(m // tm, nf),
        in_specs=[
            pl.BlockSpec((tm, d), lambda i, f: (i, 0)),
            pl.BlockSpec((1, d), lambda i, f: (0, 0)),
            pl.BlockSpec((d, tf), lambda i, f: (0, f)),
            pl.BlockSpec((d, tf), lambda i, f: (0, nf + f)),
            pl.BlockSpec((tf, d), lambda i, f: (f, 0)),
            pl.BlockSpec((1, d), lambda i, f: (0, 0)),
        ],
        out_specs=pl.BlockSpec((tm, d), lambda i, f: (i, 0)),
        out_shape=jax.ShapeDtypeStruct((m, d), F32),
        scratch_shapes=[pltpu.VMEM((tm, d), BF16), pltpu.VMEM((tm, d), F32)],
        compiler_params=_params(("parallel", "arbitrary")),
        name="ffn",
    )(x, pre_g.reshape(1, d), w_in, w_in, w_down, post_g.reshape(1, d))


def _split3(x):
    hi = x.astype(BF16)
    r = x - hi.astype(F32)
    mid = r.astype(BF16)
    lo = (r - mid.astype(F32)).astype(BF16)
    return hi, mid, lo


def _fcum_kernel(fb_ref, bias_ref, c_ref, carry_ref):
    @pl.when(pl.program_id(1) == 0)
    def _():
        carry_ref[...] = jnp.zeros_like(carry_ref)

    z = fb_ref[...] + bias_ref[...]
    ls = jnp.minimum(z, 0.0) - jnp.log(1.0 + jnp.exp(-jnp.abs(z)))
    tb = z.shape[0]
    row = lax.broadcasted_iota(jnp.int32, (tb, tb), 0)
    col = lax.broadcasted_iota(jnp.int32, (tb, tb), 1)
    tril = (col <= row).astype(BF16)
    hi, mid, lo = _split3(ls)
    c = _dot(tril, hi) + _dot(tril, mid) + _dot(tril, lo) + carry_ref[...]
    c_ref[...] = c
    carry_ref[...] = c[tb - 1:tb, :]


def _fcum(fb, bias, batch, tb=512):
    m = fb.shape[0]
    ns = m // batch // tb
    return pl.pallas_call(
        _fcum_kernel,
        grid=(batch, ns),
        in_specs=[
            pl.BlockSpec((tb, LANES), lambda b, s: (b * ns + s, 0)),
            pl.BlockSpec((1, LANES), lambda b, s: (0, 0)),
        ],
        out_specs=pl.BlockSpec((tb, LANES), lambda b, s: (b * ns + s, 0)),
        out_shape=jax.ShapeDtypeStruct((m, LANES), F32),
        scratch_shapes=[pltpu.VMEM((1, LANES), F32)],
        compiler_params=_params(("parallel", "arbitrary")),
        name="fox_decay_cumsum",
    )(fb, bias)


def _hgrn_tables(c):
    n_lvl = int(math.log2(c))
    sums = np.zeros((2 + n_lvl, c, c), np.float32)
    t = np.arange(c)[:, None]
    j = np.arange(c)[None, :]
    sums[0] = j <= t
    sums[1] = j > t
    level = np.full((c, c), -2, np.int32)
    level[t == j] = -1
    for l in range(n_lvl):
        h = c >> (l + 1)
        mid = (t // (2 * h)) * (2 * h) + h
        upper = t >= mid
        sums[2 + l] = np.where(upper, (j >= mid) & (j <= t), (j > t) & (j < mid))
        blk_t = t // (2 * h)
        blk_s = j // (2 * h)
        sel = (blk_t == blk_s) & (t >= mid) & (j < mid)
        level[sel & (level == -2)] = l
    return sums.reshape((2 + n_lvl) * c, c), level


def _hgrn_kernel(q_ref, f_ref, i_ref, lb_ref, sums_ref, lvl_ref, o_ref, st_ref, *, chunk, layer):
    @pl.when(pl.program_id(2) == 0)
    def _():
        st_ref[...] = jnp.zeros_like(st_ref)

    c = chunk
    n_lvl = int(math.log2(c))
    ts = q_ref.shape[0]
    logits = lb_ref[...]
    e = jnp.exp(logits - jnp.max(logits, axis=0, keepdims=True))
    lb = jnp.sum(e[0:layer + 1, :], axis=0, keepdims=True) / jnp.sum(e, axis=0, keepdims=True)
    sums = sums_ref[...]
    lvl = lvl_ref[...]

    for ci in range(ts // c):
        rows = pl.ds(ci * c, c)
        qraw = q_ref[rows, :]
        q = qraw * _sigmoid(qraw)
        f = lb + (1.0 - lb) * _sigmoid(f_ref[rows, :])
        logf = jnp.log(f)
        k = 1.0 - f
        v = i_ref[rows, :].astype(BF16)

        hi, mid, lo = _split3(logf)
        dec = jnp.exp(_dot(sums, hi) + _dot(sums, mid) + _dot(sums, lo))
        e_b = dec[0:c]
        e_last = dec[c:2 * c]

        a = lax.dot_general(q.astype(BF16), k.astype(BF16), NT_DIMS, preferred_element_type=F32)
        a = jnp.where(lvl == -1, a, 0.0)
        for l in range(n_lvl):
            e_l = dec[(2 + l) * c:(3 + l) * c]
            p = lax.dot_general((q * e_l).astype(BF16), (k * e_l).astype(BF16), NT_DIMS,
                                preferred_element_type=F32)
            a = jnp.where(lvl == l, p, a)

        st = st_ref[...]
        o = _dot(a.astype(BF16), v)
        o += lax.dot_general((q * e_b).astype(BF16), st.astype(BF16), NT_DIMS,
                             preferred_element_type=F32)
        o_ref[rows, :] = o
        upd = lax.dot_general(v, (k * e_last).astype(BF16), TN_DIMS, preferred_element_type=F32)
        st_ref[...] = st * e_b[c - 1:c, :] + upd


def _hgrn2(proj_a, lb_logits, layer, batch, ts=256, chunk=64):
    m = proj_a.shape[0]
    ns = m // batch // ts
    sums, level = _hgrn_tables(chunk)
    nl = lb_logits.shape[0]
    row = lambda b, h, s: b * ns + s
    return pl.pallas_call(
        functools.partial(_hgrn_kernel, chunk=chunk, layer=layer),
        grid=(batch, HG_HEADS, ns),
        in_specs=[
            pl.BlockSpec((ts, HG_DK), lambda b, h, s: (row(b, h, s), h)),
            pl.BlockSpec((ts, HG_DK), lambda b, h, s: (row(b, h, s), HG_HEADS + h)),
            pl.BlockSpec((ts, HG_DK), lambda b, h, s: (row(b, h, s), 2 * HG_HEADS + h)),
            pl.BlockSpec((nl, HG_DK), lambda b, h, s: (0, h)),
            pl.BlockSpec(sums.shape, lambda b, h, s: (0, 0)),
            pl.BlockSpec(level.shape, lambda b, h, s: (0, 0)),
        ],
        out_specs=pl.BlockSpec((ts, HG_DK), lambda b, h, s: (row(b, h, s), h)),
        out_shape=jax.ShapeDtypeStruct((m, HG_HEADS * HG_DK), F32),
        scratch_shapes=[pltpu.VMEM((HG_DK, HG_DK), F32)],
        compiler_params=_params(("parallel", "parallel", "arbitrary")),
        name="hgrn2",
    )(proj_a, proj_a, proj_a, lb_logits.reshape(nl, HG_HEADS * HG_DK),
      jnp.asarray(sums, BF16), jnp.asarray(level))


LOG2E = 1.4426950408889634
FOX_FOLD = 8


def _head_column(c_blk, h):
    lane = lax.broadcasted_iota(jnp.int32, c_blk.shape, 1)
    return jnp.sum(jnp.where(lane == h, c_blk, 0.0), axis=1, keepdims=True)


def _decay_features(c_col, key_side):
    rows = c_col.shape[0]
    lane = lax.broadcasted_iota(jnp.int32, (rows, LANES), 1)
    x = jnp.broadcast_to(-c_col if key_side else c_col, (rows, LANES))
    hi, mid, lo = _split3(x)
    base = 0 if key_side else 3
    ones = (lane >= 3 - base) & (lane < 6 - base)
    feat = jnp.where(lane == base, hi.astype(F32),
                     jnp.where(lane == base + 1, mid.astype(F32),
                               jnp.where(lane == base + 2, lo.astype(F32),
                                         jnp.where(ones, 1.0, 0.0))))
    return feat.astype(BF16)


def _fox_kernel(q_ref, k_ref, v_ref, cq_ref, ck_ref, o_ref, kx_ref, vt_ref, qx_ref, s0_ref, s1_ref,
                m_ref, l_ref, acc_ref, *, tk):
    h = pl.program_id(1)
    qi = pl.program_id(2)
    tq = q_ref.shape[0]
    nk = kx_ref.shape[0]

    @pl.when(qi == 0)
    def _():
        def setup(kb, carry):
            rows = pl.ds(pl.multiple_of(kb * tk, tk), tk)
            kx_ref[kb, :, 0:FOX_DH] = k_ref[rows, :]
            kx_ref[kb, :, FOX_DH:2 * FOX_DH] = _decay_features(
                _head_column(ck_ref[rows, :], h) * LOG2E, True)
            vt_ref[kb] = v_ref[rows, :].T
            return carry

        lax.fori_loop(0, nk, setup, 0)

    qx_ref[:, 0:FOX_DH] = q_ref[...]
    qx_ref[:, FOX_DH:2 * FOX_DH] = _decay_features(_head_column(cq_ref[...], h) * LOG2E, False)
    m_ref[...] = jnp.full_like(m_ref, -jnp.inf)
    l_ref[...] = jnp.zeros_like(l_ref)
    acc_ref[...] = jnp.zeros_like(acc_ref)

    def scores(kb, s_ref):
        s_ref[...] = lax.dot_general(kx_ref[kb], qx_ref[...], NT_DIMS,
                                     preferred_element_type=F32)

    def fold(x, op):
        part = op(x.reshape(FOX_FOLD, tk // FOX_FOLD, tq), axis=0)
        return op(part, axis=0, keepdims=True)

    def update(kb, s_ref, masked):
        st = s_ref[...]
        if masked:
            key = lax.broadcasted_iota(jnp.int32, st.shape, 0)
            qry = lax.broadcasted_iota(jnp.int32, st.shape, 1)
            st = jnp.where(key <= qry, st, -jnp.inf)
        m_prev = m_ref[...]
        m_new = jnp.maximum(m_prev, fold(st, jnp.max))
        alpha = jnp.exp2(m_prev - m_new)
        p = jnp.exp2(st - m_new)
        l_ref[...] = alpha * l_ref[...] + fold(p, jnp.sum)
        acc_ref[...] = alpha * acc_ref[...] + _dot(vt_ref[kb], p.astype(BF16))
        m_ref[...] = m_new

    scores(0, s0_ref)

    def body(j, carry):
        kb = 2 * j
        scores(kb + 1, s1_ref)
        update(kb, s0_ref, False)
        scores(kb + 2, s0_ref)
        update(kb + 1, s1_ref, False)
        return carry

    lax.fori_loop(0, qi // 2, body, 0)

    @pl.when(qi % 2 == 1)
    def _():
        scores(qi, s1_ref)
        update(qi - 1, s0_ref, False)
        update(qi, s1_ref, True)

    @pl.when(qi % 2 == 0)
    def _():
        update(qi, s0_ref, True)

    o_ref[...] = (acc_ref[...] / l_ref[...]).T.astype(o_ref.dtype)


def _fox(proj_b, c, batch, seq, tq=512):
    m = proj_b.shape[0]
    nq = seq // tq
    return pl.pallas_call(
        functools.partial(_fox_kernel, tk=tq),
        grid=(batch, FOX_HEADS, nq),
        in_specs=[
            pl.BlockSpec((tq, FOX_DH), lambda b, h, i: (b * nq + i, h)),
            pl.BlockSpec((seq, FOX_DH), lambda b, h, i: (b, FOX_HEADS + h)),
            pl.BlockSpec((seq, FOX_DH), lambda b, h, i: (b, 2 * FOX_HEADS + h)),
            pl.BlockSpec((tq, LANES), lambda b, h, i: (b * nq + i, 0)),
            pl.BlockSpec((seq, LANES), lambda b, h, i: (b, 0)),
        ],
        out_specs=pl.BlockSpec((tq, FOX_DH), lambda b, h, i: (b * nq + i, h)),
        out_shape=jax.ShapeDtypeStruct((m, FOX_HEADS * FOX_DH), BF16),
        scratch_shapes=[
            pltpu.VMEM((nq, tq, 2 * FOX_DH), BF16),
            pltpu.VMEM((nq, FOX_DH, tq), BF16),
            pltpu.VMEM((tq, 2 * FOX_DH), BF16),
            pltpu.VMEM((tq, tq), F32),
            pltpu.VMEM((tq, tq), F32),
            pltpu.VMEM((1, tq), F32),
            pltpu.VMEM((1, tq), F32),
            pltpu.VMEM((FOX_DH, tq), F32),
        ],
        compiler_params=_params(("parallel", "parallel", "arbitrary")),
        name="fox_attention",
    )(proj_b, proj_b, proj_b, c, c)


def _merge_kernel(x_ref, oa_ref, ga_ref, ob_ref, g0_ref, g1_ref, bg0_ref, bg1_ref, hg_ref,
                  wa_ref, wb_ref, wo_ref, post_ref, o_ref):
    ga = ga_ref[...]
    oa = _rms(oa_ref[...], hg_ref[...]) * (ga * _sigmoid(ga))
    ya = _dot(oa.astype(BF16), wa_ref[...])
    yb = _dot(ob_ref[...], wb_ref[...])
    y = _sigmoid(g0_ref[...] + bg0_ref[...]) * ya + _sigmoid(g1_ref[...] + bg1_ref[...]) * yb
    z = _dot(y.astype(BF16), wo_ref[...])
    o_ref[...] = x_ref[...] + _rms(z, post_ref[...])


def _merge(x, o_a, proj_a, o_b, gates, b_gate, hg_norm_g, w_a, w_b, w_o, post_g, tm=512):
    m, d = x.shape
    row = lambda c: pl.BlockSpec((tm, d), lambda i: (i, c))
    vec = lambda c: pl.BlockSpec((1, d), lambda i: (0, c))
    mat = pl.BlockSpec((d, d), lambda i: (0, 0))
    return pl.pallas_call(
        _merge_kernel,
        grid=(m // tm,),
        in_specs=[row(0), row(0), row(3), row(0), row(0), row(1), vec(0), vec(1), vec(0),
                  mat, mat, mat, vec(0)],
        out_specs=row(0),
        out_shape=jax.ShapeDtypeStruct((m, d), F32),
        compiler_params=_params(("parallel",)),
        name="branch_merge",
    )(x, o_a, proj_a, o_b, gates, gates, b_gate, b_gate, hg_norm_g.reshape(1, d),
      w_a, w_b, w_o, post_g.reshape(1, d))


def _memattn_kernel(x_ref, pre_ref, kv_ref, wq_ref, wo_ref, post_ref, o_ref):
    x = x_ref[...]
    h = _rms(x, pre_ref[...]).astype(BF16)
    q = _dot(h, wq_ref[...]).astype(BF16)
    heads = []
    for hd in range(MEM_HEADS):
        lo = hd * MEM_DH
        kh = kv_ref[:, lo:lo + MEM_DH]
        vh = kv_ref[:, D_MODEL + lo:D_MODEL + lo + MEM_DH]
        s = lax.dot_general(q[:, lo:lo + MEM_DH], kh, NT_DIMS, preferred_element_type=F32)
        p = jnp.exp(s - jnp.max(s, axis=1, keepdims=True))
        p = p / jnp.sum(p, axis=1, keepdims=True)
        heads.append(_dot(p.astype(BF16), vh).astype(BF16))
    o = jnp.concatenate(heads, axis=1)
    o_ref[...] = x + _rms(_dot(o, wo_ref[...]), post_ref[...])


def _memattn(x, pre_g, kv, w_q, w_o, post_g, batch, tm=512):
    m, d = x.shape
    ns = m // batch // tm
    vec = pl.BlockSpec((1, d), lambda b, s: (0, 0))
    mat = pl.BlockSpec((d, d), lambda b, s: (0, 0))
    return pl.pallas_call(
        _memattn_kernel,
        grid=(batch, ns),
        in_specs=[
            pl.BlockSpec((tm, d), lambda b, s: (b * ns + s, 0)),
            vec,
            pl.BlockSpec((MEM_LEN, 2 * d), lambda b, s: (b, 0)),
            mat, mat, vec,
        ],
        out_specs=pl.BlockSpec((tm, d), lambda b, s: (b * ns + s, 0)),
        out_shape=jax.ShapeDtypeStruct((m, d), F32),
        compiler_params=_params(("parallel", "parallel")),
        name="mem_cross_attention",
    )(x, pre_g.reshape(1, d), kv, w_q, w_o, post_g.reshape(1, d))


def kernel(x, mem, ffn1_pre_g, ffn1_w_in, ffn1_w_down, ffn1_post_g, mix_pre_g, w_in, hg_lb_logits, hg_norm_g, fox_f_bias, w_branch_a, w_branch_b, b_gate, w_out, mix_post_g, mem_pre_g, mem_kv_g, w_mq, w_mkv, w_mo, mem_post_g, ffn2_pre_g, ffn2_w_in, ffn2_w_down, ffn2_post_g):
    batch, seq, d = x.shape
    depth = ffn1_w_in.shape[0]
    xf = x.reshape(batch * seq, d)
    memf = mem.reshape(batch * MEM_LEN, d)
    kw = HG_HEADS * HG_DK
    off_b = 4 * kw
    off_f = off_b + 3 * FOX_HEADS * FOX_DH
    off_g = off_f + FOX_HEADS
    for l in range(depth):
        xf = _ffn(xf, ffn1_pre_g[l], ffn1_w_in[l].astype(BF16), ffn1_w_down[l].astype(BF16),
                  ffn1_post_g[l])

        w = w_in[l]
        w_a = w[:, :off_b].astype(BF16)
        q_scale = jnp.concatenate([jnp.full((FOX_HEADS * FOX_DH,), LOG2E / math.sqrt(FOX_DH), F32),
                                   jnp.ones((2 * FOX_HEADS * FOX_DH,), F32)])
        w_b = (w[:, off_b:off_f] * q_scale).astype(BF16)
        w_f = jnp.pad(w[:, off_f:off_g], ((0, 0), (0, LANES - FOX_HEADS))).astype(BF16)
        w_g = w[:, off_g:].astype(BF16)
        proj_a = _norm_matmul(xf, mix_pre_g[l], w_a, F32, 1024, 1024)
        proj_b = _norm_matmul(xf, mix_pre_g[l], w_b, BF16, 1024, 1024)
        fb = _norm_matmul(xf, mix_pre_g[l], w_f, F32, 1024, LANES)
        gates = _norm_matmul(xf, mix_pre_g[l], w_g, F32, 1024, 1024)

        bias = jnp.pad(fox_f_bias[l], (0, LANES - FOX_HEADS)).reshape(1, LANES)
        c = _fcum(fb, bias, batch)

        o_a = _hgrn2(proj_a, hg_lb_logits, l, batch)
        o_b = _fox(proj_b, c, batch, seq)

        xf = _merge(xf, o_a, proj_a, o_b, gates, b_gate[l].reshape(1, 2 * d), hg_norm_g[l],
                    w_branch_a[l].astype(BF16), w_branch_b[l].astype(BF16),
                    w_out[l].astype(BF16), mix_post_g[l])

        kv = _norm_matmul(memf, mem_kv_g[l], w_mkv[l].astype(BF16), BF16, MEM_LEN, 1024)
        xf = _memattn(xf, mem_pre_g[l], kv, (w_mq[l] * (1.0 / math.sqrt(MEM_DH))).astype(BF16),
                      w_mo[l].astype(BF16), mem_post_g[l], batch)

        xf = _ffn(xf, ffn2_pre_g[l], ffn2_w_in[l].astype(BF16), ffn2_w_down[l].astype(BF16),
                  ffn2_post_g[l])
    return xf.reshape(batch, seq, d)
```

```python
import functools
import math

import jax
import jax.numpy as jnp
import numpy as np
from jax import lax
from jax.experimental import pallas as pl
from jax.experimental.pallas import tpu as pltpu

F32 = jnp.float32
BF16 = jnp.bfloat16

D_MODEL = 1024
HG_HEADS = 8
HG_DK = 128
FOX_HEADS = 8
FOX_DH = 128
MEM_LEN = 256
MEM_HEADS = 4
MEM_DH = D_MODEL // MEM_HEADS
D_FF = 2816
EPS = 1e-6
LANES = 128

VMEM_LIMIT = 56 * 1024 * 1024

NT_DIMS = (((1,), (1,)), ((), ()))
TN_DIMS = (((0,), (0,)), ((), ()))


def _params(sem):
    return pltpu.CompilerParams(dimension_semantics=sem, vmem_limit_bytes=VMEM_LIMIT)


def _rms(x, g):
    ms = jnp.mean(x * x, axis=-1, keepdims=True)
    return x * lax.rsqrt(ms + EPS) * g


def _sigmoid(x):
    return 0.5 * jnp.tanh(0.5 * x) + 0.5


def _dot(a, b):
    return jnp.dot(a, b, preferred_element_type=F32)


def _norm_matmul_kernel(x_ref, g_ref, w_ref, o_ref, h_ref):
    @pl.when(pl.program_id(1) == 0)
    def _():
        h_ref[...] = _rms(x_ref[...], g_ref[...]).astype(BF16)

    o_ref[...] = _dot(h_ref[...], w_ref[...]).astype(o_ref.dtype)


def _norm_matmul(x, g, w, out_dtype, tm, tn):
    m, d = x.shape
    n = w.shape[1]
    return pl.pallas_call(
        _norm_matmul_kernel,
        grid=(m // tm, n // tn),
        in_specs=[
            pl.BlockSpec((tm, d), lambda i, j: (i, 0)),
            pl.BlockSpec((1, d), lambda i, j: (0, 0)),
            pl.BlockSpec((d, tn), lambda i, j: (0, j)),
        ],
        out_specs=pl.BlockSpec((tm, tn), lambda i, j: (i, j)),
        out_shape=jax.ShapeDtypeStruct((m, n), out_dtype),
        scratch_shapes=[pltpu.VMEM((tm, d), BF16)],
        compiler_params=_params(("parallel", "arbitrary")),
        name="norm_matmul",
    )(x, g.reshape(1, d), w)


def _ffn_kernel(x_ref, pre_ref, wg_ref, wu_ref, wd_ref, post_ref, o_ref, h_ref, acc_ref):
    f = pl.program_id(1)

    @pl.when(f == 0)
    def _():
        h_ref[...] = _rms(x_ref[...], pre_ref[...]).astype(BF16)
        acc_ref[...] = jnp.zeros_like(acc_ref)

    h = h_ref[...]
    gate = _dot(h, wg_ref[...])
    up = _dot(h, wu_ref[...])
    act = (gate * _sigmoid(gate) * up).astype(BF16)
    acc_ref[...] += _dot(act, wd_ref[...])

    @pl.when(f == pl.num_programs(1) - 1)
    def _():
        o_ref[...] = x_ref[...] + 0.5 * _rms(acc_ref[...], post_ref[...])


def _ffn(x, pre_g, w_in, w_down, post_g, tm=1024, tf=256):
    m, d = x.shape
    nf = D_FF // tf
    return pl.pallas_call(
        _ffn_kernel,
        grid=(m // tm, nf),
        in_specs=[
            pl.BlockSpec((tm, d), lambda i, f: (i, 0)),
            pl.BlockSpec((1, d), lambda i, f: (0, 0)),
            pl.BlockSpec((d, tf), lambda i, f: (0, f)),
            pl.BlockSpec((d, tf), lambda i, f: (0, nf + f)),
            pl.BlockSpec((tf, d), lambda i, f: (f, 0)),
            pl.BlockSpec((1, d), lambda i, f: (0, 0)),
        ],
        out_specs=pl.BlockSpec((tm, d), lambda i, f: (i, 0)),
        out_shape=jax.ShapeDtypeStruct((m, d), F32),
        scratch_shapes=[pltpu.VMEM((tm, d), BF16), pltpu.VMEM((tm, d), F32)],
        compiler_params=_params(("parallel", "arbitrary")),
        name="ffn",
    )(x, pre_g.reshape(1, d), w_in, w_in, w_down, post_g.reshape(1, d))


def _split3(x):
    hi = x.astype(BF16)
    r = x - hi.astype(F32)
    mid = r.astype(BF16)
    lo = (r - mid.astype(F32)).astype(BF16)
    return hi, mid, lo


def _fcum_kernel(fb_ref, bias_ref, c_ref, carry_ref):
    @pl.when(pl.program_id(1) == 0)
    def _():
        carry_ref[...] = jnp.zeros_like(carry_ref)

    z = fb_ref[...] + bias_ref[...]
    ls = jnp.minimum(z, 0.0) - jnp.log(1.0 + jnp.exp(-jnp.abs(z)))
    tb = z.shape[0]
    row = lax.broadcasted_iota(jnp.int32, (tb, tb), 0)
    col = lax.broadcasted_iota(jnp.int32, (tb, tb), 1)
    tril = (col <= row).astype(BF16)
    hi, mid, lo = _split3(ls)
    c = _dot(tril, hi) + _dot(tril, mid) + _dot(tril, lo) + carry_ref[...]
    c_ref[...] = c
    carry_ref[...] = c[tb - 1:tb, :]


def _fcum(fb, bias, batch, tb=512):
    m = fb.shape[0]
    ns = m // batch // tb
    return pl.pallas_call(
        _fcum_kernel,
        grid=(batch, ns),
        in_specs=[
            pl.BlockSpec((tb, LANES), lambda b, s: (b * ns + s, 0)),
            pl.BlockSpec((1, LANES), lambda b, s: (0, 0)),
        ],
        out_specs=pl.BlockSpec((tb, LANES), lambda b, s: (b * ns + s, 0)),
        out_shape=jax.ShapeDtypeStruct((m, LANES), F32),
        scratch_shapes=[pltpu.VMEM((1, LANES), F32)],
        compiler_params=_params(("parallel", "arbitrary")),
        name="fox_decay_cumsum",
    )(fb, bias)


def _hgrn_level_table(c):
    t = np.arange(c)[:, None]
    s = np.arange(c)[None, :]
    level = np.full((c, c), -2, np.int32)
    level[t == s] = -1
    for l in range(int(math.log2(c))):
        h = c >> (l + 1)
        mid = (t // (2 * h)) * (2 * h) + h
        sel = (t // (2 * h) == s // (2 * h)) & (t >= mid) & (s < mid)
        level[sel & (level == -2)] = l
    return level


def _prefix_sum_rows(x, row):
    k = 1
    while k < x.shape[0]:
        x = x + jnp.where(row >= k, pltpu.roll(x, k, axis=0), 0.0)
        k *= 2
    return x


SUBLANES = 8


def _midpoint_rows(b, halves):
    c, w = b.shape
    out = {}
    b3 = b.reshape(c // SUBLANES, SUBLANES, w)
    sub = lax.broadcasted_iota(jnp.int32, (1, SUBLANES, w), 1)
    rolled = {0: b3}
    for h in halves:
        if h >= SUBLANES:
            pieces = [jnp.broadcast_to(b[j * 2 * h + h - 1:j * 2 * h + h, :], (2 * h, w))
                      for j in range(c // (2 * h))]
            out[h] = pieces[0] if len(pieces) == 1 else jnp.concatenate(pieces, axis=0)
            continue
        pos = jnp.bitwise_and(sub, 2 * h - 1)
        r3 = b3
        for d in range(-(h - 1), h + 1):
            if d == 0:
                continue
            if d not in rolled:
                rolled[d] = pltpu.roll(b3, d % SUBLANES, axis=1)
            r3 = jnp.where(pos == d + h - 1, rolled[d], r3)
        out[h] = r3.reshape(c, w)
    return out


def _hgrn_kernel(q_ref, f_ref, i_ref, lb_ref, lvl_ref, o_ref, st_ref, *, chunk, layer):
    @pl.when(pl.program_id(2) == 0)
    def _():
        st_ref[...] = jnp.zeros_like(st_ref)

    c = chunk
    n_lvl = int(math.log2(c))
    ts = q_ref.shape[0]
    logits = lb_ref[...]
    e = jnp.exp(logits - jnp.max(logits, axis=0, keepdims=True))
    lb = jnp.sum(e[0:layer + 1, :], axis=0, keepdims=True) / jnp.sum(e, axis=0, keepdims=True)
    lvl = lvl_ref[...]
    row = lax.broadcasted_iota(jnp.int32, (c, HG_DK), 0)
    st = st_ref[...]

    for ci in range(ts // c):
        rows = pl.ds(ci * c, c)
        qraw = q_ref[rows, :]
        q = qraw * _sigmoid(qraw)
        f = lb + (1.0 - lb) * _sigmoid(f_ref[rows, :])
        k = 1.0 - f
        v = i_ref[rows, :].astype(BF16)

        b = _prefix_sum_rows(jnp.log2(f), row)
        b_last = b[c - 1:c, :]
        mids = _midpoint_rows(b, [c >> (l + 1) for l in range(n_lvl)])

        a = lax.dot_general(q.astype(BF16), k.astype(BF16), NT_DIMS, preferred_element_type=F32)
        a = jnp.where(lvl == -1, a, 0.0)
        for l in range(n_lvl):
            e_l = jnp.exp2(-jnp.abs(b - mids[c >> (l + 1)]))
            p = lax.dot_general((q * e_l).astype(BF16), (k * e_l).astype(BF16), NT_DIMS,
                                preferred_element_type=F32)
            a = jnp.where(lvl == l, p, a)

        o = _dot(a.astype(BF16), v)
        o += lax.dot_general((q * jnp.exp2(b)).astype(BF16), st.astype(BF16), NT_DIMS,
                             preferred_element_type=F32)
        o_ref[rows, :] = o
        upd = lax.dot_general(v, (k * jnp.exp2(b_last - b)).astype(BF16), TN_DIMS,
                              preferred_element_type=F32)
        st = st * jnp.exp2(b_last) + upd

    st_ref[...] = st


def _hgrn2(proj_a, lb_logits, layer, batch, ts=512, chunk=128):
    m = proj_a.shape[0]
    ns = m // batch // ts
    level = _hgrn_level_table(chunk)
    nl = lb_logits.shape[0]
    row = lambda b, h, s: b * ns + s
    return pl.pallas_call(
        functools.partial(_hgrn_kernel, chunk=chunk, layer=layer),
        grid=(batch, HG_HEADS, ns),
        in_specs=[
            pl.BlockSpec((ts, HG_DK), lambda b, h, s: (row(b, h, s), h)),
            pl.BlockSpec((ts, HG_DK), lambda b, h, s: (row(b, h, s), HG_HEADS + h)),
            pl.BlockSpec((ts, HG_DK), lambda b, h, s: (row(b, h, s), 2 * HG_HEADS + h)),
            pl.BlockSpec((nl, HG_DK), lambda b, h, s: (0, h)),
            pl.BlockSpec(level.shape, lambda b, h, s: (0, 0)),
        ],
        out_specs=pl.BlockSpec((ts, HG_DK), lambda b, h, s: (row(b, h, s), h)),
        out_shape=jax.ShapeDtypeStruct((m, HG_HEADS * HG_DK), F32),
        scratch_shapes=[pltpu.VMEM((HG_DK, HG_DK), F32)],
        compiler_params=_params(("parallel", "parallel", "arbitrary")),
        name="hgrn2",
    )(proj_a, proj_a, proj_a, lb_logits.reshape(nl, HG_HEADS * HG_DK), jnp.asarray(level))


LOG2E = 1.4426950408889634
FOX_FOLD = 8


def _head_column(c_blk, h):
    lane = lax.broadcasted_iota(jnp.int32, c_blk.shape, 1)
    return jnp.sum(jnp.where(lane == h, c_blk, 0.0), axis=1, keepdims=True)


def _decay_features(c_col, key_side):
    rows = c_col.shape[0]
    lane = lax.broadcasted_iota(jnp.int32, (rows, LANES), 1)
    x = jnp.broadcast_to(-c_col if key_side else c_col, (rows, LANES))
    hi, mid, lo = _split3(x)
    base = 0 if key_side else 3
    ones = (lane >= 3 - base) & (lane < 6 - base)
    feat = jnp.where(lane == base, hi.astype(F32),
                     jnp.where(lane == base + 1, mid.astype(F32),
                               jnp.where(lane == base + 2, lo.astype(F32),
                                         jnp.where(ones, 1.0, 0.0))))
    return feat.astype(BF16)


def _fox_kernel(q_ref, k_ref, v_ref, cq_ref, ck_ref, o_ref, kx_ref, vt_ref, qx_ref, s0_ref, s1_ref,
                m_ref, l_ref, acc_ref, *, tk):
    h = pl.program_id(1)
    qi = pl.program_id(2)
    tq = q_ref.shape[0]
    nk = kx_ref.shape[0]

    @pl.when(qi == 0)
    def _():
        def setup(kb, carry):
            rows = pl.ds(pl.multiple_of(kb * tk, tk), tk)
            kx_ref[kb, :, 0:FOX_DH] = k_ref[rows, :]
            kx_ref[kb, :, FOX_DH:2 * FOX_DH] = _decay_features(
                _head_column(ck_ref[rows, :], h) * LOG2E, True)
            vt_ref[kb] = v_ref[rows, :].T
            return carry

        lax.fori_loop(0, nk, setup, 0)

    qx_ref[:, 0:FOX_DH] = q_ref[...]
    qx_ref[:, FOX_DH:2 * FOX_DH] = _decay_features(_head_column(cq_ref[...], h) * LOG2E, False)
    m_ref[...] = jnp.full_like(m_ref, -jnp.inf)
    l_ref[...] = jnp.zeros_like(l_ref)
    acc_ref[...] = jnp.zeros_like(acc_ref)

    def scores(kb, s_ref):
        s_ref[...] = lax.dot_general(kx_ref[kb], qx_ref[...], NT_DIMS,
                                     preferred_element_type=F32)

    def fold(x, op):
        part = op(x.reshape(FOX_FOLD, tk // FOX_FOLD, tq), axis=0)
        return op(part, axis=0, keepdims=True)

    def update(kb, s_ref, masked):
        st = s_ref[...]
        if masked:
            key = lax.broadcasted_iota(jnp.int32, st.shape, 0)
            qry = lax.broadcasted_iota(jnp.int32, st.shape, 1)
            st = jnp.where(key <= qry, st, -jnp.inf)
        m_prev = m_ref[...]
        m_new = jnp.maximum(m_prev, fold(st, jnp.max))
        alpha = jnp.exp2(m_prev - m_new)
        p = jnp.exp2(st - m_new)
        l_ref[...] = alpha * l_ref[...] + fold(p, jnp.sum)
        acc_ref[...] = alpha * acc_ref[...] + _dot(vt_ref[kb], p.astype(BF16))
        m_ref[...] = m_new

    scores(0, s0_ref)

    def body(j, carry):
        kb = 2 * j
        scores(kb + 1, s1_ref)
        update(kb, s0_ref, False)
        scores(kb + 2, s0_ref)
        update(kb + 1, s1_ref, False)
        return carry

    lax.fori_loop(0, qi // 2, body, 0)

    @pl.when(qi % 2 == 1)
    def _():
        scores(qi, s1_ref)
        update(qi - 1, s0_ref, False)
        update(qi, s1_ref, True)

    @pl.when(qi % 2 == 0)
    def _():
        update(qi, s0_ref, True)

    o_ref[...] = (acc_ref[...] / l_ref[...]).T.astype(o_ref.dtype)


def _fox(proj_b, c, batch, seq, tq=512):
    m = proj_b.shape[0]
    nq = seq // tq
    return pl.pallas_call(
        functools.partial(_fox_kernel, tk=tq),
        grid=(batch, FOX_HEADS, nq),
        in_specs=[
            pl.BlockSpec((tq, FOX_DH), lambda b, h, i: (b * nq + i, h)),
            pl.BlockSpec((seq, FOX_DH), lambda b, h, i: (b, FOX_HEADS + h)),
            pl.BlockSpec((seq, FOX_DH), lambda b, h, i: (b, 2 * FOX_HEADS + h)),
            pl.BlockSpec((tq, LANES), lambda b, h, i: (b * nq + i, 0)),
            pl.BlockSpec((seq, LANES), lambda b, h, i: (b, 0)),
        ],
        out_specs=pl.BlockSpec((tq, FOX_DH), lambda b, h, i: (b * nq + i, h)),
        out_shape=jax.ShapeDtypeStruct((m, FOX_HEADS * FOX_DH), BF16),
        scratch_shapes=[
            pltpu.VMEM((nq, tq, 2 * FOX_DH), BF16),
            pltpu.VMEM((nq, FOX_DH, tq), BF16),
            pltpu.VMEM((tq, 2 * FOX_DH), BF16),
            pltpu.VMEM((tq, tq), F32),
            pltpu.VMEM((tq, tq), F32),
            pltpu.VMEM((1, tq), F32),
            pltpu.VMEM((1, tq), F32),
            pltpu.VMEM((FOX_DH, tq), F32),
        ],
        compiler_params=_params(("parallel", "parallel", "arbitrary")),
        name="fox_attention",
    )(proj_b, proj_b, proj_b, c, c)


def _merge_kernel(x_ref, oa_ref, ga_ref, ob_ref, g0_ref, g1_ref, bg0_ref, bg1_ref, hg_ref,
                  wa_ref, wb_ref, wo_ref, post_ref, o_ref):
    ga = ga_ref[...]
    oa = _rms(oa_ref[...], hg_ref[...]) * (ga * _sigmoid(ga))
    ya = _dot(oa.astype(BF16), wa_ref[...])
    yb = _dot(ob_ref[...], wb_ref[...])
    y = _sigmoid(g0_ref[...] + bg0_ref[...]) * ya + _sigmoid(g1_ref[...] + bg1_ref[...]) * yb
    z = _dot(y.astype(BF16), wo_ref[...])
    o_ref[...] = x_ref[...] + _rms(z, post_ref[...])


def _merge(x, o_a, proj_a, o_b, gates, b_gate, hg_norm_g, w_a, w_b, w_o, post_g, tm=512):
    m, d = x.shape
    row = lambda c: pl.BlockSpec((tm, d), lambda i: (i, c))
    vec = lambda c: pl.BlockSpec((1, d), lambda i: (0, c))
    mat = pl.BlockSpec((d, d), lambda i: (0, 0))
    return pl.pallas_call(
        _merge_kernel,
        grid=(m // tm,),
        in_specs=[row(0), row(0), row(3), row(0), row(0), row(1), vec(0), vec(1), vec(0),
                  mat, mat, mat, vec(0)],
        out_specs=row(0),
        out_shape=jax.ShapeDtypeStruct((m, d), F32),
        compiler_params=_params(("parallel",)),
        name="branch_merge",
    )(x, o_a, proj_a, o_b, gates, gates, b_gate, b_gate, hg_norm_g.reshape(1, d),
      w_a, w_b, w_o, post_g.reshape(1, d))


def _memattn_kernel(x_ref, pre_ref, kv_ref, wq_ref, wo_ref, post_ref, o_ref):
    x = x_ref[...]
    h = _rms(x, pre_ref[...]).astype(BF16)
    q = _dot(h, wq_ref[...]).astype(BF16)
    heads = []
    for hd in range(MEM_HEADS):
        lo = hd * MEM_DH
        kh = kv_ref[:, lo:lo + MEM_DH]
        vh = kv_ref[:, D_MODEL + lo:D_MODEL + lo + MEM_DH]
        s = lax.dot_general(q[:, lo:lo + MEM_DH], kh, NT_DIMS, preferred_element_type=F32)
        p = jnp.exp(s - jnp.max(s, axis=1, keepdims=True))
        p = p / jnp.sum(p, axis=1, keepdims=True)
        heads.append(_dot(p.astype(BF16), vh).astype(BF16))
    o = jnp.concatenate(heads, axis=1)
    o_ref[...] = x + _rms(_dot(o, wo_ref[...]), post_ref[...])


def _memattn(x, pre_g, kv, w_q, w_o, post_g, batch, tm=512):
    m, d = x.shape
    ns = m // batch // tm
    vec = pl.BlockSpec((1, d), lambda b, s: (0, 0))
    mat = pl.BlockSpec((d, d), lambda b, s: (0, 0))
    return pl.pallas_call(
        _memattn_kernel,
        grid=(batch, ns),
        in_specs=[
            pl.BlockSpec((tm, d), lambda b, s: (b * ns + s, 0)),
            vec,
            pl.BlockSpec((MEM_LEN, 2 * d), lambda b, s: (b, 0)),
            mat, mat, vec,
        ],
        out_specs=pl.BlockSpec((tm, d), lambda b, s: (b * ns + s, 0)),
        out_shape=jax.ShapeDtypeStruct((m, d), F32),
        compiler_params=_params(("parallel", "parallel")),
        name="mem_cross_attention",
    )(x, pre_g.reshape(1, d), kv, w_q, w_o, post_g.reshape(1, d))


def kernel(x, mem, ffn1_pre_g, ffn1_w_in, ffn1_w_down, ffn1_post_g, mix_pre_g, w_in, hg_lb_logits, hg_norm_g, fox_f_bias, w_branch_a, w_branch_b, b_gate, w_out, mix_post_g, mem_pre_g, mem_kv_g, w_mq, w_mkv, w_mo, mem_post_g, ffn2_pre_g, ffn2_w_in, ffn2_w_down, ffn2_post_g):
    batch, seq, d = x.shape
    depth = ffn1_w_in.shape[0]
    xf = x.reshape(batch * seq, d)
    memf = mem.reshape(batch * MEM_LEN, d)
    kw = HG_HEADS * HG_DK
    off_b = 4 * kw
    off_f = off_b + 3 * FOX_HEADS * FOX_DH
    off_g = off_f + FOX_HEADS
    for l in range(depth):
        xf = _ffn(xf, ffn1_pre_g[l], ffn1_w_in[l].astype(BF16), ffn1_w_down[l].astype(BF16),
                  ffn1_post_g[l])

        w = w_in[l]
        w_a = w[:, :off_b].astype(BF16)
        q_scale = jnp.concatenate([jnp.full((FOX_HEADS * FOX_DH,), LOG2E / math.sqrt(FOX_DH), F32),
                                   jnp.ones((2 * FOX_HEADS * FOX_DH,), F32)])
        w_b = (w[:, off_b:off_f] * q_scale).astype(BF16)
        w_f = jnp.pad(w[:, off_f:off_g], ((0, 0), (0, LANES - FOX_HEADS))).astype(BF16)
        w_g = w[:, off_g:].astype(BF16)
        proj_a = _norm_matmul(xf, mix_pre_g[l], w_a, F32, 1024, 1024)
        proj_b = _norm_matmul(xf, mix_pre_g[l], w_b, BF16, 1024, 1024)
        fb = _norm_matmul(xf, mix_pre_g[l], w_f, F32, 1024, LANES)
        gates = _norm_matmul(xf, mix_pre_g[l], w_g, F32, 1024, 1024)

        bias = jnp.pad(fox_f_bias[l], (0, LANES - FOX_HEADS)).reshape(1, LANES)
        c = _fcum(fb, bias, batch)

        o_a = _hgrn2(proj_a, hg_lb_logits, l, batch)
        o_b = _fox(proj_b, c, batch, seq)

        xf = _merge(xf, o_a, proj_a, o_b, gates, b_gate[l].reshape(1, 2 * d), hg_norm_g[l],
                    w_branch_a[l].astype(BF16), w_branch_b[l].astype(BF16),
                    w_out[l].astype(BF16), mix_post_g[l])

        kv = _norm_matmul(memf, mem_kv_g[l], w_mkv[l].astype(BF16), BF16, MEM_LEN, 1024)
        xf = _memattn(xf, mem_pre_g[l], kv, (w_mq[l] * (1.0 / math.sqrt(MEM_DH))).astype(BF16),
                      w_mo[l].astype(BF16), mem_post_g[l], batch)

        xf = _ffn(xf, ffn2_pre_g[l], ffn2_w_in[l].astype(BF16), ffn2_w_down[l].astype(BF16),
                  ffn2_post_g[l])
    return xf.reshape(batch, seq, d)
```

```python
import functools
import math

import jax
import jax.numpy as jnp
import numpy as np
from jax import lax
from jax.experimental import pallas as pl
from jax.experimental.pallas import tpu as pltpu

F32 = jnp.float32
BF16 = jnp.bfloat16

D_MODEL = 1024
HG_HEADS = 8
HG_DK = 128
FOX_HEADS = 8
FOX_DH = 128
MEM_LEN = 256
MEM_HEADS = 4
MEM_DH = D_MODEL // MEM_HEADS
D_FF = 2816
EPS = 1e-6
LANES = 128

VMEM_LIMIT = 56 * 1024 * 1024

NT_DIMS = (((1,), (1,)), ((), ()))
TN_DIMS = (((0,), (0,)), ((), ()))


def _params(sem):
    return pltpu.CompilerParams(dimension_semantics=sem, vmem_limit_bytes=VMEM_LIMIT)


def _rms(x, g):
    ms = jnp.mean(x * x, axis=-1, keepdims=True)
    return x * lax.rsqrt(ms + EPS) * g


def _sigmoid(x):
    return 0.5 * jnp.tanh(0.5 * x) + 0.5


def _dot(a, b):
    return jnp.dot(a, b, preferred_element_type=F32)


def _norm_matmul_kernel(x_ref, g_ref, w_ref, o_ref, h_ref):
    @pl.when(pl.program_id(1) == 0)
    def _():
        h_ref[...] = _rms(x_ref[...], g_ref[...]).astype(BF16)

    o_ref[...] = _dot(h_ref[...], w_ref[...]).astype(o_ref.dtype)


def _norm_matmul(x, g, w, out_dtype, tm, tn):
    m, d = x.shape
    n = w.shape[1]
    return pl.pallas_call(
        _norm_matmul_kernel,
        grid=(m // tm, n // tn),
        in_specs=[
            pl.BlockSpec((tm, d), lambda i, j: (i, 0)),
            pl.BlockSpec((1, d), lambda i, j: (0, 0)),
            pl.BlockSpec((d, tn), lambda i, j: (0, j)),
        ],
        out_specs=pl.BlockSpec((tm, tn), lambda i, j: (i, j)),
        out_shape=jax.ShapeDtypeStruct((m, n), out_dtype),
        scratch_shapes=[pltpu.VMEM((tm, d), BF16)],
        compiler_params=_params(("parallel", "arbitrary")),
        name="norm_matmul",
    )(x, g.reshape(1, d), w)


def _resident(shape):
    return pl.BlockSpec(shape, lambda *_: (0,) * len(shape), pipeline_mode=pl.Buffered(1))


def _inproj_kernel(x_ref, g_ref, w_ref, *out_refs, tn):
    h = _rms(x_ref[...], g_ref[...]).astype(BF16)
    off = 0
    for o_ref in out_refs:
        n = o_ref.shape[1]
        for j in range(0, n, tn):
            width = min(tn, n - j)
            o_ref[:, j:j + width] = _dot(h, w_ref[:, off + j:off + j + width]).astype(o_ref.dtype)
        off += n


def _inproj(x, g, w, widths, dtypes, tm=256, tn=512):
    m, d = x.shape
    return pl.pallas_call(
        functools.partial(_inproj_kernel, tn=tn),
        grid=(m // tm,),
        in_specs=[pl.BlockSpec((tm, d), lambda i: (i, 0)), _resident((1, d)), _resident(w.shape)],
        out_specs=[pl.BlockSpec((tm, n), lambda i: (i, 0)) for n in widths],
        out_shape=[jax.ShapeDtypeStruct((m, n), dt) for n, dt in zip(widths, dtypes)],
        compiler_params=_params(("parallel",)),
        name="input_projection",
    )(x, g.reshape(1, d), w)


def _ffn_kernel(x_ref, pre_ref, w_in_ref, w_down_ref, post_ref, o_ref, acc_ref, *, tf):
    x = x_ref[...]
    h = _rms(x, pre_ref[...]).astype(BF16)
    for f in range(D_FF // tf):
        cols = slice(f * tf, (f + 1) * tf)
        gate = _dot(h, w_in_ref[:, cols])
        up = _dot(h, w_in_ref[:, D_FF + f * tf:D_FF + (f + 1) * tf])
        act = (gate * _sigmoid(gate) * up).astype(BF16)
        part = _dot(act, w_down_ref[cols, :])
        if f == 0:
            acc_ref[...] = part
        else:
            acc_ref[...] += part
    o_ref[...] = x + 0.5 * _rms(acc_ref[...], post_ref[...])


def _ffn(x, pre_g, w_in, w_down, post_g, tm=512, tf=256):
    m, d = x.shape
    return pl.pallas_call(
        functools.partial(_ffn_kernel, tf=tf),
        grid=(m // tm,),
        in_specs=[
            pl.BlockSpec((tm, d), lambda i: (i, 0)),
            _resident((1, d)),
            _resident(w_in.shape),
            _resident(w_down.shape),
            _resident((1, d)),
        ],
        out_specs=pl.BlockSpec((tm, d), lambda i: (i, 0)),
        out_shape=jax.ShapeDtypeStruct((m, d), F32),
        scratch_shapes=[pltpu.VMEM((tm, d), F32)],
        compiler_params=_params(("parallel",)),
        name="ffn",
    )(x, pre_g.reshape(1, d), w_in, w_down, post_g.reshape(1, d))


def _split3(x):
    hi = x.astype(BF16)
    r = x - hi.astype(F32)
    mid = r.astype(BF16)
    lo = (r - mid.astype(F32)).astype(BF16)
    return hi, mid, lo


def _fcum_kernel(fb_ref, bias_ref, c_ref, carry_ref):
    @pl.when(pl.program_id(1) == 0)
    def _():
        carry_ref[...] = jnp.zeros_like(carry_ref)

    z = fb_ref[...] + bias_ref[...]
    ls = jnp.minimum(z, 0.0) - jnp.log(1.0 + jnp.exp(-jnp.abs(z)))
    tb = z.shape[0]
    row = lax.broadcasted_iota(jnp.int32, (tb, tb), 0)
    col = lax.broadcasted_iota(jnp.int32, (tb, tb), 1)
    tril = (col <= row).astype(BF16)
    hi, mid, lo = _split3(ls)
    c = _dot(tril, hi) + _dot(tril, mid) + _dot(tril, lo) + carry_ref[...]
    c_ref[...] = c
    carry_ref[...] = c[tb - 1:tb, :]


def _fcum(fb, bias, batch, tb=512):
    m = fb.shape[0]
    ns = m // batch // tb
    return pl.pallas_call(
        _fcum_kernel,
        grid=(batch, ns),
        in_specs=[
            pl.BlockSpec((tb, LANES), lambda b, s: (b * ns + s, 0)),
            pl.BlockSpec((1, LANES), lambda b, s: (0, 0)),
        ],
        out_specs=pl.BlockSpec((tb, LANES), lambda b, s: (b * ns + s, 0)),
        out_shape=jax.ShapeDtypeStruct((m, LANES), F32),
        scratch_shapes=[pltpu.VMEM((1, LANES), F32)],
        compiler_params=_params(("parallel", "arbitrary")),
        name="fox_decay_cumsum",
    )(fb, bias)


def _hgrn_level_table(c):
    t = np.arange(c)[:, None]
    s = np.arange(c)[None, :]
    level = np.full((c, c), -2, np.int32)
    level[t == s] = -1
    for l in range(int(math.log2(c))):
        h = c >> (l + 1)
        mid = (t // (2 * h)) * (2 * h) + h
        sel = (t // (2 * h) == s // (2 * h)) & (t >= mid) & (s < mid)
        level[sel & (level == -2)] = l
    return level


def _prefix_sum_rows(x, row):
    k = 1
    while k < x.shape[0]:
        x = x + jnp.where(row >= k, pltpu.roll(x, k, axis=0), 0.0)
        k *= 2
    return x


SUBLANES = 8


def _midpoint_rows(b, halves):
    c, w = b.shape
    out = {}
    b3 = b.reshape(c // SUBLANES, SUBLANES, w)
    sub = lax.broadcasted_iota(jnp.int32, (1, SUBLANES, w), 1)
    rolled = {0: b3}
    for h in halves:
        if h >= SUBLANES:
            pieces = [jnp.broadcast_to(b[j * 2 * h + h - 1:j * 2 * h + h, :], (2 * h, w))
                      for j in range(c // (2 * h))]
            out[h] = pieces[0] if len(pieces) == 1 else jnp.concatenate(pieces, axis=0)
            continue
        pos = jnp.bitwise_and(sub, 2 * h - 1)
        r3 = b3
        for d in range(-(h - 1), h + 1):
            if d == 0:
                continue
            if d not in rolled:
                rolled[d] = pltpu.roll(b3, d % SUBLANES, axis=1)
            r3 = jnp.where(pos == d + h - 1, rolled[d], r3)
        out[h] = r3.reshape(c, w)
    return out


def _hgrn_kernel(q_ref, f_ref, i_ref, lb_ref, lvl_ref, o_ref, st_ref, *, chunk, layer):
    @pl.when(pl.program_id(2) == 0)
    def _():
        st_ref[...] = jnp.zeros_like(st_ref)

    c = chunk
    n_lvl = int(math.log2(c))
    ts = q_ref.shape[0]
    logits = lb_ref[...]
    e = jnp.exp(logits - jnp.max(logits, axis=0, keepdims=True))
    lb = jnp.sum(e[0:layer + 1, :], axis=0, keepdims=True) / jnp.sum(e, axis=0, keepdims=True)
    lvl = lvl_ref[...]
    row = lax.broadcasted_iota(jnp.int32, (c, HG_DK), 0)
    st = st_ref[...]

    for ci in range(ts // c):
        rows = pl.ds(ci * c, c)
        qraw = q_ref[rows, :]
        q = qraw * _sigmoid(qraw)
        f = lb + (1.0 - lb) * _sigmoid(f_ref[rows, :])
        k = 1.0 - f
        v = i_ref[rows, :].astype(BF16)

        b = _prefix_sum_rows(jnp.log2(f), row)
        b_last = b[c - 1:c, :]
        mids = _midpoint_rows(b, [c >> (l + 1) for l in range(n_lvl)])

        a = lax.dot_general(q.astype(BF16), k.astype(BF16), NT_DIMS, preferred_element_type=F32)
        a = jnp.where(lvl == -1, a, 0.0)
        for l in range(n_lvl):
            e_l = jnp.exp2(-jnp.abs(b - mids[c >> (l + 1)]))
            p = lax.dot_general((q * e_l).astype(BF16), (k * e_l).astype(BF16), NT_DIMS,
                                preferred_element_type=F32)
            a = jnp.where(lvl == l, p, a)

        o = _dot(a.astype(BF16), v)
        o += lax.dot_general((q * jnp.exp2(b)).astype(BF16), st.astype(BF16), NT_DIMS,
                             preferred_element_type=F32)
        o_ref[rows, :] = o
        upd = lax.dot_general(v, (k * jnp.exp2(b_last - b)).astype(BF16), TN_DIMS,
                              preferred_element_type=F32)
        st = st * jnp.exp2(b_last) + upd

    st_ref[...] = st


def _hgrn2(proj_a, lb_logits, layer, batch, ts=512, chunk=128):
    m = proj_a.shape[0]
    ns = m // batch // ts
    level = _hgrn_level_table(chunk)
    nl = lb_logits.shape[0]
    row = lambda b, h, s: b * ns + s
    return pl.pallas_call(
        functools.partial(_hgrn_kernel, chunk=chunk, layer=layer),
        grid=(batch, HG_HEADS, ns),
        in_specs=[
            pl.BlockSpec((ts, HG_DK), lambda b, h, s: (row(b, h, s), h)),
            pl.BlockSpec((ts, HG_DK), lambda b, h, s: (row(b, h, s), HG_HEADS + h)),
            pl.BlockSpec((ts, HG_DK), lambda b, h, s: (row(b, h, s), 2 * HG_HEADS + h)),
            pl.BlockSpec((nl, HG_DK), lambda b, h, s: (0, h)),
            pl.BlockSpec(level.shape, lambda b, h, s: (0, 0)),
        ],
        out_specs=pl.BlockSpec((ts, HG_DK), lambda b, h, s: (row(b, h, s), h)),
        out_shape=jax.ShapeDtypeStruct((m, HG_HEADS * HG_DK), F32),
        scratch_shapes=[pltpu.VMEM((HG_DK, HG_DK), F32)],
        compiler_params=_params(("parallel", "parallel", "arbitrary")),
        name="hgrn2",
    )(proj_a, proj_a, proj_a, lb_logits.reshape(nl, HG_HEADS * HG_DK), jnp.asarray(level))


LOG2E = 1.4426950408889634
FOX_FOLD = 8


def _head_column(c_blk, h):
    lane = lax.broadcasted_iota(jnp.int32, c_blk.shape, 1)
    return jnp.sum(jnp.where(lane == h, c_blk, 0.0), axis=1, keepdims=True)


def _decay_features(c_col, key_side):
    rows = c_col.shape[0]
    lane = lax.broadcasted_iota(jnp.int32, (rows, LANES), 1)
    x = jnp.broadcast_to(-c_col if key_side else c_col, (rows, LANES))
    hi, mid, lo = _split3(x)
    base = 0 if key_side else 3
    ones = (lane >= 3 - base) & (lane < 6 - base)
    feat = jnp.where(lane == base, hi.astype(F32),
                     jnp.where(lane == base + 1, mid.astype(F32),
                               jnp.where(lane == base + 2, lo.astype(F32),
                                         jnp.where(ones, 1.0, 0.0))))
    return feat.astype(BF16)


def _fox_kernel(q_ref, k_ref, v_ref, cq_ref, ck_ref, o_ref, kx_ref, vt_ref, qxt_ref,
                s0_ref, s1_ref, m_ref, l_ref, acc_ref, *, tk):
    h = pl.program_id(1)
    qi = pl.program_id(2)
    tq = q_ref.shape[0]
    nk = kx_ref.shape[0]

    @pl.when(qi == 0)
    def _():
        def setup(kb, carry):
            rows = pl.ds(pl.multiple_of(kb * tk, tk), tk)
            kx_ref[kb, :, 0:FOX_DH] = k_ref[rows, :]
            kx_ref[kb, :, FOX_DH:2 * FOX_DH] = _decay_features(
                _head_column(ck_ref[rows, :], h) * LOG2E, True)
            vt_ref[kb] = v_ref[rows, :].T
            return carry

        lax.fori_loop(0, nk, setup, 0)

    qx = jnp.concatenate(
        [q_ref[...], _decay_features(_head_column(cq_ref[...], h) * LOG2E, False)], axis=1)
    qxt_ref[...] = qx.T
    m_ref[...] = jnp.full_like(m_ref, -jnp.inf)
    l_ref[...] = jnp.zeros_like(l_ref)
    acc_ref[...] = jnp.zeros_like(acc_ref)

    def scores(kb, s_ref):
        s_ref[...] = _dot(kx_ref[kb], qxt_ref[...])

    def fold(x, op):
        part = op(x.reshape(FOX_FOLD, tk // FOX_FOLD, tq), axis=0)
        return op(part, axis=0, keepdims=True)

    def update(kb, s_ref, masked):
        st = s_ref[...]
        if masked:
            key = lax.broadcasted_iota(jnp.int32, st.shape, 0)
            qry = lax.broadcasted_iota(jnp.int32, st.shape, 1)
            st = jnp.where(key <= qry, st, -jnp.inf)
        m_prev = m_ref[...]
        m_new = jnp.maximum(m_prev, fold(st, jnp.max))
        alpha = jnp.exp2(m_prev - m_new)
        p = jnp.exp2(st - m_new)
        l_ref[...] = alpha * l_ref[...] + fold(p, jnp.sum)
        m_ref[...] = m_new
        acc_ref[...] = alpha * acc_ref[...] + _dot(vt_ref[kb], p.astype(BF16))

    scores(0, s0_ref)

    def pair(kb):
        scores(kb + 1, s1_ref)
        update(kb, s0_ref, False)
        scores(kb + 2, s0_ref)
        update(kb + 1, s1_ref, False)

    def body4(j, carry):
        pair(4 * j)
        pair(4 * j + 2)
        return carry

    def body2(j, carry):
        pair(2 * j)
        return carry

    n_pairs = qi // 2
    lax.fori_loop(0, n_pairs // 2, body4, 0)
    lax.fori_loop(2 * (n_pairs // 2), n_pairs, body2, 0)

    @pl.when(qi % 2 == 1)
    def _():
        scores(qi, s1_ref)
        update(qi - 1, s0_ref, False)
        update(qi, s1_ref, True)

    @pl.when(qi % 2 == 0)
    def _():
        update(qi, s0_ref, True)

    o_ref[...] = (acc_ref[...] / l_ref[...]).T.astype(o_ref.dtype)


def _fox(proj_b, c, batch, seq, tq=512):
    m = proj_b.shape[0]
    nq = seq // tq
    return pl.pallas_call(
        functools.partial(_fox_kernel, tk=tq),
        grid=(batch, FOX_HEADS, nq),
        in_specs=[
            pl.BlockSpec((tq, FOX_DH), lambda b, h, i: (b * nq + i, h)),
            pl.BlockSpec((seq, FOX_DH), lambda b, h, i: (b, FOX_HEADS + h)),
            pl.BlockSpec((seq, FOX_DH), lambda b, h, i: (b, 2 * FOX_HEADS + h)),
            pl.BlockSpec((tq, LANES), lambda b, h, i: (b * nq + i, 0)),
            pl.BlockSpec((seq, LANES), lambda b, h, i: (b, 0)),
        ],
        out_specs=pl.BlockSpec((tq, FOX_DH), lambda b, h, i: (b * nq + i, h)),
        out_shape=jax.ShapeDtypeStruct((m, FOX_HEADS * FOX_DH), BF16),
        scratch_shapes=[
            pltpu.VMEM((nq, tq, 2 * FOX_DH), BF16),
            pltpu.VMEM((nq, FOX_DH, tq), BF16),
            pltpu.VMEM((2 * FOX_DH, tq), BF16),
            pltpu.VMEM((tq, tq), F32),
            pltpu.VMEM((tq, tq), F32),
            pltpu.VMEM((1, tq), F32),
            pltpu.VMEM((1, tq), F32),
            pltpu.VMEM((FOX_DH, tq), F32),
        ],
        compiler_params=_params(("parallel", "parallel", "arbitrary")),
        name="fox_attention",
    )(proj_b, proj_b, proj_b, c, c)


def _merge_kernel(x_ref, oa_ref, ga_ref, ob_ref, g0_ref, g1_ref, bg0_ref, bg1_ref, hg_ref,
                  wa_ref, wb_ref, wo_ref, post_ref, o_ref):
    ga = ga_ref[...]
    oa = _rms(oa_ref[...], hg_ref[...]) * (ga * _sigmoid(ga))
    ya = _dot(oa.astype(BF16), wa_ref[...])
    yb = _dot(ob_ref[...], wb_ref[...])
    y = _sigmoid(g0_ref[...] + bg0_ref[...]) * ya + _sigmoid(g1_ref[...] + bg1_ref[...]) * yb
    z = _dot(y.astype(BF16), wo_ref[...])
    o_ref[...] = x_ref[...] + _rms(z, post_ref[...])


def _merge(x, o_a, proj_a, o_b, gates, b_gate, hg_norm_g, w_a, w_b, w_o, post_g, tm=512):
    m, d = x.shape
    row = lambda c: pl.BlockSpec((tm, d), lambda i: (i, c))
    vec = lambda c: pl.BlockSpec((1, d), lambda i: (0, c))
    mat = pl.BlockSpec((d, d), lambda i: (0, 0))
    return pl.pallas_call(
        _merge_kernel,
        grid=(m // tm,),
        in_specs=[row(0), row(0), row(3), row(0), row(0), row(1), vec(0), vec(1), vec(0),
                  mat, mat, mat, vec(0)],
        out_specs=row(0),
        out_shape=jax.ShapeDtypeStruct((m, d), F32),
        compiler_params=_params(("parallel",)),
        name="branch_merge",
    )(x, o_a, proj_a, o_b, gates, gates, b_gate, b_gate, hg_norm_g.reshape(1, d),
      w_a, w_b, w_o, post_g.reshape(1, d))


def _memattn_kernel(x_ref, pre_ref, kv_ref, wq_ref, wo_ref, post_ref, o_ref):
    x = x_ref[...]
    h = _rms(x, pre_ref[...]).astype(BF16)
    q = _dot(h, wq_ref[...]).astype(BF16)
    heads = []
    for hd in range(MEM_HEADS):
        lo = hd * MEM_DH
        kh = kv_ref[:, lo:lo + MEM_DH]
        vh = kv_ref[:, D_MODEL + lo:D_MODEL + lo + MEM_DH]
        s = lax.dot_general(q[:, lo:lo + MEM_DH], kh, NT_DIMS, preferred_element_type=F32)
        p = jnp.exp(s - jnp.max(s, axis=1, keepdims=True))
        p = p / jnp.sum(p, axis=1, keepdims=True)
        heads.append(_dot(p.astype(BF16), vh).astype(BF16))
    o = jnp.concatenate(heads, axis=1)
    o_ref[...] = x + _rms(_dot(o, wo_ref[...]), post_ref[...])


def _memattn(x, pre_g, kv, w_q, w_o, post_g, batch, tm=512):
    m, d = x.shape
    ns = m // batch // tm
    vec = pl.BlockSpec((1, d), lambda b, s: (0, 0))
    mat = pl.BlockSpec((d, d), lambda b, s: (0, 0))
    return pl.pallas_call(
        _memattn_kernel,
        grid=(batch, ns),
        in_specs=[
            pl.BlockSpec((tm, d), lambda b, s: (b * ns + s, 0)),
            vec,
            pl.BlockSpec((MEM_LEN, 2 * d), lambda b, s: (b, 0)),
            mat, mat, vec,
        ],
        out_specs=pl.BlockSpec((tm, d), lambda b, s: (b * ns + s, 0)),
        out_shape=jax.ShapeDtypeStruct((m, d), F32),
        compiler_params=_params(("parallel", "parallel")),
        name="mem_cross_attention",
    )(x, pre_g.reshape(1, d), kv, w_q, w_o, post_g.reshape(1, d))


def kernel(x, mem, ffn1_pre_g, ffn1_w_in, ffn1_w_down, ffn1_post_g, mix_pre_g, w_in, hg_lb_logits, hg_norm_g, fox_f_bias, w_branch_a, w_branch_b, b_gate, w_out, mix_post_g, mem_pre_g, mem_kv_g, w_mq, w_mkv, w_mo, mem_post_g, ffn2_pre_g, ffn2_w_in, ffn2_w_down, ffn2_post_g):
    batch, seq, d = x.shape
    depth = ffn1_w_in.shape[0]
    xf = x.reshape(batch * seq, d)
    memf = mem.reshape(batch * MEM_LEN, d)
    kw = HG_HEADS * HG_DK
    off_b = 4 * kw
    off_f = off_b + 3 * FOX_HEADS * FOX_DH
    off_g = off_f + FOX_HEADS
    for l in range(depth):
        xf = _ffn(xf, ffn1_pre_g[l], ffn1_w_in[l].astype(BF16), ffn1_w_down[l].astype(BF16),
                  ffn1_post_g[l])

        w = w_in[l]
        fox_w = FOX_HEADS * FOX_DH
        w_cat = jnp.concatenate([
            w[:, :off_b],
            w[:, off_b:off_b + fox_w] * (LOG2E / math.sqrt(FOX_DH)),
            w[:, off_b + fox_w:off_f],
            jnp.pad(w[:, off_f:off_g], ((0, 0), (0, LANES - FOX_HEADS))),
            w[:, off_g:],
        ], axis=1).astype(BF16)
        proj_a, proj_b, fb, gates = _inproj(
            xf, mix_pre_g[l], w_cat, (off_b, 3 * fox_w, LANES, 2 * d), (F32, BF16, F32, F32))

        bias = jnp.pad(fox_f_bias[l], (0, LANES - FOX_HEADS)).reshape(1, LANES)
        c = _fcum(fb, bias, batch)

        o_a = _hgrn2(proj_a, hg_lb_logits, l, batch)
        o_b = _fox(proj_b, c, batch, seq)

        xf = _merge(xf, o_a, proj_a, o_b, gates, b_gate[l].reshape(1, 2 * d), hg_norm_g[l],
                    w_branch_a[l].astype(BF16), w_branch_b[l].astype(BF16),
                    w_out[l].astype(BF16), mix_post_g[l])

        kv = _norm_matmul(memf, mem_kv_g[l], w_mkv[l].astype(BF16), BF16, MEM_LEN, 1024)
        xf = _memattn(xf, mem_pre_g[l], kv, (w_mq[l] * (1.0 / math.sqrt(MEM_DH))).astype(BF16),
                      w_mo[l].astype(BF16), mem_post_g[l], batch)

        xf = _ffn(xf, ffn2_pre_g[l], ffn2_w_in[l].astype(BF16), ffn2_w_down[l].astype(BF16),
                  ffn2_post_g[l])
    return xf.reshape(batch, seq, d)
```

```python
import functools
import math

import jax
import jax.numpy as jnp
import numpy as np
from jax import lax
from jax.experimental import pallas as pl
from jax.experimental.pallas import tpu as pltpu

F32 = jnp.float32
BF16 = jnp.bfloat16

D_MODEL = 1024
HG_HEADS = 8
HG_DK = 128
FOX_HEADS = 8
FOX_DH = 128
MEM_LEN = 256
MEM_HEADS = 4
MEM_DH = D_MODEL // MEM_HEADS
D_FF = 2816
EPS = 1e-6
LANES = 128

VMEM_LIMIT = 56 * 1024 * 1024

NT_DIMS = (((1,), (1,)), ((), ()))
TN_DIMS = (((0,), (0,)), ((), ()))


def _params(sem):
    return pltpu.CompilerParams(dimension_semantics=sem, vmem_limit_bytes=VMEM_LIMIT)


def _rms(x, g):
    ms = jnp.mean(x * x, axis=-1, keepdims=True)
    return x * lax.rsqrt(ms + EPS) * g


def _sigmoid(x):
    return 0.5 * jnp.tanh(0.5 * x) + 0.5


def _dot(a, b):
    return jnp.dot(a, b, preferred_element_type=F32)


def _norm_matmul_kernel(x_ref, g_ref, w_ref, o_ref, h_ref):
    @pl.when(pl.program_id(1) == 0)
    def _():
        h_ref[...] = _rms(x_ref[...], g_ref[...]).astype(BF16)

    o_ref[...] = _dot(h_ref[...], w_ref[...]).astype(o_ref.dtype)


def _norm_matmul(x, g, w, out_dtype, tm, tn):
    m, d = x.shape
    n = w.shape[1]
    return pl.pallas_call(
        _norm_matmul_kernel,
        grid=(m // tm, n // tn),
        in_specs=[
            pl.BlockSpec((tm, d), lambda i, j: (i, 0)),
            pl.BlockSpec((1, d), lambda i, j: (0, 0)),
            pl.BlockSpec((d, tn), lambda i, j: (0, j)),
        ],
        out_specs=pl.BlockSpec((tm, tn), lambda i, j: (i, j)),
        out_shape=jax.ShapeDtypeStruct((m, n), out_dtype),
        scratch_shapes=[pltpu.VMEM((tm, d), BF16)],
        compiler_params=_params(("parallel", "arbitrary")),
        name="norm_matmul",
    )(x, g.reshape(1, d), w)


def _resident(shape):
    return pl.BlockSpec(shape, lambda *_: (0,) * len(shape), pipeline_mode=pl.Buffered(1))


def _inproj_kernel(x_ref, g_ref, w_ref, *out_refs, tn):
    h = _rms(x_ref[...], g_ref[...]).astype(BF16)
    off = 0
    for o_ref in out_refs:
        n = o_ref.shape[1]
        for j in range(0, n, tn):
            width = min(tn, n - j)
            o_ref[:, j:j + width] = _dot(h, w_ref[:, off + j:off + j + width]).astype(o_ref.dtype)
        off += n


def _inproj(x, g, w, widths, dtypes, tm=256, tn=512):
    m, d = x.shape
    return pl.pallas_call(
        functools.partial(_inproj_kernel, tn=tn),
        grid=(m // tm,),
        in_specs=[pl.BlockSpec((tm, d), lambda i: (i, 0)), _resident((1, d)), _resident(w.shape)],
        out_specs=[pl.BlockSpec((tm, n), lambda i: (i, 0)) for n in widths],
        out_shape=[jax.ShapeDtypeStruct((m, n), dt) for n, dt in zip(widths, dtypes)],
        compiler_params=_params(("parallel",)),
        name="input_projection",
    )(x, g.reshape(1, d), w)


def _ffn_kernel(x_ref, pre_ref, w_in_ref, w_down_ref, post_ref, o_ref, acc_ref, *, tf):
    x = x_ref[...]
    h = _rms(x, pre_ref[...]).astype(BF16)
    for f in range(D_FF // tf):
        cols = slice(f * tf, (f + 1) * tf)
        gate = _dot(h, w_in_ref[:, cols])
        up = _dot(h, w_in_ref[:, D_FF + f * tf:D_FF + (f + 1) * tf])
        act = (gate * _sigmoid(gate) * up).astype(BF16)
        part = _dot(act, w_down_ref[cols, :])
        if f == 0:
            acc_ref[...] = part
        else:
            acc_ref[...] += part
    o_ref[...] = x + 0.5 * _rms(acc_ref[...], post_ref[...])


def _ffn(x, pre_g, w_in, w_down, post_g, tm=512, tf=256):
    m, d = x.shape
    return pl.pallas_call(
        functools.partial(_ffn_kernel, tf=tf),
        grid=(m // tm,),
        in_specs=[
            pl.BlockSpec((tm, d), lambda i: (i, 0)),
            _resident((1, d)),
            _resident(w_in.shape),
            _resident(w_down.shape),
            _resident((1, d)),
        ],
        out_specs=pl.BlockSpec((tm, d), lambda i: (i, 0)),
        out_shape=jax.ShapeDtypeStruct((m, d), F32),
        scratch_shapes=[pltpu.VMEM((tm, d), F32)],
        compiler_params=_params(("parallel",)),
        name="ffn",
    )(x, pre_g.reshape(1, d), w_in, w_down, post_g.reshape(1, d))


def _split3(x):
    hi = x.astype(BF16)
    r = x - hi.astype(F32)
    mid = r.astype(BF16)
    lo = (r - mid.astype(F32)).astype(BF16)
    return hi, mid, lo


def _fcum_kernel(fb_ref, bias_ref, c_ref, carry_ref):
    @pl.when(pl.program_id(1) == 0)
    def _():
        carry_ref[...] = jnp.zeros_like(carry_ref)

    z = fb_ref[...] + bias_ref[...]
    ls = jnp.minimum(z, 0.0) - jnp.log(1.0 + jnp.exp(-jnp.abs(z)))
    tb = z.shape[0]
    row = lax.broadcasted_iota(jnp.int32, (tb, tb), 0)
    col = lax.broadcasted_iota(jnp.int32, (tb, tb), 1)
    tril = (col <= row).astype(BF16)
    hi, mid, lo = _split3(ls)
    c = _dot(tril, hi) + _dot(tril, mid) + _dot(tril, lo) + carry_ref[...]
    c_ref[...] = c
    carry_ref[...] = c[tb - 1:tb, :]


def _fcum(fb, bias, batch, tb=512):
    m = fb.shape[0]
    ns = m // batch // tb
    return pl.pallas_call(
        _fcum_kernel,
        grid=(batch, ns),
        in_specs=[
            pl.BlockSpec((tb, LANES), lambda b, s: (b * ns + s, 0)),
            pl.BlockSpec((1, LANES), lambda b, s: (0, 0)),
        ],
        out_specs=pl.BlockSpec((tb, LANES), lambda b, s: (b * ns + s, 0)),
        out_shape=jax.ShapeDtypeStruct((m, LANES), F32),
        scratch_shapes=[pltpu.VMEM((1, LANES), F32)],
        compiler_params=_params(("parallel", "arbitrary")),
        name="fox_decay_cumsum",
    )(fb, bias)


def _hgrn_level_table(c):
    t = np.arange(c)[:, None]
    s = np.arange(c)[None, :]
    level = np.full((c, c), -2, np.int32)
    level[t == s] = -1
    for l in range(int(math.log2(c))):
        h = c >> (l + 1)
        mid = (t // (2 * h)) * (2 * h) + h
        sel = (t // (2 * h) == s // (2 * h)) & (t >= mid) & (s < mid)
        level[sel & (level == -2)] = l
    return level


def _prefix_sum_rows(x, row):
    k = 1
    while k < x.shape[0]:
        x = x + jnp.where(row >= k, pltpu.roll(x, k, axis=0), 0.0)
        k *= 2
    return x


SUBLANES = 8


def _midpoint_rows(b, halves):
    c, w = b.shape
    out = {}
    groups = c // SUBLANES
    b3 = b.reshape(groups, SUBLANES, w)
    sub = lax.broadcasted_iota(jnp.int32, (1, SUBLANES, w), 1)
    for h in halves:
        if h >= SUBLANES:
            pieces = [jnp.broadcast_to(b[j * 2 * h + h - 1:j * 2 * h + h, :], (2 * h, w))
                      for j in range(c // (2 * h))]
            out[h] = pieces[0] if len(pieces) == 1 else jnp.concatenate(pieces, axis=0)
            continue
        r3 = None
        for j in range(SUBLANES // (2 * h)):
            src = j * 2 * h + h - 1
            piece = jnp.broadcast_to(b3[:, src:src + 1, :], (groups, SUBLANES, w))
            r3 = piece if r3 is None else jnp.where(sub >= j * 2 * h, piece, r3)
        out[h] = r3.reshape(c, w)
    return out


def _hgrn_kernel(q_ref, f_ref, i_ref, lb_ref, lvl_ref, o_ref, st_ref, *, chunk, layer):
    @pl.when(pl.program_id(2) == 0)
    def _():
        st_ref[...] = jnp.zeros_like(st_ref)

    c = chunk
    n_lvl = int(math.log2(c))
    ts = q_ref.shape[0]
    logits = lb_ref[...]
    e = jnp.exp(logits - jnp.max(logits, axis=0, keepdims=True))
    lb = jnp.sum(e[0:layer + 1, :], axis=0, keepdims=True) / jnp.sum(e, axis=0, keepdims=True)
    lvl = lvl_ref[...]
    row = lax.broadcasted_iota(jnp.int32, (c, HG_DK), 0)
    st = st_ref[...]

    for ci in range(ts // c):
        rows = pl.ds(ci * c, c)
        qraw = q_ref[rows, :]
        q = qraw * _sigmoid(qraw)
        f = lb + (1.0 - lb) * _sigmoid(f_ref[rows, :])
        k = 1.0 - f
        v = i_ref[rows, :].astype(BF16)

        b = _prefix_sum_rows(jnp.log2(f), row)
        b_last = b[c - 1:c, :]
        mids = _midpoint_rows(b, [c >> (l + 1) for l in range(n_lvl)])

        q16 = q.astype(BF16)
        k16 = k.astype(BF16)
        a = lax.dot_general(q16, k16, NT_DIMS, preferred_element_type=F32)
        a = jnp.where(lvl == -1, a, 0.0)
        for l in range(n_lvl):
            e_l = jnp.exp2(-jnp.abs(b - mids[c >> (l + 1)])).astype(BF16)
            p = lax.dot_general(q16 * e_l, k16 * e_l, NT_DIMS, preferred_element_type=F32)
            a = jnp.where(lvl == l, p, a)

        o = _dot(a.astype(BF16), v)
        o += lax.dot_general((q * jnp.exp2(b)).astype(BF16), st.astype(BF16), NT_DIMS,
                             preferred_element_type=F32)
        o_ref[rows, :] = o
        upd = lax.dot_general(v, (k * jnp.exp2(b_last - b)).astype(BF16), TN_DIMS,
                              preferred_element_type=F32)
        st = st * jnp.exp2(b_last) + upd

    st_ref[...] = st


def _hgrn2(proj_a, lb_logits, layer, batch, ts=512, chunk=128):
    m = proj_a.shape[0]
    ns = m // batch // ts
    level = _hgrn_level_table(chunk)
    nl = lb_logits.shape[0]
    row = lambda b, h, s: b * ns + s
    return pl.pallas_call(
        functools.partial(_hgrn_kernel, chunk=chunk, layer=layer),
        grid=(batch, HG_HEADS, ns),
        in_specs=[
            pl.BlockSpec((ts, HG_DK), lambda b, h, s: (row(b, h, s), h)),
            pl.BlockSpec((ts, HG_DK), lambda b, h, s: (row(b, h, s), HG_HEADS + h)),
            pl.BlockSpec((ts, HG_DK), lambda b, h, s: (row(b, h, s), 2 * HG_HEADS + h)),
            pl.BlockSpec((nl, HG_DK), lambda b, h, s: (0, h)),
            pl.BlockSpec(level.shape, lambda b, h, s: (0, 0)),
        ],
        out_specs=pl.BlockSpec((ts, HG_DK), lambda b, h, s: (row(b, h, s), h)),
        out_shape=jax.ShapeDtypeStruct((m, HG_HEADS * HG_DK), F32),
        scratch_shapes=[pltpu.VMEM((HG_DK, HG_DK), F32)],
        compiler_params=_params(("parallel", "parallel", "arbitrary")),
        name="hgrn2",
    )(proj_a, proj_a, proj_a, lb_logits.reshape(nl, HG_HEADS * HG_DK), jnp.asarray(level))


LOG2E = 1.4426950408889634
FOX_FOLD = 8


def _head_column(c_blk, h):
    lane = lax.broadcasted_iota(jnp.int32, c_blk.shape, 1)
    return jnp.sum(jnp.where(lane == h, c_blk, 0.0), axis=1, keepdims=True)


def _decay_features(c_col):
    rows = c_col.shape[0]
    lane = lax.broadcasted_iota(jnp.int32, (rows, LANES), 1)
    hi, mid, lo = (part.astype(F32) for part in _split3(jnp.broadcast_to(c_col, (rows, LANES))))

    def place(base, sign):
        ones = (lane >= 3 - base) & (lane < 6 - base)
        feat = jnp.where(lane == base, sign * hi,
                         jnp.where(lane == base + 1, sign * mid,
                                   jnp.where(lane == base + 2, sign * lo,
                                             jnp.where(ones, 1.0, 0.0))))
        return feat.astype(BF16)

    return place(3, 1.0), place(0, -1.0)


def _fox_kernel(q_ref, k_ref, v_ref, c_ref, o_ref, kx_ref, qf_ref, vt_ref, qxt_ref,
                s0_ref, s1_ref, x0_ref, x1_ref, m_ref, l_ref, acc_ref, *, tk):
    h = pl.program_id(1)
    qi = pl.program_id(2)
    tq = q_ref.shape[0]
    nk = kx_ref.shape[0]

    @pl.when(qi == 0)
    def _():
        def setup(kb, carry):
            rows = pl.ds(pl.multiple_of(kb * tk, tk), tk)
            q_feat, k_feat = _decay_features(_head_column(c_ref[rows, :], h) * LOG2E)
            kx_ref[kb, :, 0:FOX_DH] = k_ref[rows, :]
            kx_ref[kb, :, FOX_DH:2 * FOX_DH] = k_feat
            qf_ref[kb] = q_feat
            vt_ref[kb] = v_ref[rows, :].T
            return carry

        lax.fori_loop(0, nk, setup, 0)

    qxt_ref[...] = jnp.concatenate([q_ref[...], qf_ref[qi]], axis=1).T
    m_ref[...] = jnp.full_like(m_ref, -jnp.inf)
    l_ref[...] = jnp.zeros_like(l_ref)
    acc_ref[...] = jnp.zeros_like(acc_ref)

    def fold(x, op):
        part = op(x.reshape(FOX_FOLD, tk // FOX_FOLD, tq), axis=0)
        return op(part, axis=0, keepdims=True)

    def scores(kb, s_ref, smax_ref):
        st = _dot(kx_ref[kb], qxt_ref[...])
        s_ref[...] = st
        smax_ref[...] = fold(st, jnp.max)

    def update(kb, s_ref, smax_ref, masked):
        st = s_ref[...]
        if masked:
            key = lax.broadcasted_iota(jnp.int32, st.shape, 0)
            qry = lax.broadcasted_iota(jnp.int32, st.shape, 1)
            st = jnp.where(key <= qry, st, -jnp.inf)
            blk_max = fold(st, jnp.max)
        else:
            blk_max = smax_ref[...]
        m_prev = m_ref[...]
        m_new = jnp.maximum(m_prev, blk_max)
        alpha = jnp.exp2(m_prev - m_new)
        p = jnp.exp2(st - m_new)
        l_ref[...] = alpha * l_ref[...] + fold(p, jnp.sum)
        m_ref[...] = m_new
        acc_ref[...] = alpha * acc_ref[...] + _dot(vt_ref[kb], p.astype(BF16))

    scores(0, s0_ref, x0_ref)

    def pair(kb):
        scores(kb + 1, s1_ref, x1_ref)
        update(kb, s0_ref, x0_ref, False)
        scores(kb + 2, s0_ref, x0_ref)
        update(kb + 1, s1_ref, x1_ref, False)

    def body4(j, carry):
        pair(4 * j)
        pair(4 * j + 2)
        return carry

    def body2(j, carry):
        pair(2 * j)
        return carry

    n_pairs = qi // 2
    lax.fori_loop(0, n_pairs // 2, body4, 0)
    lax.fori_loop(2 * (n_pairs // 2), n_pairs, body2, 0)

    @pl.when(qi % 2 == 1)
    def _():
        scores(qi, s1_ref, x1_ref)
        update(qi - 1, s0_ref, x0_ref, False)
        update(qi, s1_ref, x1_ref, True)

    @pl.when(qi % 2 == 0)
    def _():
        update(qi, s0_ref, x0_ref, True)

    o_ref[...] = (acc_ref[...] / l_ref[...]).T.astype(o_ref.dtype)


def _fox(proj_b, c, batch, seq, tq=512):
    m = proj_b.shape[0]
    nq = seq // tq
    return pl.pallas_call(
        functools.partial(_fox_kernel, tk=tq),
        grid=(batch, FOX_HEADS, nq),
        in_specs=[
            pl.BlockSpec((tq, FOX_DH), lambda b, h, i: (b * nq + i, h)),
            pl.BlockSpec((seq, FOX_DH), lambda b, h, i: (b, FOX_HEADS + h)),
            pl.BlockSpec((seq, FOX_DH), lambda b, h, i: (b, 2 * FOX_HEADS + h)),
            pl.BlockSpec((seq, LANES), lambda b, h, i: (b, 0)),
        ],
        out_specs=pl.BlockSpec((tq, FOX_DH), lambda b, h, i: (b * nq + i, h)),
        out_shape=jax.ShapeDtypeStruct((m, FOX_HEADS * FOX_DH), BF16),
        scratch_shapes=[
            pltpu.VMEM((nq, tq, 2 * FOX_DH), BF16),
            pltpu.VMEM((nq, tq, FOX_DH), BF16),
            pltpu.VMEM((nq, FOX_DH, tq), BF16),
            pltpu.VMEM((2 * FOX_DH, tq), BF16),
            pltpu.VMEM((tq, tq), F32),
            pltpu.VMEM((tq, tq), F32),
            pltpu.VMEM((1, tq), F32),
            pltpu.VMEM((1, tq), F32),
            pltpu.VMEM((1, tq), F32),
            pltpu.VMEM((1, tq), F32),
            pltpu.VMEM((FOX_DH, tq), F32),
        ],
        compiler_params=_params(("parallel", "parallel", "arbitrary")),
        name="fox_attention",
    )(proj_b, proj_b, proj_b, c)


def _merge_kernel(x_ref, oa_ref, ga_ref, ob_ref, g0_ref, g1_ref, bg0_ref, bg1_ref, hg_ref,
                  wa_ref, wb_ref, wo_ref, post_ref, o_ref):
    ga = ga_ref[...]
    oa = _rms(oa_ref[...], hg_ref[...]) * (ga * _sigmoid(ga))
    ya = _dot(oa.astype(BF16), wa_ref[...])
    yb = _dot(ob_ref[...], wb_ref[...])
    y = _sigmoid(g0_ref[...] + bg0_ref[...]) * ya + _sigmoid(g1_ref[...] + bg1_ref[...]) * yb
    z = _dot(y.astype(BF16), wo_ref[...])
    o_ref[...] = x_ref[...] + _rms(z, post_ref[...])


def _merge(x, o_a, proj_a, o_b, gates, b_gate, hg_norm_g, w_a, w_b, w_o, post_g, tm=512):
    m, d = x.shape
    row = lambda c: pl.BlockSpec((tm, d), lambda i: (i, c))
    vec = lambda c: pl.BlockSpec((1, d), lambda i: (0, c))
    mat = pl.BlockSpec((d, d), lambda i: (0, 0))
    return pl.pallas_call(
        _merge_kernel,
        grid=(m // tm,),
        in_specs=[row(0), row(0), row(3), row(0), row(0), row(1), vec(0), vec(1), vec(0),
                  mat, mat, mat, vec(0)],
        out_specs=row(0),
        out_shape=jax.ShapeDtypeStruct((m, d), F32),
        compiler_params=_params(("parallel",)),
        name="branch_merge",
    )(x, o_a, proj_a, o_b, gates, gates, b_gate, b_gate, hg_norm_g.reshape(1, d),
      w_a, w_b, w_o, post_g.reshape(1, d))


def _memattn_kernel(x_ref, pre_ref, kv_ref, wq_ref, wo_ref, post_ref, o_ref):
    x = x_ref[...]
    h = _rms(x, pre_ref[...]).astype(BF16)
    q = _dot(h, wq_ref[...]).astype(BF16)
    heads = []
    for hd in range(MEM_HEADS):
        lo = hd * MEM_DH
        kh = kv_ref[:, lo:lo + MEM_DH]
        vh = kv_ref[:, D_MODEL + lo:D_MODEL + lo + MEM_DH]
        s = lax.dot_general(q[:, lo:lo + MEM_DH], kh, NT_DIMS, preferred_element_type=F32)
        p = jnp.exp(s - jnp.max(s, axis=1, keepdims=True))
        p = p / jnp.sum(p, axis=1, keepdims=True)
        heads.append(_dot(p.astype(BF16), vh).astype(BF16))
    o = jnp.concatenate(heads, axis=1)
    o_ref[...] = x + _rms(_dot(o, wo_ref[...]), post_ref[...])


def _memattn(x, pre_g, kv, w_q, w_o, post_g, batch, tm=512):
    m, d = x.shape
    ns = m // batch // tm
    vec = pl.BlockSpec((1, d), lambda b, s: (0, 0))
    mat = pl.BlockSpec((d, d), lambda b, s: (0, 0))
    return pl.pallas_call(
        _memattn_kernel,
        grid=(batch, ns),
        in_specs=[
            pl.BlockSpec((tm, d), lambda b, s: (b * ns + s, 0)),
            vec,
            pl.BlockSpec((MEM_LEN, 2 * d), lambda b, s: (b, 0)),
            mat, mat, vec,
        ],
        out_specs=pl.BlockSpec((tm, d), lambda b, s: (b * ns + s, 0)),
        out_shape=jax.ShapeDtypeStruct((m, d), F32),
        compiler_params=_params(("parallel", "parallel")),
        name="mem_cross_attention",
    )(x, pre_g.reshape(1, d), kv, w_q, w_o, post_g.reshape(1, d))


def kernel(x, mem, ffn1_pre_g, ffn1_w_in, ffn1_w_down, ffn1_post_g, mix_pre_g, w_in, hg_lb_logits, hg_norm_g, fox_f_bias, w_branch_a, w_branch_b, b_gate, w_out, mix_post_g, mem_pre_g, mem_kv_g, w_mq, w_mkv, w_mo, mem_post_g, ffn2_pre_g, ffn2_w_in, ffn2_w_down, ffn2_post_g):
    batch, seq, d = x.shape
    depth = ffn1_w_in.shape[0]
    xf = x.reshape(batch * seq, d)
    memf = mem.reshape(batch * MEM_LEN, d)
    kw = HG_HEADS * HG_DK
    off_b = 4 * kw
    off_f = off_b + 3 * FOX_HEADS * FOX_DH
    off_g = off_f + FOX_HEADS
    for l in range(depth):
        xf = _ffn(xf, ffn1_pre_g[l], ffn1_w_in[l].astype(BF16), ffn1_w_down[l].astype(BF16),
                  ffn1_post_g[l])

        w = w_in[l]
        fox_w = FOX_HEADS * FOX_DH
        w_cat = jnp.concatenate([
            w[:, :off_b],
            w[:, off_b:off_b + fox_w] * (LOG2E / math.sqrt(FOX_DH)),
            w[:, off_b + fox_w:off_f],
            jnp.pad(w[:, off_f:off_g], ((0, 0), (0, LANES - FOX_HEADS))),
            w[:, off_g:],
        ], axis=1).astype(BF16)
        proj_a, proj_b, fb, gates = _inproj(
            xf, mix_pre_g[l], w_cat, (off_b, 3 * fox_w, LANES, 2 * d), (F32, BF16, F32, F32))

        bias = jnp.pad(fox_f_bias[l], (0, LANES - FOX_HEADS)).reshape(1, LANES)
        c = _fcum(fb, bias, batch)

        o_a = _hgrn2(proj_a, hg_lb_logits, l, batch)
        o_b = _fox(proj_b, c, batch, seq)

        xf = _merge(xf, o_a, proj_a, o_b, gates, b_gate[l].reshape(1, 2 * d), hg_norm_g[l],
                    w_branch_a[l].astype(BF16), w_branch_b[l].astype(BF16),
                    w_out[l].astype(BF16), mix_post_g[l])

        kv = _norm_matmul(memf, mem_kv_g[l], w_mkv[l].astype(BF16), BF16, MEM_LEN, 1024)
        xf = _memattn(xf, mem_pre_g[l], kv, (w_mq[l] * (1.0 / math.sqrt(MEM_DH))).astype(BF16),
                      w_mo[l].astype(BF16), mem_post_g[l], batch)

        xf = _ffn(xf, ffn2_pre_g[l], ffn2_w_in[l].astype(BF16), ffn2_w_down[l].astype(BF16),
                  ffn2_post_g[l])
    return xf.reshape(batch, seq, d)
```

```python
import functools
import math

import jax
import jax.numpy as jnp
import numpy as np
from jax import lax
from jax.experimental import pallas as pl
from jax.experimental.pallas import tpu as pltpu

F32 = jnp.float32
BF16 = jnp.bfloat16

D_MODEL = 1024
HG_HEADS = 8
HG_DK = 128
FOX_HEADS = 8
FOX_DH = 128
MEM_LEN = 256
MEM_HEADS = 4
MEM_DH = D_MODEL // MEM_HEADS
D_FF = 2816
EPS = 1e-6
LANES = 128

VMEM_LIMIT = 56 * 1024 * 1024

NT_DIMS = (((1,), (1,)), ((), ()))
TN_DIMS = (((0,), (0,)), ((), ()))


def _params(sem):
    return pltpu.CompilerParams(dimension_semantics=sem, vmem_limit_bytes=VMEM_LIMIT)


def _rms(x, g):
    ms = jnp.mean(x * x, axis=-1, keepdims=True)
    return x * lax.rsqrt(ms + EPS) * g


def _sigmoid(x):
    return 0.5 * jnp.tanh(0.5 * x) + 0.5


def _dot(a, b):
    return jnp.dot(a, b, preferred_element_type=F32)


def _norm_matmul_kernel(x_ref, g_ref, w_ref, o_ref, h_ref):
    @pl.when(pl.program_id(1) == 0)
    def _():
        h_ref[...] = _rms(x_ref[...], g_ref[...]).astype(BF16)

    o_ref[...] = _dot(h_ref[...], w_ref[...]).astype(o_ref.dtype)


def _norm_matmul(x, g, w, out_dtype, tm, tn):
    m, d = x.shape
    n = w.shape[1]
    return pl.pallas_call(
        _norm_matmul_kernel,
        grid=(m // tm, n // tn),
        in_specs=[
            pl.BlockSpec((tm, d), lambda i, j: (i, 0)),
            pl.BlockSpec((1, d), lambda i, j: (0, 0)),
            pl.BlockSpec((d, tn), lambda i, j: (0, j)),
        ],
        out_specs=pl.BlockSpec((tm, tn), lambda i, j: (i, j)),
        out_shape=jax.ShapeDtypeStruct((m, n), out_dtype),
        scratch_shapes=[pltpu.VMEM((tm, d), BF16)],
        compiler_params=_params(("parallel", "arbitrary")),
        name="norm_matmul",
    )(x, g.reshape(1, d), w)


def _resident(shape):
    return pl.BlockSpec(shape, lambda *_: (0,) * len(shape), pipeline_mode=pl.Buffered(1))


def _inproj_kernel(x_ref, g_ref, w_ref, bias_ref, *out_refs, tn, acts):
    h = _rms(x_ref[...], g_ref[...]).astype(BF16)
    off = 0
    for o_ref, act in zip(out_refs, acts):
        n = o_ref.shape[1]
        for j in range(0, n, tn):
            width = min(tn, n - j)
            y = _dot(h, w_ref[:, off + j:off + j + width])
            if act == "silu":
                y = y * _sigmoid(y)
            elif act == "sigmoid_bias":
                y = _sigmoid(y + bias_ref[:, j:j + width])
            o_ref[:, j:j + width] = y.astype(o_ref.dtype)
        off += n


def _inproj(x, g, w, bias, widths, dtypes, acts, tm=512, tn=512):
    m, d = x.shape
    return pl.pallas_call(
        functools.partial(_inproj_kernel, tn=tn, acts=acts),
        grid=(m // tm,),
        in_specs=[pl.BlockSpec((tm, d), lambda i: (i, 0)), _resident((1, d)), _resident(w.shape),
                  _resident(bias.shape)],
        out_specs=[pl.BlockSpec((tm, n), lambda i: (i, 0)) for n in widths],
        out_shape=[jax.ShapeDtypeStruct((m, n), dt) for n, dt in zip(widths, dtypes)],
        compiler_params=_params(("parallel",)),
        name="input_projection",
    )(x, g.reshape(1, d), w, bias)


def _ffn_kernel(x_ref, pre_ref, w_in_ref, w_down_ref, post_ref, o_ref, acc_ref, *, tf):
    x = x_ref[...]
    h = _rms(x, pre_ref[...]).astype(BF16)
    for f in range(D_FF // tf):
        cols = slice(f * tf, (f + 1) * tf)
        gate = _dot(h, w_in_ref[:, cols])
        up = _dot(h, w_in_ref[:, D_FF + f * tf:D_FF + (f + 1) * tf])
        act = (gate * _sigmoid(gate) * up).astype(BF16)
        part = _dot(act, w_down_ref[cols, :])
        if f == 0:
            acc_ref[...] = part
        else:
            acc_ref[...] += part
    o_ref[...] = x + 0.5 * _rms(acc_ref[...], post_ref[...])


def _ffn(x, pre_g, w_in, w_down, post_g, tm=512, tf=256):
    m, d = x.shape
    return pl.pallas_call(
        functools.partial(_ffn_kernel, tf=tf),
        grid=(m // tm,),
        in_specs=[
            pl.BlockSpec((tm, d), lambda i: (i, 0)),
            _resident((1, d)),
            _resident(w_in.shape),
            _resident(w_down.shape),
            _resident((1, d)),
        ],
        out_specs=pl.BlockSpec((tm, d), lambda i: (i, 0)),
        out_shape=jax.ShapeDtypeStruct((m, d), F32),
        scratch_shapes=[pltpu.VMEM((tm, d), F32)],
        compiler_params=_params(("parallel",)),
        name="ffn",
    )(x, pre_g.reshape(1, d), w_in, w_down, post_g.reshape(1, d))


def _split3(x):
    hi = x.astype(BF16)
    r = x - hi.astype(F32)
    mid = r.astype(BF16)
    lo = (r - mid.astype(F32)).astype(BF16)
    return hi, mid, lo


def _fcum_kernel(fb_ref, bias_ref, c_ref, carry_ref):
    @pl.when(pl.program_id(1) == 0)
    def _():
        carry_ref[...] = jnp.zeros_like(carry_ref)

    z = fb_ref[...] + bias_ref[...]
    ls = jnp.minimum(z, 0.0) - jnp.log(1.0 + jnp.exp(-jnp.abs(z)))
    tb = z.shape[0]
    row = lax.broadcasted_iota(jnp.int32, (tb, tb), 0)
    col = lax.broadcasted_iota(jnp.int32, (tb, tb), 1)
    tril = (col <= row).astype(BF16)
    hi, mid, lo = _split3(ls)
    c = _dot(tril, hi) + _dot(tril, mid) + _dot(tril, lo) + carry_ref[...]
    c_ref[...] = c
    carry_ref[...] = c[tb - 1:tb, :]


def _fcum(fb, bias, batch, tb=512):
    m = fb.shape[0]
    ns = m // batch // tb
    return pl.pallas_call(
        _fcum_kernel,
        grid=(batch, ns),
        in_specs=[
            pl.BlockSpec((tb, LANES), lambda b, s: (b * ns + s, 0)),
            pl.BlockSpec((1, LANES), lambda b, s: (0, 0)),
        ],
        out_specs=pl.BlockSpec((tb, LANES), lambda b, s: (b * ns + s, 0)),
        out_shape=jax.ShapeDtypeStruct((m, LANES), F32),
        scratch_shapes=[pltpu.VMEM((1, LANES), F32)],
        compiler_params=_params(("parallel", "arbitrary")),
        name="fox_decay_cumsum",
    )(fb, bias)


def _hgrn_level_table(c):
    t = np.arange(c)[:, None]
    s = np.arange(c)[None, :]
    level = np.full((c, c), -2, np.int32)
    level[t == s] = -1
    for l in range(int(math.log2(c))):
        h = c >> (l + 1)
        mid = (t // (2 * h)) * (2 * h) + h
        sel = (t // (2 * h) == s // (2 * h)) & (t >= mid) & (s < mid)
        level[sel & (level == -2)] = l
    return level


def _prefix_sum_rows(x, row):
    k = 1
    while k < x.shape[0]:
        x = x + jnp.where(row >= k, pltpu.roll(x, k, axis=0), 0.0)
        k *= 2
    return x


SUBLANES = 8


def _midpoint_rows(b, halves):
    c, w = b.shape
    out = {}
    groups = c // SUBLANES
    b3 = b.reshape(groups, SUBLANES, w)
    sub = lax.broadcasted_iota(jnp.int32, (1, SUBLANES, w), 1)
    for h in halves:
        if h >= SUBLANES:
            pieces = [jnp.broadcast_to(b[j * 2 * h + h - 1:j * 2 * h + h, :], (2 * h, w))
                      for j in range(c // (2 * h))]
            out[h] = pieces[0] if len(pieces) == 1 else jnp.concatenate(pieces, axis=0)
            continue
        r3 = None
        for j in range(SUBLANES // (2 * h)):
            src = j * 2 * h + h - 1
            piece = jnp.broadcast_to(b3[:, src:src + 1, :], (groups, SUBLANES, w))
            r3 = piece if r3 is None else jnp.where(sub >= j * 2 * h, piece, r3)
        out[h] = r3.reshape(c, w)
    return out


def _hgrn_kernel(q_ref, f_ref, i_ref, lb_ref, lvl_ref, o_ref, st_ref, *, chunk, layer):
    @pl.when(pl.program_id(2) == 0)
    def _():
        st_ref[...] = jnp.zeros_like(st_ref)

    c = chunk
    n_lvl = int(math.log2(c))
    ts = q_ref.shape[0]
    logits = lb_ref[...]
    e = jnp.exp(logits - jnp.max(logits, axis=0, keepdims=True))
    lb = jnp.sum(e[0:layer + 1, :], axis=0, keepdims=True) / jnp.sum(e, axis=0, keepdims=True)
    lvl = lvl_ref[...]
    row = lax.broadcasted_iota(jnp.int32, (c, HG_DK), 0)
    st = st_ref[...]

    for ci in range(ts // c):
        rows = pl.ds(ci * c, c)
        q16 = q_ref[rows, :]
        f = lb + (1.0 - lb) * _sigmoid(f_ref[rows, :])
        k = 1.0 - f
        v = i_ref[rows, :]

        b = _prefix_sum_rows(jnp.log2(f), row)
        b_last = b[c - 1:c, :]
        mids = _midpoint_rows(b, [c >> (l + 1) for l in range(n_lvl)])

        k16 = k.astype(BF16)
        a = lax.dot_general(q16, k16, NT_DIMS, preferred_element_type=F32)
        a = jnp.where(lvl == -1, a, 0.0)
        for l in range(n_lvl):
            e_l = jnp.exp2(-jnp.abs(b - mids[c >> (l + 1)])).astype(BF16)
            p = lax.dot_general(q16 * e_l, k16 * e_l, NT_DIMS, preferred_element_type=F32)
            a = jnp.where(lvl == l, p, a)

        o = _dot(a.astype(BF16), v)
        o += lax.dot_general(q16 * jnp.exp2(b).astype(BF16), st.astype(BF16), NT_DIMS,
                             preferred_element_type=F32)
        o_ref[rows, :] = o.astype(o_ref.dtype)
        upd = lax.dot_general(v, (k * jnp.exp2(b_last - b)).astype(BF16), TN_DIMS,
                              preferred_element_type=F32)
        st = st * jnp.exp2(b_last) + upd

    st_ref[...] = st


def _hgrn2(q_act, f_logit, inp, lb_logits, layer, batch, ts=512, chunk=128):
    m = q_act.shape[0]
    ns = m // batch // ts
    level = _hgrn_level_table(chunk)
    nl = lb_logits.shape[0]
    blk = pl.BlockSpec((ts, HG_DK), lambda b, h, s: (b * ns + s, h))
    return pl.pallas_call(
        functools.partial(_hgrn_kernel, chunk=chunk, layer=layer),
        grid=(batch, HG_HEADS, ns),
        in_specs=[
            blk, blk, blk,
            pl.BlockSpec((nl, HG_DK), lambda b, h, s: (0, h)),
            pl.BlockSpec(level.shape, lambda b, h, s: (0, 0)),
        ],
        out_specs=blk,
        out_shape=jax.ShapeDtypeStruct((m, HG_HEADS * HG_DK), BF16),
        scratch_shapes=[pltpu.VMEM((HG_DK, HG_DK), F32)],
        compiler_params=_params(("parallel", "parallel", "arbitrary")),
        name="hgrn2",
    )(q_act, f_logit, inp, lb_logits.reshape(nl, HG_HEADS * HG_DK), jnp.asarray(level))


LOG2E = 1.4426950408889634
FOX_FOLD = 8
FOX_SUM_ROWS = 16


def _head_column(c_blk, h):
    lane = lax.broadcasted_iota(jnp.int32, c_blk.shape, 1)
    return jnp.sum(jnp.where(lane == h, c_blk, 0.0), axis=1, keepdims=True)


def _decay_features(c_col):
    rows = c_col.shape[0]
    lane = lax.broadcasted_iota(jnp.int32, (rows, LANES), 1)
    hi, mid, lo = (part.astype(F32) for part in _split3(jnp.broadcast_to(c_col, (rows, LANES))))

    def place(base, sign):
        ones = (lane >= 3 - base) & (lane < 6 - base)
        feat = jnp.where(lane == base, sign * hi,
                         jnp.where(lane == base + 1, sign * mid,
                                   jnp.where(lane == base + 2, sign * lo,
                                             jnp.where(ones, 1.0, 0.0))))
        return feat.astype(BF16)

    return place(3, 1.0), place(0, -1.0)


def _fox_kernel(q_ref, k_ref, v_ref, c_ref, o_ref, kx_ref, qf_ref, vt_ref, qxt_ref,
                s0_ref, s1_ref, x0_ref, x1_ref, m_ref, acc_ref, *, tk):
    h = pl.program_id(1)
    qi = pl.program_id(2)
    tq = q_ref.shape[0]
    nk = kx_ref.shape[0]

    @pl.when(qi == 0)
    def _():
        def setup(kb, carry):
            rows = pl.ds(pl.multiple_of(kb * tk, tk), tk)
            q_feat, k_feat = _decay_features(_head_column(c_ref[rows, :], h) * LOG2E)
            kx_ref[kb, :, 0:FOX_DH] = k_ref[rows, :]
            kx_ref[kb, :, FOX_DH:2 * FOX_DH] = k_feat
            qf_ref[kb] = q_feat
            vt_ref[kb, 0:FOX_DH, :] = v_ref[rows, :].T
            ones_row = lax.broadcasted_iota(jnp.int32, (FOX_SUM_ROWS, tk), 0) == 0
            vt_ref[kb, FOX_DH:FOX_DH + FOX_SUM_ROWS, :] = ones_row.astype(F32).astype(BF16)
            return carry

        lax.fori_loop(0, nk, setup, 0)

    qxt_ref[...] = jnp.concatenate([q_ref[...], qf_ref[qi]], axis=1).T
    m_ref[...] = jnp.full_like(m_ref, -jnp.inf)
    acc_ref[...] = jnp.zeros_like(acc_ref)

    def fold(x, op):
        part = op(x.reshape(FOX_FOLD, tk // FOX_FOLD, tq), axis=0)
        return op(part, axis=0, keepdims=True)

    def scores(kb, s_ref, smax_ref):
        st = _dot(kx_ref[kb], qxt_ref[...])
        s_ref[...] = st
        smax_ref[...] = fold(st, jnp.max)

    def update(kb, s_ref, smax_ref, masked):
        st = s_ref[...]
        if masked:
            key = lax.broadcasted_iota(jnp.int32, st.shape, 0)
            qry = lax.broadcasted_iota(jnp.int32, st.shape, 1)
            st = jnp.where(key <= qry, st, -jnp.inf)
            blk_max = fold(st, jnp.max)
        else:
            blk_max = smax_ref[...]
        m_prev = m_ref[...]
        m_new = jnp.maximum(m_prev, blk_max)
        alpha = jnp.exp2(m_prev - m_new)
        p = jnp.exp2(st - m_new)
        m_ref[...] = m_new
        acc_ref[...] = alpha * acc_ref[...] + _dot(vt_ref[kb], p.astype(BF16))

    scores(0, s0_ref, x0_ref)

    def pair(kb):
        scores(kb + 1, s1_ref, x1_ref)
        update(kb, s0_ref, x0_ref, False)
        scores(kb + 2, s0_ref, x0_ref)
        update(kb + 1, s1_ref, x1_ref, False)

    def body4(j, carry):
        pair(4 * j)
        pair(4 * j + 2)
        return carry

    def body2(j, carry):
        pair(2 * j)
        return carry

    n_pairs = qi // 2
    lax.fori_loop(0, n_pairs // 2, body4, 0)
    lax.fori_loop(2 * (n_pairs // 2), n_pairs, body2, 0)

    @pl.when(qi % 2 == 1)
    def _():
        scores(qi, s1_ref, x1_ref)
        update(qi - 1, s0_ref, x0_ref, False)
        update(qi, s1_ref, x1_ref, True)

    @pl.when(qi % 2 == 0)
    def _():
        update(qi, s0_ref, x0_ref, True)

    out = acc_ref[0:FOX_DH, :] / acc_ref[FOX_DH:FOX_DH + 1, :]
    o_ref[...] = out.T.astype(o_ref.dtype)


def _fox(proj_b, c, batch, seq, tq=512):
    m = proj_b.shape[0]
    nq = seq // tq
    return pl.pallas_call(
        functools.partial(_fox_kernel, tk=tq),
        grid=(batch, FOX_HEADS, nq),
        in_specs=[
            pl.BlockSpec((tq, FOX_DH), lambda b, h, i: (b * nq + i, h)),
            pl.BlockSpec((seq, FOX_DH), lambda b, h, i: (b, FOX_HEADS + h)),
            pl.BlockSpec((seq, FOX_DH), lambda b, h, i: (b, 2 * FOX_HEADS + h)),
            pl.BlockSpec((seq, LANES), lambda b, h, i: (b, 0)),
        ],
        out_specs=pl.BlockSpec((tq, FOX_DH), lambda b, h, i: (b * nq + i, h)),
        out_shape=jax.ShapeDtypeStruct((m, FOX_HEADS * FOX_DH), BF16),
        scratch_shapes=[
            pltpu.VMEM((nq, tq, 2 * FOX_DH), BF16),
            pltpu.VMEM((nq, tq, FOX_DH), BF16),
            pltpu.VMEM((nq, FOX_DH + FOX_SUM_ROWS, tq), BF16),
            pltpu.VMEM((2 * FOX_DH, tq), BF16),
            pltpu.VMEM((tq, tq), F32),
            pltpu.VMEM((tq, tq), F32),
            pltpu.VMEM((1, tq), F32),
            pltpu.VMEM((1, tq), F32),
            pltpu.VMEM((1, tq), F32),
            pltpu.VMEM((FOX_DH + FOX_SUM_ROWS, tq), F32),
        ],
        compiler_params=_params(("parallel", "parallel", "arbitrary")),
        name="fox_attention",
    )(proj_b, proj_b, proj_b, c)


def _merge_kernel(x_ref, oa_ref, ga_ref, ob_ref, g0_ref, g1_ref, hg_ref,
                  wa_ref, wb_ref, wo_ref, post_ref, o_ref):
    oa = _rms(oa_ref[...].astype(F32), hg_ref[...]) * ga_ref[...].astype(F32)
    ya = _dot(oa.astype(BF16), wa_ref[...])
    yb = _dot(ob_ref[...], wb_ref[...])
    y = g0_ref[...].astype(F32) * ya + g1_ref[...].astype(F32) * yb
    z = _dot(y.astype(BF16), wo_ref[...])
    o_ref[...] = x_ref[...] + _rms(z, post_ref[...])


def _merge(x, o_a, g_act, o_b, gates, hg_norm_g, w_a, w_b, w_o, post_g, tm=512):
    m, d = x.shape
    row = lambda c: pl.BlockSpec((tm, d), lambda i: (i, c))
    return pl.pallas_call(
        _merge_kernel,
        grid=(m // tm,),
        in_specs=[row(0), row(0), row(0), row(0), row(0), row(1), _resident((1, d)),
                  _resident((d, d)), _resident((d, d)), _resident((d, d)), _resident((1, d))],
        out_specs=row(0),
        out_shape=jax.ShapeDtypeStruct((m, d), F32),
        compiler_params=_params(("parallel",)),
        name="branch_merge",
    )(x, o_a, g_act, o_b, gates, gates, hg_norm_g.reshape(1, d), w_a, w_b, w_o,
      post_g.reshape(1, d))


def _memattn_kernel(x_ref, pre_ref, kv_ref, wq_ref, wo_ref, post_ref, o_ref):
    x = x_ref[...]
    h = _rms(x, pre_ref[...]).astype(BF16)
    q = _dot(h, wq_ref[...]).astype(BF16)
    heads = []
    for hd in range(MEM_HEADS):
        lo = hd * MEM_DH
        kh = kv_ref[:, lo:lo + MEM_DH]
        vh = kv_ref[:, D_MODEL + lo:D_MODEL + lo + MEM_DH]
        s = lax.dot_general(q[:, lo:lo + MEM_DH], kh, NT_DIMS, preferred_element_type=F32)
        p = jnp.exp(s - jnp.max(s, axis=1, keepdims=True))
        p = p / jnp.sum(p, axis=1, keepdims=True)
        heads.append(_dot(p.astype(BF16), vh).astype(BF16))
    o = jnp.concatenate(heads, axis=1)
    o_ref[...] = x + _rms(_dot(o, wo_ref[...]), post_ref[...])


def _memattn(x, pre_g, kv, w_q, w_o, post_g, batch, tm=512):
    m, d = x.shape
    ns = m // batch // tm
    vec = pl.BlockSpec((1, d), lambda b, s: (0, 0))
    mat = pl.BlockSpec((d, d), lambda b, s: (0, 0))
    return pl.pallas_call(
        _memattn_kernel,
        grid=(batch, ns),
        in_specs=[
            pl.BlockSpec((tm, d), lambda b, s: (b * ns + s, 0)),
            vec,
            pl.BlockSpec((MEM_LEN, 2 * d), lambda b, s: (b, 0)),
            mat, mat, vec,
        ],
        out_specs=pl.BlockSpec((tm, d), lambda b, s: (b * ns + s, 0)),
        out_shape=jax.ShapeDtypeStruct((m, d), F32),
        compiler_params=_params(("parallel", "parallel")),
        name="mem_cross_attention",
    )(x, pre_g.reshape(1, d), kv, w_q, w_o, post_g.reshape(1, d))


def kernel(x, mem, ffn1_pre_g, ffn1_w_in, ffn1_w_down, ffn1_post_g, mix_pre_g, w_in, hg_lb_logits, hg_norm_g, fox_f_bias, w_branch_a, w_branch_b, b_gate, w_out, mix_post_g, mem_pre_g, mem_kv_g, w_mq, w_mkv, w_mo, mem_post_g, ffn2_pre_g, ffn2_w_in, ffn2_w_down, ffn2_post_g):
    batch, seq, d = x.shape
    depth = ffn1_w_in.shape[0]
    xf = x.reshape(batch * seq, d)
    memf = mem.reshape(batch * MEM_LEN, d)
    kw = HG_HEADS * HG_DK
    off_b = 4 * kw
    off_f = off_b + 3 * FOX_HEADS * FOX_DH
    off_g = off_f + FOX_HEADS
    for l in range(depth):
        xf = _ffn(xf, ffn1_pre_g[l], ffn1_w_in[l].astype(BF16), ffn1_w_down[l].astype(BF16),
                  ffn1_post_g[l])

        w = w_in[l]
        fox_w = FOX_HEADS * FOX_DH
        w_cat = jnp.concatenate([
            w[:, :off_b],
            w[:, off_b:off_b + fox_w] * (LOG2E / math.sqrt(FOX_DH)),
            w[:, off_b + fox_w:off_f],
            jnp.pad(w[:, off_f:off_g], ((0, 0), (0, LANES - FOX_HEADS))),
            w[:, off_g:],
        ], axis=1).astype(BF16)
        q_act, f_logit, inp, g_act, proj_b, fb, gates = _inproj(
            xf, mix_pre_g[l], w_cat, b_gate[l].reshape(1, 2 * d),
            (kw, kw, kw, kw, 3 * fox_w, LANES, 2 * d),
            (BF16, F32, BF16, BF16, BF16, F32, BF16),
            ("silu", None, None, "silu", None, None, "sigmoid_bias"))

        bias = jnp.pad(fox_f_bias[l], (0, LANES - FOX_HEADS)).reshape(1, LANES)
        c = _fcum(fb, bias, batch)

        o_a = _hgrn2(q_act, f_logit, inp, hg_lb_logits, l, batch)
        o_b = _fox(proj_b, c, batch, seq)

        xf = _merge(xf, o_a, g_act, o_b, gates, hg_norm_g[l],
                    w_branch_a[l].astype(BF16), w_branch_b[l].astype(BF16),
                    w_out[l].astype(BF16), mix_post_g[l])

        kv = _norm_matmul(memf, mem_kv_g[l], w_mkv[l].astype(BF16), BF16, MEM_LEN, 1024)
        xf = _memattn(xf, mem_pre_g[l], kv, (w_mq[l] * (1.0 / math.sqrt(MEM_DH))).astype(BF16),
                      w_mo[l].astype(BF16), mem_post_g[l], batch)

        xf = _ffn(xf, ffn2_pre_g[l], ffn2_w_in[l].astype(BF16), ffn2_w_down[l].astype(BF16),
                  ffn2_post_g[l])
    return xf.reshape(batch, seq, d)
```

```python
import functools
import math

import jax
import jax.numpy as jnp
import numpy as np
from jax import lax
from jax.experimental import pallas as pl
from jax.experimental.pallas import tpu as pltpu

F32 = jnp.float32
BF16 = jnp.bfloat16

D_MODEL = 1024
HG_HEADS = 8
HG_DK = 128
FOX_HEADS = 8
FOX_DH = 128
MEM_LEN = 256
MEM_HEADS = 4
MEM_DH = D_MODEL // MEM_HEADS
D_FF = 2816
EPS = 1e-6
LANES = 128

VMEM_LIMIT = 56 * 1024 * 1024

NT_DIMS = (((1,), (1,)), ((), ()))
TN_DIMS = (((0,), (0,)), ((), ()))


def _params(sem):
    return pltpu.CompilerParams(dimension_semantics=sem, vmem_limit_bytes=VMEM_LIMIT)


def _rms(x, g):
    ms = jnp.mean(x * x, axis=-1, keepdims=True)
    return x * lax.rsqrt(ms + EPS) * g


def _sigmoid(x):
    return 0.5 * jnp.tanh(0.5 * x) + 0.5


def _dot(a, b):
    return jnp.dot(a, b, preferred_element_type=F32)


def _norm_matmul_kernel(x_ref, g_ref, w_ref, o_ref, h_ref):
    @pl.when(pl.program_id(1) == 0)
    def _():
        h_ref[...] = _rms(x_ref[...], g_ref[...]).astype(BF16)

    o_ref[...] = _dot(h_ref[...], w_ref[...]).astype(o_ref.dtype)


def _norm_matmul(x, g, w, out_dtype, tm, tn):
    m, d = x.shape
    n = w.shape[1]
    return pl.pallas_call(
        _norm_matmul_kernel,
        grid=(m // tm, n // tn),
        in_specs=[
            pl.BlockSpec((tm, d), lambda i, j: (i, 0)),
            pl.BlockSpec((1, d), lambda i, j: (0, 0)),
            pl.BlockSpec((d, tn), lambda i, j: (0, j)),
        ],
        out_specs=pl.BlockSpec((tm, tn), lambda i, j: (i, j)),
        out_shape=jax.ShapeDtypeStruct((m, n), out_dtype),
        scratch_shapes=[pltpu.VMEM((tm, d), BF16)],
        compiler_params=_params(("parallel", "arbitrary")),
        name="norm_matmul",
    )(x, g.reshape(1, d), w)


def _resident(shape):
    return pl.BlockSpec(shape, lambda *_: (0,) * len(shape), pipeline_mode=pl.Buffered(1))


def _inproj_kernel(x_ref, g_ref, w_ref, bias_ref, *out_refs, tn, acts):
    h = _rms(x_ref[...], g_ref[...]).astype(BF16)
    off = 0
    for o_ref, act in zip(out_refs, acts):
        n = o_ref.shape[1]
        for j in range(0, n, tn):
            width = min(tn, n - j)
            y = _dot(h, w_ref[:, off + j:off + j + width])
            if act == "silu":
                y = y * _sigmoid(y)
            elif act == "sigmoid_bias":
                y = _sigmoid(y + bias_ref[:, j:j + width])
            o_ref[:, j:j + width] = y.astype(o_ref.dtype)
        off += n


def _inproj(x, g, w, bias, widths, dtypes, acts, tm=512, tn=512):
    m, d = x.shape
    return pl.pallas_call(
        functools.partial(_inproj_kernel, tn=tn, acts=acts),
        grid=(m // tm,),
        in_specs=[pl.BlockSpec((tm, d), lambda i: (i, 0)), _resident((1, d)), _resident(w.shape),
                  _resident(bias.shape)],
        out_specs=[pl.BlockSpec((tm, n), lambda i: (i, 0)) for n in widths],
        out_shape=[jax.ShapeDtypeStruct((m, n), dt) for n, dt in zip(widths, dtypes)],
        compiler_params=_params(("parallel",)),
        name="input_projection",
    )(x, g.reshape(1, d), w, bias)


FFN_CHUNK = 256


def _ffn_body(x, pre_ref, w_in_ref, w_down_ref, post_ref, acc_ref):
    h = _rms(x, pre_ref[...]).astype(BF16)
    for f in range(D_FF // FFN_CHUNK):
        cols = slice(f * FFN_CHUNK, (f + 1) * FFN_CHUNK)
        gate = _dot(h, w_in_ref[:, cols])
        up = _dot(h, w_in_ref[:, D_FF + f * FFN_CHUNK:D_FF + (f + 1) * FFN_CHUNK])
        act = (gate * _sigmoid(gate) * up).astype(BF16)
        part = _dot(act, w_down_ref[cols, :])
        if f == 0:
            acc_ref[...] = part
        else:
            acc_ref[...] += part
    return x + 0.5 * _rms(acc_ref[...], post_ref[...])


def _ffn_kernel(x_ref, pre_ref, w_in_ref, w_down_ref, post_ref, o_ref, acc_ref):
    o_ref[...] = _ffn_body(x_ref[...], pre_ref, w_in_ref, w_down_ref, post_ref, acc_ref)


def _ffn(x, pre_g, w_in, w_down, post_g, tm=512):
    m, d = x.shape
    return pl.pallas_call(
        _ffn_kernel,
        grid=(m // tm,),
        in_specs=[
            pl.BlockSpec((tm, d), lambda i: (i, 0)),
            _resident((1, d)),
            _resident(w_in.shape),
            _resident(w_down.shape),
            _resident((1, d)),
        ],
        out_specs=pl.BlockSpec((tm, d), lambda i: (i, 0)),
        out_shape=jax.ShapeDtypeStruct((m, d), F32),
        scratch_shapes=[pltpu.VMEM((tm, d), F32)],
        compiler_params=_params(("parallel",)),
        name="ffn",
    )(x, pre_g.reshape(1, d), w_in, w_down, post_g.reshape(1, d))


def _split3(x):
    hi = x.astype(BF16)
    r = x - hi.astype(F32)
    mid = r.astype(BF16)
    lo = (r - mid.astype(F32)).astype(BF16)
    return hi, mid, lo


def _fcum_kernel(fb_ref, bias_ref, c_ref, carry_ref):
    @pl.when(pl.program_id(1) == 0)
    def _():
        carry_ref[...] = jnp.zeros_like(carry_ref)

    z = fb_ref[...] + bias_ref[...]
    ls = jnp.minimum(z, 0.0) - jnp.log(1.0 + jnp.exp(-jnp.abs(z)))
    tb = z.shape[0]
    row = lax.broadcasted_iota(jnp.int32, (tb, tb), 0)
    col = lax.broadcasted_iota(jnp.int32, (tb, tb), 1)
    tril = (col <= row).astype(BF16)
    hi, mid, lo = _split3(ls)
    c = _dot(tril, hi) + _dot(tril, mid) + _dot(tril, lo) + carry_ref[...]
    c_ref[...] = c
    carry_ref[...] = c[tb - 1:tb, :]


def _fcum(fb, bias, batch, tb=256):
    m = fb.shape[0]
    ns = m // batch // tb
    return pl.pallas_call(
        _fcum_kernel,
        grid=(batch, ns),
        in_specs=[
            pl.BlockSpec((tb, LANES), lambda b, s: (b * ns + s, 0)),
            pl.BlockSpec((1, LANES), lambda b, s: (0, 0)),
        ],
        out_specs=pl.BlockSpec((tb, LANES), lambda b, s: (b * ns + s, 0)),
        out_shape=jax.ShapeDtypeStruct((m, LANES), F32),
        scratch_shapes=[pltpu.VMEM((1, LANES), F32)],
        compiler_params=_params(("parallel", "arbitrary")),
        name="fox_decay_cumsum",
    )(fb, bias)


def _hgrn_level_table(c):
    t = np.arange(c)[:, None]
    s = np.arange(c)[None, :]
    level = np.full((c, c), -2, np.int32)
    level[t == s] = -1
    for l in range(int(math.log2(c))):
        h = c >> (l + 1)
        mid = (t // (2 * h)) * (2 * h) + h
        sel = (t // (2 * h) == s // (2 * h)) & (t >= mid) & (s < mid)
        level[sel & (level == -2)] = l
    return level


SUBLANES = 8


def _prefix_sum_rows(x, row):
    k = 1
    while k < x.shape[0]:
        x = x + jnp.where(row >= k, pltpu.roll(x, k, axis=0), 0.0)
        k *= 2
    return x


def _hgrn_sign_table(c):
    t = np.arange(c)[:, None]
    halves = [c >> (l + 1) for l in range(int(math.log2(c)))]
    sign = [np.where((t % (2 * h)) >= h, 1.0, -1.0) * np.ones((1, HG_DK)) for h in halves]
    return np.stack(sign).astype(np.float32)


def _midpoint_rows(b, h):
    c, w = b.shape
    if h >= SUBLANES:
        pieces = [jnp.broadcast_to(b[j * 2 * h + h - 1:j * 2 * h + h, :], (2 * h, w))
                  for j in range(c // (2 * h))]
        return pieces[0] if len(pieces) == 1 else jnp.concatenate(pieces, axis=0)
    groups = c // SUBLANES
    b3 = b.reshape(groups, SUBLANES, w)
    sub = lax.broadcasted_iota(jnp.int32, (1, SUBLANES, w), 1)
    r3 = None
    for j in range(SUBLANES // (2 * h)):
        src = j * 2 * h + h - 1
        piece = jnp.broadcast_to(b3[:, src:src + 1, :], (groups, SUBLANES, w))
        r3 = piece if r3 is None else jnp.where(sub >= j * 2 * h, piece, r3)
    return r3.reshape(c, w)


def _hgrn_kernel(q_ref, f_ref, i_ref, lb_ref, lvl_ref, sign_ref, o_ref, st_ref, *, chunk, layer):
    @pl.when(pl.program_id(2) == 0)
    def _():
        st_ref[...] = jnp.zeros_like(st_ref)

    c = chunk
    n_lvl = int(math.log2(c))
    ts = q_ref.shape[0]
    logits = lb_ref[...]
    e = jnp.exp(logits - jnp.max(logits, axis=0, keepdims=True))
    lb = jnp.sum(e[0:layer + 1, :], axis=0, keepdims=True) / jnp.sum(e, axis=0, keepdims=True)
    lvl = lvl_ref[...]
    st = st_ref[...]

    row = lax.broadcasted_iota(jnp.int32, (c, HG_DK), 0)
    ks, bs = [], []
    for ci in range(ts // c):
        f = lb + (1.0 - lb) * _sigmoid(f_ref[pl.ds(ci * c, c), :])
        bs.append(_prefix_sum_rows(jnp.log2(f), row))
        ks.append(1.0 - f)

    chunks = range(ts // c)
    q16s = [q_ref[pl.ds(ci * c, c), :] for ci in chunks]
    vs = [i_ref[pl.ds(ci * c, c), :] for ci in chunks]
    k16s = [k.astype(BF16) for k in ks]

    a = [jnp.where(lvl == -1, lax.dot_general(q16s[ci], k16s[ci], NT_DIMS,
                                              preferred_element_type=F32), 0.0) for ci in chunks]
    for l in range(n_lvl):
        h = c >> (l + 1)
        for ci in chunks:
            d_l = (bs[ci] - _midpoint_rows(bs[ci], h)) * sign_ref[l]
            e_l = jnp.exp2(d_l).astype(BF16)
            p = lax.dot_general(q16s[ci] * e_l, k16s[ci] * e_l, NT_DIMS,
                                preferred_element_type=F32)
            a[ci] = jnp.where(lvl == l, p, a[ci])

    o_intra = [_dot(a[ci].astype(BF16), vs[ci]) for ci in chunks]
    upds = [lax.dot_general(vs[ci], (ks[ci] * jnp.exp2(bs[ci][c - 1:c, :] - bs[ci])).astype(BF16),
                            TN_DIMS, preferred_element_type=F32) for ci in chunks]

    for ci in chunks:
        o = o_intra[ci] + lax.dot_general(q16s[ci] * jnp.exp2(bs[ci]).astype(BF16),
                                          st.astype(BF16), NT_DIMS, preferred_element_type=F32)
        o_ref[pl.ds(ci * c, c), :] = o.astype(o_ref.dtype)
        st = st * jnp.exp2(bs[ci][c - 1:c, :]) + upds[ci]

    st_ref[...] = st


def _hgrn2(q_act, f_logit, inp, lb_logits, layer, batch, ts=512, chunk=128):
    m = q_act.shape[0]
    ns = m // batch // ts
    level = _hgrn_level_table(chunk)
    sign = _hgrn_sign_table(chunk)
    nl = lb_logits.shape[0]
    blk = pl.BlockSpec((ts, HG_DK), lambda b, h, s: (b * ns + s, h))
    return pl.pallas_call(
        functools.partial(_hgrn_kernel, chunk=chunk, layer=layer),
        grid=(batch, HG_HEADS, ns),
        in_specs=[
            blk, blk, blk,
            pl.BlockSpec((nl, HG_DK), lambda b, h, s: (0, h)),
            _resident(level.shape), _resident(sign.shape),
        ],
        out_specs=blk,
        out_shape=jax.ShapeDtypeStruct((m, HG_HEADS * HG_DK), BF16),
        scratch_shapes=[pltpu.VMEM((HG_DK, HG_DK), F32)],
        compiler_params=_params(("parallel", "parallel", "arbitrary")),
        name="hgrn2",
    )(q_act, f_logit, inp, lb_logits.reshape(nl, HG_HEADS * HG_DK), jnp.asarray(level),
      jnp.asarray(sign))


LOG2E = 1.4426950408889634
FOX_FOLD = 8
FOX_SUM_ROWS = 16


def _head_column(c_blk, h):
    lane = lax.broadcasted_iota(jnp.int32, c_blk.shape, 1)
    return jnp.sum(jnp.where(lane == h, c_blk, 0.0), axis=1, keepdims=True)


def _decay_features(c_col):
    rows = c_col.shape[0]
    lane = lax.broadcasted_iota(jnp.int32, (rows, LANES), 1)
    hi, mid, lo = (part.astype(F32) for part in _split3(jnp.broadcast_to(c_col, (rows, LANES))))

    def place(base, sign):
        ones = (lane >= 3 - base) & (lane < 6 - base)
        feat = jnp.where(lane == base, sign * hi,
                         jnp.where(lane == base + 1, sign * mid,
                                   jnp.where(lane == base + 2, sign * lo,
                                             jnp.where(ones, 1.0, 0.0))))
        return feat.astype(BF16)

    return place(3, 1.0), place(0, -1.0)


def _fox_kernel(q_ref, k_ref, v_ref, c_ref, o_ref, kx_ref, qf_ref, vt_ref, qxt_ref,
                s0_ref, s1_ref, x0_ref, x1_ref, m_ref, acc_ref, *, tk):
    h = pl.program_id(1)
    qi = pl.program_id(2)
    tq = q_ref.shape[0]
    nk = kx_ref.shape[0]

    @pl.when(qi == 0)
    def _():
        def setup(kb, carry):
            rows = pl.ds(pl.multiple_of(kb * tk, tk), tk)
            q_feat, k_feat = _decay_features(_head_column(c_ref[rows, :], h) * LOG2E)
            kx_ref[kb, :, 0:FOX_DH] = k_ref[rows, :]
            kx_ref[kb, :, FOX_DH:2 * FOX_DH] = k_feat
            qf_ref[kb] = q_feat
            vt_ref[kb, 0:FOX_DH, :] = v_ref[rows, :].T
            ones_row = lax.broadcasted_iota(jnp.int32, (FOX_SUM_ROWS, tk), 0) == 0
            vt_ref[kb, FOX_DH:FOX_DH + FOX_SUM_ROWS, :] = ones_row.astype(F32).astype(BF16)
            return carry

        lax.fori_loop(0, nk, setup, 0)

    qxt_ref[...] = jnp.concatenate([q_ref[...], qf_ref[qi]], axis=1).T
    m_ref[...] = jnp.full_like(m_ref, -jnp.inf)
    acc_ref[...] = jnp.zeros_like(acc_ref)

    def fold(x, op):
        part = op(x.reshape(FOX_FOLD, tk // FOX_FOLD, tq), axis=0)
        return op(part, axis=0, keepdims=True)

    def scores(kb, s_ref, smax_ref):
        st = _dot(kx_ref[kb], qxt_ref[...])
        s_ref[...] = st
        smax_ref[...] = fold(st, jnp.max)

    def update(kb, s_ref, smax_ref, masked):
        st = s_ref[...]
        if masked:
            key = lax.broadcasted_iota(jnp.int32, st.shape, 0)
            qry = lax.broadcasted_iota(jnp.int32, st.shape, 1)
            st = jnp.where(key <= qry, st, -jnp.inf)
            blk_max = fold(st, jnp.max)
        else:
            blk_max = smax_ref[...]
        m_prev = m_ref[...]
        m_new = jnp.maximum(m_prev, blk_max)
        alpha = jnp.exp2(m_prev - m_new)
        p = jnp.exp2(st - m_new)
        m_ref[...] = m_new
        acc_ref[...] = alpha * acc_ref[...] + _dot(vt_ref[kb], p.astype(BF16))

    scores(0, s0_ref, x0_ref)

    def pair(kb):
        scores(kb + 1, s1_ref, x1_ref)
        update(kb, s0_ref, x0_ref, False)
        scores(kb + 2, s0_ref, x0_ref)
        update(kb + 1, s1_ref, x1_ref, False)

    def body4(j, carry):
        pair(4 * j)
        pair(4 * j + 2)
        return carry

    def body2(j, carry):
        pair(2 * j)
        return carry

    n_pairs = qi // 2
    lax.fori_loop(0, n_pairs // 2, body4, 0)
    lax.fori_loop(2 * (n_pairs // 2), n_pairs, body2, 0)

    @pl.when(qi % 2 == 1)
    def _():
        scores(qi, s1_ref, x1_ref)
        update(qi - 1, s0_ref, x0_ref, False)
        update(qi, s1_ref, x1_ref, True)

    @pl.when(qi % 2 == 0)
    def _():
        update(qi, s0_ref, x0_ref, True)

    out = acc_ref[0:FOX_DH, :] / acc_ref[FOX_DH:FOX_DH + 1, :]
    o_ref[...] = out.T.astype(o_ref.dtype)


def _fox(proj_b, c, batch, seq, tq=512):
    m = proj_b.shape[0]
    nq = seq // tq
    return pl.pallas_call(
        functools.partial(_fox_kernel, tk=tq),
        grid=(batch, FOX_HEADS, nq),
        in_specs=[
            pl.BlockSpec((tq, FOX_DH), lambda b, h, i: (b * nq + i, h)),
            pl.BlockSpec((seq, FOX_DH), lambda b, h, i: (b, FOX_HEADS + h)),
            pl.BlockSpec((seq, FOX_DH), lambda b, h, i: (b, 2 * FOX_HEADS + h)),
            pl.BlockSpec((seq, LANES), lambda b, h, i: (b, 0)),
        ],
        out_specs=pl.BlockSpec((tq, FOX_DH), lambda b, h, i: (b * nq + i, h)),
        out_shape=jax.ShapeDtypeStruct((m, FOX_HEADS * FOX_DH), BF16),
        scratch_shapes=[
            pltpu.VMEM((nq, tq, 2 * FOX_DH), BF16),
            pltpu.VMEM((nq, tq, FOX_DH), BF16),
            pltpu.VMEM((nq, FOX_DH + FOX_SUM_ROWS, tq), BF16),
            pltpu.VMEM((2 * FOX_DH, tq), BF16),
            pltpu.VMEM((tq, tq), F32),
            pltpu.VMEM((tq, tq), F32),
            pltpu.VMEM((1, tq), F32),
            pltpu.VMEM((1, tq), F32),
            pltpu.VMEM((1, tq), F32),
            pltpu.VMEM((FOX_DH + FOX_SUM_ROWS, tq), F32),
        ],
        compiler_params=_params(("parallel", "parallel", "arbitrary")),
        name="fox_attention",
    )(proj_b, proj_b, proj_b, c)


def _merge_body(x, oa_ref, ga_ref, ob_ref, g0_ref, g1_ref, hg_ref, wa_ref, wb_ref, wo_ref, post_ref):
    oa = _rms(oa_ref[...].astype(F32), hg_ref[...]) * ga_ref[...].astype(F32)
    ya = _dot(oa.astype(BF16), wa_ref[...])
    yb = _dot(ob_ref[...], wb_ref[...])
    y = g0_ref[...].astype(F32) * ya + g1_ref[...].astype(F32) * yb
    z = _dot(y.astype(BF16), wo_ref[...])
    return x + _rms(z, post_ref[...])


def _memattn_body(x, pre_ref, kv_ref, wq_ref, wo_ref, post_ref):
    h = _rms(x, pre_ref[...]).astype(BF16)
    q = _dot(h, wq_ref[...]).astype(BF16)
    heads = []
    for hd in range(MEM_HEADS):
        lo = hd * MEM_DH
        kh = kv_ref[:, lo:lo + MEM_DH]
        vh = kv_ref[:, D_MODEL + lo:D_MODEL + lo + MEM_DH]
        s = lax.dot_general(q[:, lo:lo + MEM_DH], kh, NT_DIMS, preferred_element_type=F32)
        p = jnp.exp(s - jnp.max(s, axis=1, keepdims=True))
        inv = 1.0 / jnp.sum(p, axis=1, keepdims=True)
        heads.append((_dot(p.astype(BF16), vh) * inv).astype(BF16))
    o = jnp.concatenate(heads, axis=1)
    return x + _rms(_dot(o, wo_ref[...]), post_ref[...])


def _tail_kernel(x_ref, oa_ref, ga_ref, ob_ref, g0_ref, g1_ref, kv_ref, hg_ref, wa_ref, wb_ref,
                 wo_ref, mix_post_ref, mem_pre_ref, wq_ref, wmo_ref, mem_post_ref,
                 ffn_pre_ref, w_in_ref, w_down_ref, ffn_post_ref, o_ref, acc_ref):
    x = _merge_body(x_ref[...], oa_ref, ga_ref, ob_ref, g0_ref, g1_ref, hg_ref, wa_ref, wb_ref,
                    wo_ref, mix_post_ref)
    x = _memattn_body(x, mem_pre_ref, kv_ref, wq_ref, wmo_ref, mem_post_ref)
    o_ref[...] = _ffn_body(x, ffn_pre_ref, w_in_ref, w_down_ref, ffn_post_ref, acc_ref)


def _tail(x, o_a, g_act, o_b, gates, kv, hg_norm_g, w_a, w_b, w_o, mix_post_g, mem_pre_g, w_q,
          w_mo, mem_post_g, ffn_pre_g, w_in, w_down, ffn_post_g, batch, tm=512):
    m, d = x.shape
    ns = m // batch // tm
    row = lambda c: pl.BlockSpec((tm, d), lambda b, s: (b * ns + s, c))
    vec, mat = _resident((1, d)), _resident((d, d))
    r1 = lambda g: g.reshape(1, d)
    return pl.pallas_call(
        _tail_kernel,
        grid=(batch, ns),
        in_specs=[row(0), row(0), row(0), row(0), row(0), row(1),
                  pl.BlockSpec((MEM_LEN, 2 * d), lambda b, s: (b, 0)),
                  vec, mat, mat, mat, vec, vec, mat, mat, vec,
                  vec, _resident(w_in.shape), _resident(w_down.shape), vec],
        out_specs=row(0),
        out_shape=jax.ShapeDtypeStruct((m, d), F32),
        scratch_shapes=[pltpu.VMEM((tm, d), F32)],
        compiler_params=_params(("parallel", "parallel")),
        name="merge_memattn_ffn",
    )(x, o_a, g_act, o_b, gates, gates, kv, r1(hg_norm_g), w_a, w_b, w_o, r1(mix_post_g),
      r1(mem_pre_g), w_q, w_mo, r1(mem_post_g), r1(ffn_pre_g), w_in, w_down, r1(ffn_post_g))


def kernel(x, mem, ffn1_pre_g, ffn1_w_in, ffn1_w_down, ffn1_post_g, mix_pre_g, w_in, hg_lb_logits, hg_norm_g, fox_f_bias, w_branch_a, w_branch_b, b_gate, w_out, mix_post_g, mem_pre_g, mem_kv_g, w_mq, w_mkv, w_mo, mem_post_g, ffn2_pre_g, ffn2_w_in, ffn2_w_down, ffn2_post_g):
    batch, seq, d = x.shape
    depth = ffn1_w_in.shape[0]
    xf = x.reshape(batch * seq, d)
    memf = mem.reshape(batch * MEM_LEN, d)
    kw = HG_HEADS * HG_DK
    off_b = 4 * kw
    off_f = off_b + 3 * FOX_HEADS * FOX_DH
    off_g = off_f + FOX_HEADS
    for l in range(depth):
        xf = _ffn(xf, ffn1_pre_g[l], ffn1_w_in[l].astype(BF16), ffn1_w_down[l].astype(BF16),
                  ffn1_post_g[l])

        w = w_in[l]
        fox_w = FOX_HEADS * FOX_DH
        w_cat = jnp.concatenate([
            w[:, :off_b],
            w[:, off_b:off_b + fox_w] * (LOG2E / math.sqrt(FOX_DH)),
            w[:, off_b + fox_w:off_f],
            jnp.pad(w[:, off_f:off_g], ((0, 0), (0, LANES - FOX_HEADS))),
            w[:, off_g:],
        ], axis=1).astype(BF16)
        q_act, f_logit, inp, g_act, proj_b, fb, gates = _inproj(
            xf, mix_pre_g[l], w_cat, b_gate[l].reshape(1, 2 * d),
            (kw, kw, kw, kw, 3 * fox_w, LANES, 2 * d),
            (BF16, F32, BF16, BF16, BF16, F32, BF16),
            ("silu", None, None, "silu", None, None, "sigmoid_bias"))

        bias = jnp.pad(fox_f_bias[l], (0, LANES - FOX_HEADS)).reshape(1, LANES)
        c = _fcum(fb, bias, batch)

        o_a = _hgrn2(q_act, f_logit, inp, hg_lb_logits, l, batch)
        o_b = _fox(proj_b, c, batch, seq)

        kv = _norm_matmul(memf, mem_kv_g[l], w_mkv[l].astype(BF16), BF16, MEM_LEN, 1024)
        xf = _tail(xf, o_a, g_act, o_b, gates, kv, hg_norm_g[l],
                   w_branch_a[l].astype(BF16), w_branch_b[l].astype(BF16), w_out[l].astype(BF16),
                   mix_post_g[l], mem_pre_g[l],
                   (w_mq[l] * (1.0 / math.sqrt(MEM_DH))).astype(BF16), w_mo[l].astype(BF16),
                   mem_post_g[l], ffn2_pre_g[l], ffn2_w_in[l].astype(BF16),
                   ffn2_w_down[l].astype(BF16), ffn2_post_g[l], batch)
    return xf.reshape(batch, seq, d)
```

```python
import functools
import math

import jax
import jax.numpy as jnp
import numpy as np
from jax import lax
from jax.experimental import pallas as pl
from jax.experimental.pallas import tpu as pltpu

F32 = jnp.float32
BF16 = jnp.bfloat16

D_MODEL = 1024
HG_HEADS = 8
HG_DK = 128
FOX_HEADS = 8
FOX_DH = 128
MEM_LEN = 256
MEM_HEADS = 4
MEM_DH = D_MODEL // MEM_HEADS
D_FF = 2816
EPS = 1e-6
LANES = 128

VMEM_LIMIT = 56 * 1024 * 1024

NT_DIMS = (((1,), (1,)), ((), ()))
TN_DIMS = (((0,), (0,)), ((), ()))


def _params(sem):
    return pltpu.CompilerParams(dimension_semantics=sem, vmem_limit_bytes=VMEM_LIMIT)


def _rms(x, g):
    ms = jnp.mean(x * x, axis=-1, keepdims=True)
    return x * lax.rsqrt(ms + EPS) * g


def _sigmoid(x):
    return 0.5 * jnp.tanh(0.5 * x) + 0.5


def _dot(a, b):
    return jnp.dot(a, b, preferred_element_type=F32)


def _norm_matmul_kernel(x_ref, g_ref, w_ref, o_ref, h_ref):
    @pl.when(pl.program_id(1) == 0)
    def _():
        h_ref[...] = _rms(x_ref[...], g_ref[...]).astype(BF16)

    o_ref[...] = _dot(h_ref[...], w_ref[...]).astype(o_ref.dtype)


def _norm_matmul(x, g, w, out_dtype, tm, tn):
    m, d = x.shape
    n = w.shape[1]
    return pl.pallas_call(
        _norm_matmul_kernel,
        grid=(m // tm, n // tn),
        in_specs=[
            pl.BlockSpec((tm, d), lambda i, j: (i, 0)),
            pl.BlockSpec((1, d), lambda i, j: (0, 0)),
            pl.BlockSpec((d, tn), lambda i, j: (0, j)),
        ],
        out_specs=pl.BlockSpec((tm, tn), lambda i, j: (i, j)),
        out_shape=jax.ShapeDtypeStruct((m, n), out_dtype),
        scratch_shapes=[pltpu.VMEM((tm, d), BF16)],
        compiler_params=_params(("parallel", "arbitrary")),
        name="norm_matmul",
    )(x, g.reshape(1, d), w)


def _resident(shape):
    return pl.BlockSpec(shape, lambda *_: (0,) * len(shape), pipeline_mode=pl.Buffered(1))


def _inproj_kernel(x_ref, g_ref, wt_ref, bias_ref, *out_refs, tn, acts):
    h = _rms(x_ref[...], g_ref[...]).astype(BF16)
    off = 0
    for o_ref, act in zip(out_refs, acts):
        n = o_ref.shape[1]
        for j in range(0, n, tn):
            width = min(tn, n - j)
            y = lax.dot_general(h, wt_ref[off + j:off + j + width, :], NT_DIMS,
                                preferred_element_type=F32)
            if act == "silu":
                y = y * _sigmoid(y)
            elif act == "sigmoid_bias":
                y = _sigmoid(y + bias_ref[:, j:j + width])
            o_ref[:, j:j + width] = y.astype(o_ref.dtype)
        off += n


def _inproj(x, g, w, bias, widths, dtypes, acts, tm=512, tn=512):
    m, d = x.shape
    return pl.pallas_call(
        functools.partial(_inproj_kernel, tn=tn, acts=acts),
        grid=(m // tm,),
        in_specs=[pl.BlockSpec((tm, d), lambda i: (i, 0)), _resident((1, d)), _resident(w.shape),
                  _resident(bias.shape)],
        out_specs=[pl.BlockSpec((tm, n), lambda i: (i, 0)) for n in widths],
        out_shape=[jax.ShapeDtypeStruct((m, n), dt) for n, dt in zip(widths, dtypes)],
        compiler_params=_params(("parallel",)),
        name="input_projection",
    )(x, g.reshape(1, d), w, bias)


FFN_CHUNK = 256


def _ffn_body(x, pre_ref, w_in_ref, w_down_ref, post_ref, acc_ref):
    h = _rms(x, pre_ref[...]).astype(BF16)
    for f in range(D_FF // FFN_CHUNK):
        cols = slice(f * FFN_CHUNK, (f + 1) * FFN_CHUNK)
        gate = _dot(h, w_in_ref[:, cols])
        up = _dot(h, w_in_ref[:, D_FF + f * FFN_CHUNK:D_FF + (f + 1) * FFN_CHUNK])
        act = (gate * _sigmoid(gate) * up).astype(BF16)
        part = _dot(act, w_down_ref[cols, :])
        if f == 0:
            acc_ref[...] = part
        else:
            acc_ref[...] += part
    return x + 0.5 * _rms(acc_ref[...], post_ref[...])


def _ffn_kernel(x_ref, pre_ref, w_in_ref, w_down_ref, post_ref, o_ref, acc_ref):
    o_ref[...] = _ffn_body(x_ref[...], pre_ref, w_in_ref, w_down_ref, post_ref, acc_ref)


def _ffn(x, pre_g, w_in, w_down, post_g, tm=512):
    m, d = x.shape
    return pl.pallas_call(
        _ffn_kernel,
        grid=(m // tm,),
        in_specs=[
            pl.BlockSpec((tm, d), lambda i: (i, 0)),
            _resident((1, d)),
            _resident(w_in.shape),
            _resident(w_down.shape),
            _resident((1, d)),
        ],
        out_specs=pl.BlockSpec((tm, d), lambda i: (i, 0)),
        out_shape=jax.ShapeDtypeStruct((m, d), F32),
        scratch_shapes=[pltpu.VMEM((tm, d), F32)],
        compiler_params=_params(("parallel",)),
        name="ffn",
    )(x, pre_g.reshape(1, d), w_in, w_down, post_g.reshape(1, d))


def _split3(x):
    hi = x.astype(BF16)
    r = x - hi.astype(F32)
    mid = r.astype(BF16)
    lo = (r - mid.astype(F32)).astype(BF16)
    return hi, mid, lo


def _fcum_kernel(fb_ref, bias_ref, c_ref, carry_ref):
    @pl.when(pl.program_id(1) == 0)
    def _():
        carry_ref[...] = jnp.zeros_like(carry_ref)

    z = fb_ref[...] + bias_ref[...]
    ls = jnp.minimum(z, 0.0) - jnp.log(1.0 + jnp.exp(-jnp.abs(z)))
    tb = z.shape[0]
    row = lax.broadcasted_iota(jnp.int32, (tb, tb), 0)
    col = lax.broadcasted_iota(jnp.int32, (tb, tb), 1)
    tril = (col <= row).astype(BF16)
    hi, mid, lo = _split3(ls)
    c = _dot(tril, hi) + _dot(tril, mid) + _dot(tril, lo) + carry_ref[...]
    c_ref[...] = c
    carry_ref[...] = c[tb - 1:tb, :]


def _fcum(fb, bias, batch, tb=512):
    m = fb.shape[0]
    ns = m // batch // tb
    return pl.pallas_call(
        _fcum_kernel,
        grid=(batch, ns),
        in_specs=[
            pl.BlockSpec((tb, LANES), lambda b, s: (b * ns + s, 0)),
            pl.BlockSpec((1, LANES), lambda b, s: (0, 0)),
        ],
        out_specs=pl.BlockSpec((tb, LANES), lambda b, s: (b * ns + s, 0)),
        out_shape=jax.ShapeDtypeStruct((m, LANES), F32),
        scratch_shapes=[pltpu.VMEM((1, LANES), F32)],
        compiler_params=_params(("parallel", "arbitrary")),
        name="fox_decay_cumsum",
    )(fb, bias)


def _hgrn_level_table(c):
    t = np.arange(c)[:, None]
    s = np.arange(c)[None, :]
    level = np.full((c, c), -2, np.int32)
    level[t == s] = -1
    for l in range(int(math.log2(c))):
        h = c >> (l + 1)
        mid = (t // (2 * h)) * (2 * h) + h
        sel = (t // (2 * h) == s // (2 * h)) & (t >= mid) & (s < mid)
        level[sel & (level == -2)] = l
    return level


SUBLANES = 8


def _prefix_sum_rows(x, row):
    k = 1
    while k < x.shape[0]:
        x = x + jnp.where(row >= k, pltpu.roll(x, k, axis=0), 0.0)
        k *= 2
    return x


def _hgrn_sign_table(c):
    t = np.arange(c)[:, None]
    halves = [c >> (l + 1) for l in range(int(math.log2(c)))]
    sign = [np.where((t % (2 * h)) >= h, 1.0, -1.0) * np.ones((1, HG_DK)) for h in halves]
    return np.stack(sign).astype(np.float32)


def _midpoint_rows(b, h):
    c, w = b.shape
    if h >= SUBLANES:
        pieces = [jnp.broadcast_to(b[j * 2 * h + h - 1:j * 2 * h + h, :], (2 * h, w))
                  for j in range(c // (2 * h))]
        return pieces[0] if len(pieces) == 1 else jnp.concatenate(pieces, axis=0)
    groups = c // SUBLANES
    b3 = b.reshape(groups, SUBLANES, w)
    sub = lax.broadcasted_iota(jnp.int32, (1, SUBLANES, w), 1)
    r3 = None
    for j in range(SUBLANES // (2 * h)):
        src = j * 2 * h + h - 1
        piece = jnp.broadcast_to(b3[:, src:src + 1, :], (groups, SUBLANES, w))
        r3 = piece if r3 is None else jnp.where(sub >= j * 2 * h, piece, r3)
    return r3.reshape(c, w)


def _hgrn_kernel(q_ref, f_ref, i_ref, lb_ref, lvl_ref, sign_ref, o_ref, st_ref, *, chunk, layer):
    @pl.when(pl.program_id(2) == 0)
    def _():
        st_ref[...] = jnp.zeros_like(st_ref)

    c = chunk
    n_lvl = int(math.log2(c))
    ts = q_ref.shape[0]
    logits = lb_ref[...]
    e = jnp.exp(logits - jnp.max(logits, axis=0, keepdims=True))
    lb = jnp.sum(e[0:layer + 1, :], axis=0, keepdims=True) / jnp.sum(e, axis=0, keepdims=True)
    lvl = lvl_ref[...]
    st = st_ref[...]

    row = lax.broadcasted_iota(jnp.int32, (c, HG_DK), 0)
    ks, bs = [], []
    for ci in range(ts // c):
        f = lb + (1.0 - lb) * _sigmoid(f_ref[pl.ds(ci * c, c), :])
        bs.append(_prefix_sum_rows(jnp.log2(f), row))
        ks.append(1.0 - f)

    chunks = range(ts // c)
    q16s = [q_ref[pl.ds(ci * c, c), :] for ci in chunks]
    vs = [i_ref[pl.ds(ci * c, c), :] for ci in chunks]
    k16s = [k.astype(BF16) for k in ks]

    a = [jnp.where(lvl == -1, lax.dot_general(q16s[ci], k16s[ci], NT_DIMS,
                                              preferred_element_type=F32), 0.0) for ci in chunks]
    for l in range(n_lvl):
        h = c >> (l + 1)
        for ci in chunks:
            d_l = (bs[ci] - _midpoint_rows(bs[ci], h)) * sign_ref[l]
            e_l = jnp.exp2(d_l).astype(BF16)
            p = lax.dot_general(q16s[ci] * e_l, k16s[ci] * e_l, NT_DIMS,
                                preferred_element_type=F32)
            a[ci] = jnp.where(lvl == l, p, a[ci])

    o_intra = [_dot(a[ci].astype(BF16), vs[ci]) for ci in chunks]
    upds = [lax.dot_general(vs[ci], (ks[ci] * jnp.exp2(bs[ci][c - 1:c, :] - bs[ci])).astype(BF16),
                            TN_DIMS, preferred_element_type=F32) for ci in chunks]

    for ci in chunks:
        o = o_intra[ci] + lax.dot_general(q16s[ci] * jnp.exp2(bs[ci]).astype(BF16),
                                          st.astype(BF16), NT_DIMS, preferred_element_type=F32)
        o_ref[pl.ds(ci * c, c), :] = o.astype(o_ref.dtype)
        st = st * jnp.exp2(bs[ci][c - 1:c, :]) + upds[ci]

    st_ref[...] = st


def _hgrn2(q_act, f_logit, inp, lb_logits, layer, batch, ts=2048, chunk=128):
    m = q_act.shape[0]
    ns = m // batch // ts
    level = _hgrn_level_table(chunk)
    sign = _hgrn_sign_table(chunk)
    nl = lb_logits.shape[0]
    blk = pl.BlockSpec((ts, HG_DK), lambda b, h, s: (b * ns + s, h))
    return pl.pallas_call(
        functools.partial(_hgrn_kernel, chunk=chunk, layer=layer),
        grid=(batch, HG_HEADS, ns),
        in_specs=[
            blk, blk, blk,
            pl.BlockSpec((nl, HG_DK), lambda b, h, s: (0, h)),
            _resident(level.shape), _resident(sign.shape),
        ],
        out_specs=blk,
        out_shape=jax.ShapeDtypeStruct((m, HG_HEADS * HG_DK), BF16),
        scratch_shapes=[pltpu.VMEM((HG_DK, HG_DK), F32)],
        compiler_params=_params(("parallel", "parallel", "arbitrary")),
        name="hgrn2",
    )(q_act, f_logit, inp, lb_logits.reshape(nl, HG_HEADS * HG_DK), jnp.asarray(level),
      jnp.asarray(sign))


LOG2E = 1.4426950408889634
FOX_FOLD = 8
FOX_SUM_ROWS = 16


def _head_column(c_blk, h):
    lane = lax.broadcasted_iota(jnp.int32, c_blk.shape, 1)
    return jnp.sum(jnp.where(lane == h, c_blk, 0.0), axis=1, keepdims=True)


def _decay_features(c_col):
    rows = c_col.shape[0]
    lane = lax.broadcasted_iota(jnp.int32, (rows, LANES), 1)
    hi, mid, lo = (part.astype(F32) for part in _split3(jnp.broadcast_to(c_col, (rows, LANES))))

    def place(base, sign):
        ones = (lane >= 3 - base) & (lane < 6 - base)
        feat = jnp.where(lane == base, sign * hi,
                         jnp.where(lane == base + 1, sign * mid,
                                   jnp.where(lane == base + 2, sign * lo,
                                             jnp.where(ones, 1.0, 0.0))))
        return feat.astype(BF16)

    return place(3, 1.0), place(0, -1.0)


def _fox_kernel(q_ref, k_ref, v_ref, c_ref, o_ref, kx_ref, qf_ref, vt_ref, qxt_ref,
                s0_ref, s1_ref, x0_ref, x1_ref, m_ref, acc_ref, *, tk):
    h = pl.program_id(1)
    qi = pl.program_id(2)
    tq = q_ref.shape[0]
    nk = kx_ref.shape[0]

    @pl.when(qi == 0)
    def _():
        def setup(kb, carry):
            rows = pl.ds(pl.multiple_of(kb * tk, tk), tk)
            q_feat, k_feat = _decay_features(_head_column(c_ref[rows, :], h) * LOG2E)
            kx_ref[kb, :, 0:FOX_DH] = k_ref[rows, :]
            kx_ref[kb, :, FOX_DH:2 * FOX_DH] = k_feat
            qf_ref[kb] = q_feat
            vt_ref[kb, 0:FOX_DH, :] = v_ref[rows, :].T
            ones_row = lax.broadcasted_iota(jnp.int32, (FOX_SUM_ROWS, tk), 0) == 0
            vt_ref[kb, FOX_DH:FOX_DH + FOX_SUM_ROWS, :] = ones_row.astype(F32).astype(BF16)
            return carry

        lax.fori_loop(0, nk, setup, 0)

    qxt_ref[...] = jnp.concatenate([q_ref[...], qf_ref[qi]], axis=1).T
    m_ref[...] = jnp.full_like(m_ref, -jnp.inf)
    acc_ref[...] = jnp.zeros_like(acc_ref)

    def fold(x, op):
        part = op(x.reshape(FOX_FOLD, tk // FOX_FOLD, tq), axis=0)
        return op(part, axis=0, keepdims=True)

    def scores(kb, s_ref, smax_ref):
        st = _dot(kx_ref[kb], qxt_ref[...])
        s_ref[...] = st
        smax_ref[...] = fold(st, jnp.max)

    def update(kb, s_ref, smax_ref, masked):
        st = s_ref[...]
        if masked:
            key = lax.broadcasted_iota(jnp.int32, st.shape, 0)
            qry = lax.broadcasted_iota(jnp.int32, st.shape, 1)
            st = jnp.where(key <= qry, st, -jnp.inf)
            blk_max = fold(st, jnp.max)
        else:
            blk_max = smax_ref[...]
        m_prev = m_ref[...]
        m_new = jnp.maximum(m_prev, blk_max)
        alpha = jnp.exp2(m_prev - m_new)
        p = jnp.exp2(st - m_new)
        m_ref[...] = m_new
        acc_ref[...] = alpha * acc_ref[...] + _dot(vt_ref[kb], p.astype(BF16))

    scores(0, s0_ref, x0_ref)

    def pair(kb):
        scores(kb + 1, s1_ref, x1_ref)
        update(kb, s0_ref, x0_ref, False)
        scores(kb + 2, s0_ref, x0_ref)
        update(kb + 1, s1_ref, x1_ref, False)

    def body4(j, carry):
        pair(4 * j)
        pair(4 * j + 2)
        return carry

    def body2(j, carry):
        pair(2 * j)
        return carry

    n_pairs = qi // 2
    lax.fori_loop(0, n_pairs // 2, body4, 0)
    lax.fori_loop(2 * (n_pairs // 2), n_pairs, body2, 0)

    @pl.when(qi % 2 == 1)
    def _():
        scores(qi, s1_ref, x1_ref)
        update(qi - 1, s0_ref, x0_ref, False)
        update(qi, s1_ref, x1_ref, True)

    @pl.when(qi % 2 == 0)
    def _():
        update(qi, s0_ref, x0_ref, True)

    out = acc_ref[0:FOX_DH, :] / acc_ref[FOX_DH:FOX_DH + 1, :]
    o_ref[...] = out.T.astype(o_ref.dtype)


def _fox(proj_b, c, batch, seq, tq=512):
    m = proj_b.shape[0]
    nq = seq // tq
    return pl.pallas_call(
        functools.partial(_fox_kernel, tk=tq),
        grid=(batch, FOX_HEADS, nq),
        in_specs=[
            pl.BlockSpec((tq, FOX_DH), lambda b, h, i: (b * nq + i, h)),
            pl.BlockSpec((seq, FOX_DH), lambda b, h, i: (b, FOX_HEADS + h)),
            pl.BlockSpec((seq, FOX_DH), lambda b, h, i: (b, 2 * FOX_HEADS + h)),
            pl.BlockSpec((seq, LANES), lambda b, h, i: (b, 0)),
        ],
        out_specs=pl.BlockSpec((tq, FOX_DH), lambda b, h, i: (b * nq + i, h)),
        out_shape=jax.ShapeDtypeStruct((m, FOX_HEADS * FOX_DH), BF16),
        scratch_shapes=[
            pltpu.VMEM((nq, tq, 2 * FOX_DH), BF16),
            pltpu.VMEM((nq, tq, FOX_DH), BF16),
            pltpu.VMEM((nq, FOX_DH + FOX_SUM_ROWS, tq), BF16),
            pltpu.VMEM((2 * FOX_DH, tq), BF16),
            pltpu.VMEM((tq, tq), F32),
            pltpu.VMEM((tq, tq), F32),
            pltpu.VMEM((1, tq), F32),
            pltpu.VMEM((1, tq), F32),
            pltpu.VMEM((1, tq), F32),
            pltpu.VMEM((FOX_DH + FOX_SUM_ROWS, tq), F32),
        ],
        compiler_params=_params(("parallel", "parallel", "arbitrary")),
        name="fox_attention",
    )(proj_b, proj_b, proj_b, c)


def _merge_body(x, oa_ref, ga_ref, ob_ref, g0_ref, g1_ref, hg_ref, wa_ref, wb_ref, wo_ref, post_ref):
    oa = _rms(oa_ref[...].astype(F32), hg_ref[...]) * ga_ref[...].astype(F32)
    ya = _dot(oa.astype(BF16), wa_ref[...])
    yb = _dot(ob_ref[...], wb_ref[...])
    y = g0_ref[...].astype(F32) * ya + g1_ref[...].astype(F32) * yb
    z = _dot(y.astype(BF16), wo_ref[...])
    return x + _rms(z, post_ref[...])


def _memattn_body(x, pre_ref, kv_ref, wq_ref, wo_ref, post_ref):
    h = _rms(x, pre_ref[...]).astype(BF16)
    q = _dot(h, wq_ref[...]).astype(BF16)
    heads = []
    for hd in range(MEM_HEADS):
        lo = hd * MEM_DH
        kh = kv_ref[:, lo:lo + MEM_DH]
        vh = kv_ref[:, D_MODEL + lo:D_MODEL + lo + MEM_DH]
        s = lax.dot_general(q[:, lo:lo + MEM_DH], kh, NT_DIMS, preferred_element_type=F32)
        p = jnp.exp(s - jnp.max(s, axis=1, keepdims=True))
        inv = 1.0 / jnp.sum(p, axis=1, keepdims=True)
        heads.append((_dot(p.astype(BF16), vh) * inv).astype(BF16))
    o = jnp.concatenate(heads, axis=1)
    return x + _rms(_dot(o, wo_ref[...]), post_ref[...])


def _tail_kernel(x_ref, oa_ref, ga_ref, ob_ref, g0_ref, g1_ref, kv_ref, hg_ref, wa_ref, wb_ref,
                 wo_ref, mix_post_ref, mem_pre_ref, wq_ref, wmo_ref, mem_post_ref,
                 ffn_pre_ref, w_in_ref, w_down_ref, ffn_post_ref, o_ref, acc_ref):
    x = _merge_body(x_ref[...], oa_ref, ga_ref, ob_ref, g0_ref, g1_ref, hg_ref, wa_ref, wb_ref,
                    wo_ref, mix_post_ref)
    x = _memattn_body(x, mem_pre_ref, kv_ref, wq_ref, wmo_ref, mem_post_ref)
    o_ref[...] = _ffn_body(x, ffn_pre_ref, w_in_ref, w_down_ref, ffn_post_ref, acc_ref)


def _tail(x, o_a, g_act, o_b, gates, kv, hg_norm_g, w_a, w_b, w_o, mix_post_g, mem_pre_g, w_q,
          w_mo, mem_post_g, ffn_pre_g, w_in, w_down, ffn_post_g, batch, tm=512):
    m, d = x.shape
    ns = m // batch // tm
    row = lambda c: pl.BlockSpec((tm, d), lambda b, s: (b * ns + s, c))
    vec, mat = _resident((1, d)), _resident((d, d))
    r1 = lambda g: g.reshape(1, d)
    return pl.pallas_call(
        _tail_kernel,
        grid=(batch, ns),
        in_specs=[row(0), row(0), row(0), row(0), row(0), row(1),
                  pl.BlockSpec((MEM_LEN, 2 * d), lambda b, s: (b, 0)),
                  vec, mat, mat, mat, vec, vec, mat, mat, vec,
                  vec, _resident(w_in.shape), _resident(w_down.shape), vec],
        out_specs=row(0),
        out_shape=jax.ShapeDtypeStruct((m, d), F32),
        scratch_shapes=[pltpu.VMEM((tm, d), F32)],
        compiler_params=_params(("parallel", "parallel")),
        name="merge_memattn_ffn",
    )(x, o_a, g_act, o_b, gates, gates, kv, r1(hg_norm_g), w_a, w_b, w_o, r1(mix_post_g),
      r1(mem_pre_g), w_q, w_mo, r1(mem_post_g), r1(ffn_pre_g), w_in, w_down, r1(ffn_post_g))


def kernel(x, mem, ffn1_pre_g, ffn1_w_in, ffn1_w_down, ffn1_post_g, mix_pre_g, w_in, hg_lb_logits, hg_norm_g, fox_f_bias, w_branch_a, w_branch_b, b_gate, w_out, mix_post_g, mem_pre_g, mem_kv_g, w_mq, w_mkv, w_mo, mem_post_g, ffn2_pre_g, ffn2_w_in, ffn2_w_down, ffn2_post_g):
    batch, seq, d = x.shape
    depth = ffn1_w_in.shape[0]
    xf = x.reshape(batch * seq, d)
    memf = mem.reshape(batch * MEM_LEN, d)
    kw = HG_HEADS * HG_DK
    off_b = 4 * kw
    off_f = off_b + 3 * FOX_HEADS * FOX_DH
    off_g = off_f + FOX_HEADS
    for l in range(depth):
        xf = _ffn(xf, ffn1_pre_g[l], ffn1_w_in[l].astype(BF16), ffn1_w_down[l].astype(BF16),
                  ffn1_post_g[l])

        wt = jnp.swapaxes(w_in[l], 0, 1)
        fox_w = FOX_HEADS * FOX_DH
        w_cat = jnp.concatenate([
            wt[:off_b],
            wt[off_b:off_b + fox_w] * (LOG2E / math.sqrt(FOX_DH)),
            wt[off_b + fox_w:off_f],
            jnp.pad(wt[off_f:off_g], ((0, LANES - FOX_HEADS), (0, 0))),
            wt[off_g:],
        ], axis=0).astype(BF16)
        q_act, f_logit, inp, g_act, proj_b, fb, gates = _inproj(
            xf, mix_pre_g[l], w_cat, b_gate[l].reshape(1, 2 * d),
            (kw, kw, kw, kw, 3 * fox_w, LANES, 2 * d),
            (BF16, F32, BF16, BF16, BF16, F32, BF16),
            ("silu", None, None, "silu", None, None, "sigmoid_bias"))

        bias = jnp.pad(fox_f_bias[l], (0, LANES - FOX_HEADS)).reshape(1, LANES)
        c = _fcum(fb, bias, batch)

        o_a = _hgrn2(q_act, f_logit, inp, hg_lb_logits, l, batch)
        o_b = _fox(proj_b, c, batch, seq)

        kv = _norm_matmul(memf, mem_kv_g[l], w_mkv[l].astype(BF16), BF16, MEM_LEN, 1024)
        xf = _tail(xf, o_a, g_act, o_b, gates, kv, hg_norm_g[l],
                   w_branch_a[l].astype(BF16), w_branch_b[l].astype(BF16), w_out[l].astype(BF16),
                   mix_post_g[l], mem_pre_g[l],
                   (w_mq[l] * (1.0 / math.sqrt(MEM_DH))).astype(BF16), w_mo[l].astype(BF16),
                   mem_post_g[l], ffn2_pre_g[l], ffn2_w_in[l].astype(BF16),
                   ffn2_w_down[l].astype(BF16), ffn2_post_g[l], batch)
    return xf.reshape(batch, seq, d)
```

```python
import functools
import math

import jax
import jax.numpy as jnp
import numpy as np
from jax import lax
from jax.experimental import pallas as pl
from jax.experimental.pallas import tpu as pltpu

F32 = jnp.float32
BF16 = jnp.bfloat16

D_MODEL = 1024
HG_HEADS = 8
HG_DK = 128
FOX_HEADS = 8
FOX_DH = 128
MEM_LEN = 256
MEM_HEADS = 4
MEM_DH = D_MODEL // MEM_HEADS
D_FF = 2816
EPS = 1e-6
LANES = 128

VMEM_LIMIT = 56 * 1024 * 1024

NT_DIMS = (((1,), (1,)), ((), ()))
TN_DIMS = (((0,), (0,)), ((), ()))


def _params(sem):
    return pltpu.CompilerParams(dimension_semantics=sem, vmem_limit_bytes=VMEM_LIMIT)


def _rms(x, g):
    ms = jnp.mean(x * x, axis=-1, keepdims=True)
    return x * lax.rsqrt(ms + EPS) * g


def _sigmoid(x):
    return 0.5 * jnp.tanh(0.5 * x) + 0.5


def _dot(a, b):
    return jnp.dot(a, b, preferred_element_type=F32)


def _norm_matmul_kernel(x_ref, g_ref, w_ref, o_ref, h_ref):
    @pl.when(pl.program_id(1) == 0)
    def _():
        h_ref[...] = _rms(x_ref[...], g_ref[...]).astype(BF16)

    o_ref[...] = _dot(h_ref[...], w_ref[...]).astype(o_ref.dtype)


def _norm_matmul(x, g, w, out_dtype, tm, tn):
    m, d = x.shape
    n = w.shape[1]
    return pl.pallas_call(
        _norm_matmul_kernel,
        grid=(m // tm, n // tn),
        in_specs=[
            pl.BlockSpec((tm, d), lambda i, j: (i, 0)),
            pl.BlockSpec((1, d), lambda i, j: (0, 0)),
            pl.BlockSpec((d, tn), lambda i, j: (0, j)),
        ],
        out_specs=pl.BlockSpec((tm, tn), lambda i, j: (i, j)),
        out_shape=jax.ShapeDtypeStruct((m, n), out_dtype),
        scratch_shapes=[pltpu.VMEM((tm, d), BF16)],
        compiler_params=_params(("parallel", "arbitrary")),
        name="norm_matmul",
    )(x, g.reshape(1, d), w)


def _resident(shape):
    return pl.BlockSpec(shape, lambda *_: (0,) * len(shape), pipeline_mode=pl.Buffered(1))


def _inproj_kernel(x_ref, g_ref, wt_ref, bias_ref, *out_refs, tn, acts):
    h = _rms(x_ref[...], g_ref[...]).astype(BF16)
    off = 0
    for o_ref, act in zip(out_refs, acts):
        n = o_ref.shape[1]
        for j in range(0, n, tn):
            width = min(tn, n - j)
            y = lax.dot_general(h, wt_ref[off + j:off + j + width, :], NT_DIMS,
                                preferred_element_type=F32)
            if act == "silu":
                y = y * _sigmoid(y)
            elif act == "sigmoid_bias":
                y = _sigmoid(y + bias_ref[:, j:j + width])
            o_ref[:, j:j + width] = y.astype(o_ref.dtype)
        off += n


def _inproj(x, g, w, bias, widths, dtypes, acts, tm=512, tn=512):
    m, d = x.shape
    return pl.pallas_call(
        functools.partial(_inproj_kernel, tn=tn, acts=acts),
        grid=(m // tm,),
        in_specs=[pl.BlockSpec((tm, d), lambda i: (i, 0)), _resident((1, d)), _resident(w.shape),
                  _resident(bias.shape)],
        out_specs=[pl.BlockSpec((tm, n), lambda i: (i, 0)) for n in widths],
        out_shape=[jax.ShapeDtypeStruct((m, n), dt) for n, dt in zip(widths, dtypes)],
        compiler_params=_params(("parallel",)),
        name="input_projection",
    )(x, g.reshape(1, d), w, bias)


FFN_CHUNK = 256


def _ffn_body(x, pre_ref, w_in_ref, w_down_ref, post_ref, acc_ref):
    h = _rms(x, pre_ref[...]).astype(BF16)
    for f in range(D_FF // FFN_CHUNK):
        cols = slice(f * FFN_CHUNK, (f + 1) * FFN_CHUNK)
        gate = _dot(h, w_in_ref[:, cols])
        up = _dot(h, w_in_ref[:, D_FF + f * FFN_CHUNK:D_FF + (f + 1) * FFN_CHUNK])
        act = (gate * _sigmoid(gate) * up).astype(BF16)
        part = _dot(act, w_down_ref[cols, :])
        if f == 0:
            acc_ref[...] = part
        else:
            acc_ref[...] += part
    return x + 0.5 * _rms(acc_ref[...], post_ref[...])


def _ffn_kernel(x_ref, pre_ref, w_in_ref, w_down_ref, post_ref, o_ref, acc_ref):
    o_ref[...] = _ffn_body(x_ref[...], pre_ref, w_in_ref, w_down_ref, post_ref, acc_ref)


def _ffn(x, pre_g, w_in, w_down, post_g, tm=512):
    m, d = x.shape
    return pl.pallas_call(
        _ffn_kernel,
        grid=(m // tm,),
        in_specs=[
            pl.BlockSpec((tm, d), lambda i: (i, 0)),
            _resident((1, d)),
            _resident(w_in.shape),
            _resident(w_down.shape),
            _resident((1, d)),
        ],
        out_specs=pl.BlockSpec((tm, d), lambda i: (i, 0)),
        out_shape=jax.ShapeDtypeStruct((m, d), F32),
        scratch_shapes=[pltpu.VMEM((tm, d), F32)],
        compiler_params=_params(("parallel",)),
        name="ffn",
    )(x, pre_g.reshape(1, d), w_in, w_down, post_g.reshape(1, d))


def _split3(x):
    hi = x.astype(BF16)
    r = x - hi.astype(F32)
    mid = r.astype(BF16)
    lo = (r - mid.astype(F32)).astype(BF16)
    return hi, mid, lo


def _fcum_kernel(fb_ref, bias_ref, c_ref, carry_ref):
    @pl.when(pl.program_id(1) == 0)
    def _():
        carry_ref[...] = jnp.zeros_like(carry_ref)

    z = fb_ref[...] + bias_ref[...]
    ls = jnp.minimum(z, 0.0) - jnp.log(1.0 + jnp.exp(-jnp.abs(z)))
    tb = z.shape[0]
    row = lax.broadcasted_iota(jnp.int32, (tb, tb), 0)
    col = lax.broadcasted_iota(jnp.int32, (tb, tb), 1)
    tril = (col <= row).astype(BF16)
    hi, mid, lo = _split3(ls)
    c = _dot(tril, hi) + _dot(tril, mid) + _dot(tril, lo) + carry_ref[...]
    c_ref[...] = c
    carry_ref[...] = c[tb - 1:tb, :]


def _fcum(fb, bias, batch, tb=512):
    m = fb.shape[0]
    ns = m // batch // tb
    return pl.pallas_call(
        _fcum_kernel,
        grid=(batch, ns),
        in_specs=[
            pl.BlockSpec((tb, LANES), lambda b, s: (b * ns + s, 0)),
            pl.BlockSpec((1, LANES), lambda b, s: (0, 0)),
        ],
        out_specs=pl.BlockSpec((tb, LANES), lambda b, s: (b * ns + s, 0)),
        out_shape=jax.ShapeDtypeStruct((m, LANES), F32),
        scratch_shapes=[pltpu.VMEM((1, LANES), F32)],
        compiler_params=_params(("parallel", "arbitrary")),
        name="fox_decay_cumsum",
    )(fb, bias)


def _hgrn_level_table(c):
    t = np.arange(c)[:, None]
    s = np.arange(c)[None, :]
    level = np.full((c, c), -2, np.int32)
    level[t == s] = -1
    for l in range(int(math.log2(c))):
        h = c >> (l + 1)
        mid = (t // (2 * h)) * (2 * h) + h
        sel = (t // (2 * h) == s // (2 * h)) & (t >= mid) & (s < mid)
        level[sel & (level == -2)] = l
    return level


SUBLANES = 8


def _prefix_sum_rows(x, row):
    k = 1
    while k < x.shape[0]:
        x = x + jnp.where(row >= k, pltpu.roll(x, k, axis=0), 0.0)
        k *= 2
    return x


def _hgrn_sign_table(c):
    t = np.arange(c)[:, None]
    halves = [c >> (l + 1) for l in range(int(math.log2(c)))]
    sign = [np.where((t % (2 * h)) >= h, 1.0, -1.0) * np.ones((1, HG_DK)) for h in halves]
    return np.stack(sign).astype(np.float32)


def _midpoint_rows(b, h):
    c, w = b.shape
    if h >= SUBLANES:
        pieces = [jnp.broadcast_to(b[j * 2 * h + h - 1:j * 2 * h + h, :], (2 * h, w))
                  for j in range(c // (2 * h))]
        return pieces[0] if len(pieces) == 1 else jnp.concatenate(pieces, axis=0)
    groups = c // SUBLANES
    b3 = b.reshape(groups, SUBLANES, w)
    sub = lax.broadcasted_iota(jnp.int32, (1, SUBLANES, w), 1)
    r3 = None
    for j in range(SUBLANES // (2 * h)):
        src = j * 2 * h + h - 1
        piece = jnp.broadcast_to(b3[:, src:src + 1, :], (groups, SUBLANES, w))
        r3 = piece if r3 is None else jnp.where(sub >= j * 2 * h, piece, r3)
    return r3.reshape(c, w)


def _hgrn_kernel(q_ref, f_ref, i_ref, lb_ref, lvl_ref, sign_ref, o_ref, st_ref, *, chunk, layer):
    @pl.when(pl.program_id(2) == 0)
    def _():
        st_ref[...] = jnp.zeros_like(st_ref)

    c = chunk
    n_lvl = int(math.log2(c))
    ts = q_ref.shape[0]
    logits = lb_ref[...]
    e = jnp.exp(logits - jnp.max(logits, axis=0, keepdims=True))
    lb = jnp.sum(e[0:layer + 1, :], axis=0, keepdims=True) / jnp.sum(e, axis=0, keepdims=True)
    lvl = lvl_ref[...]
    st = st_ref[...]

    row = lax.broadcasted_iota(jnp.int32, (c, HG_DK), 0)
    ks, bs = [], []
    for ci in range(ts // c):
        f = lb + (1.0 - lb) * _sigmoid(f_ref[pl.ds(ci * c, c), :])
        bs.append(_prefix_sum_rows(jnp.log2(f), row))
        ks.append(1.0 - f)

    chunks = range(ts // c)
    q16s = [q_ref[pl.ds(ci * c, c), :] for ci in chunks]
    vs = [i_ref[pl.ds(ci * c, c), :] for ci in chunks]
    k16s = [k.astype(BF16) for k in ks]

    a = [jnp.where(lvl == -1, lax.dot_general(q16s[ci], k16s[ci], NT_DIMS,
                                              preferred_element_type=F32), 0.0) for ci in chunks]
    for l in range(n_lvl):
        h = c >> (l + 1)
        for ci in chunks:
            d_l = (bs[ci] - _midpoint_rows(bs[ci], h)) * sign_ref[l]
            e_l = jnp.exp2(d_l).astype(BF16)
            p = lax.dot_general(q16s[ci] * e_l, k16s[ci] * e_l, NT_DIMS,
                                preferred_element_type=F32)
            a[ci] = jnp.where(lvl == l, p, a[ci])

    o_intra = [_dot(a[ci].astype(BF16), vs[ci]) for ci in chunks]
    upds = [lax.dot_general(vs[ci], (ks[ci] * jnp.exp2(bs[ci][c - 1:c, :] - bs[ci])).astype(BF16),
                            TN_DIMS, preferred_element_type=F32) for ci in chunks]

    for ci in chunks:
        o = o_intra[ci] + lax.dot_general(q16s[ci] * jnp.exp2(bs[ci]).astype(BF16),
                                          st.astype(BF16), NT_DIMS, preferred_element_type=F32)
        o_ref[pl.ds(ci * c, c), :] = o.astype(o_ref.dtype)
        st = st * jnp.exp2(bs[ci][c - 1:c, :]) + upds[ci]

    st_ref[...] = st


def _hgrn2(q_act, f_logit, inp, lb_logits, layer, batch, ts=2048, chunk=128):
    m = q_act.shape[0]
    ns = m // batch // ts
    level = _hgrn_level_table(chunk)
    sign = _hgrn_sign_table(chunk)
    nl = lb_logits.shape[0]
    blk = pl.BlockSpec((ts, HG_DK), lambda b, h, s: (b * ns + s, h))
    return pl.pallas_call(
        functools.partial(_hgrn_kernel, chunk=chunk, layer=layer),
        grid=(batch, HG_HEADS, ns),
        in_specs=[
            blk, blk, blk,
            pl.BlockSpec((nl, HG_DK), lambda b, h, s: (0, h)),
            _resident(level.shape), _resident(sign.shape),
        ],
        out_specs=blk,
        out_shape=jax.ShapeDtypeStruct((m, HG_HEADS * HG_DK), BF16),
        scratch_shapes=[pltpu.VMEM((HG_DK, HG_DK), F32)],
        compiler_params=_params(("parallel", "parallel", "arbitrary")),
        name="hgrn2",
    )(q_act, f_logit, inp, lb_logits.reshape(nl, HG_HEADS * HG_DK), jnp.asarray(level),
      jnp.asarray(sign))


LOG2E = 1.4426950408889634
FOX_FOLD = 8
FOX_SUM_ROWS = 16


def _head_column(c_blk, h):
    lane = lax.broadcasted_iota(jnp.int32, c_blk.shape, 1)
    return jnp.sum(jnp.where(lane == h, c_blk, 0.0), axis=1, keepdims=True)


def _decay_features(c_col):
    rows = c_col.shape[0]
    lane = lax.broadcasted_iota(jnp.int32, (rows, LANES), 1)
    hi, mid, lo = (part.astype(F32) for part in _split3(jnp.broadcast_to(c_col, (rows, LANES))))

    def place(base, sign):
        ones = (lane >= 3 - base) & (lane < 6 - base)
        feat = jnp.where(lane == base, sign * hi,
                         jnp.where(lane == base + 1, sign * mid,
                                   jnp.where(lane == base + 2, sign * lo,
                                             jnp.where(ones, 1.0, 0.0))))
        return feat.astype(BF16)

    return place(3, 1.0), place(0, -1.0)


def _fox_kernel(q_ref, k_ref, v_ref, c_ref, o_ref, kx_ref, qf_ref, vt_ref, qxt_ref,
                s0_ref, s1_ref, x0_ref, x1_ref, m_ref, acc_ref, *, tk):
    h = pl.program_id(1)
    qi = pl.program_id(2)
    tq = q_ref.shape[0]
    nk = kx_ref.shape[0]

    @pl.when(qi == 0)
    def _():
        def setup(kb, carry):
            rows = pl.ds(pl.multiple_of(kb * tk, tk), tk)
            q_feat, k_feat = _decay_features(_head_column(c_ref[rows, :], h) * LOG2E)
            kx_ref[kb, :, 0:FOX_DH] = k_ref[rows, :]
            kx_ref[kb, :, FOX_DH:2 * FOX_DH] = k_feat
            qf_ref[kb] = q_feat
            vt_ref[kb, 0:FOX_DH, :] = v_ref[rows, :].T
            ones_row = lax.broadcasted_iota(jnp.int32, (FOX_SUM_ROWS, tk), 0) == 0
            vt_ref[kb, FOX_DH:FOX_DH + FOX_SUM_ROWS, :] = ones_row.astype(F32).astype(BF16)
            return carry

        lax.fori_loop(0, nk, setup, 0)

    ratio = tq // tk
    n_full = ratio * qi
    q_feat = [qf_ref[n_full + j] for j in range(ratio)]
    q_feat = q_feat[0] if ratio == 1 else jnp.concatenate(q_feat, axis=0)
    qxt_ref[...] = jnp.concatenate([q_ref[...], q_feat], axis=1).T
    m_ref[...] = jnp.full_like(m_ref, -jnp.inf)
    acc_ref[...] = jnp.zeros_like(acc_ref)

    def fold(x, op):
        part = op(x.reshape(FOX_FOLD, tk // FOX_FOLD, tq), axis=0)
        return op(part, axis=0, keepdims=True)

    def scores(kb, s_ref, smax_ref):
        st = _dot(kx_ref[kb], qxt_ref[...])
        s_ref[...] = st
        smax_ref[...] = fold(st, jnp.max)

    def update(kb, s_ref, smax_ref, diag_offset=None):
        st = s_ref[...]
        if diag_offset is not None:
            key = lax.broadcasted_iota(jnp.int32, st.shape, 0) + diag_offset
            qry = lax.broadcasted_iota(jnp.int32, st.shape, 1)
            st = jnp.where(key <= qry, st, -jnp.inf)
            blk_max = fold(st, jnp.max)
        else:
            blk_max = smax_ref[...]
        m_prev = m_ref[...]
        m_new = jnp.maximum(m_prev, blk_max)
        alpha = jnp.exp2(m_prev - m_new)
        p = jnp.exp2(st - m_new)
        m_ref[...] = m_new
        acc_ref[...] = alpha * acc_ref[...] + _dot(vt_ref[kb], p.astype(BF16))

    scores(0, s0_ref, x0_ref)

    def pair(kb):
        scores(kb + 1, s1_ref, x1_ref)
        update(kb, s0_ref, x0_ref)
        scores(kb + 2, s0_ref, x0_ref)
        update(kb + 1, s1_ref, x1_ref)

    def body4(j, carry):
        pair(4 * j)
        pair(4 * j + 2)
        return carry

    def body2(j, carry):
        pair(2 * j)
        return carry

    n_pairs = n_full // 2
    lax.fori_loop(0, n_pairs // 2, body4, 0)
    lax.fori_loop(2 * (n_pairs // 2), n_pairs, body2, 0)

    if ratio == 2:
        scores(n_full + 1, s1_ref, x1_ref)
        update(n_full, s0_ref, x0_ref, 0)
        update(n_full + 1, s1_ref, x1_ref, tk)
    else:
        @pl.when(qi % 2 == 1)
        def _():
            scores(qi, s1_ref, x1_ref)
            update(qi - 1, s0_ref, x0_ref)
            update(qi, s1_ref, x1_ref, 0)

        @pl.when(qi % 2 == 0)
        def _():
            update(qi, s0_ref, x0_ref, 0)

    out = acc_ref[0:FOX_DH, :] / acc_ref[FOX_DH:FOX_DH + 1, :]
    o_ref[...] = out.T.astype(o_ref.dtype)


def _fox(proj_b, c, batch, seq, tq=1024, tk=512):
    assert tq in (tk, 2 * tk)
    m = proj_b.shape[0]
    nq = seq // tq
    nk = seq // tk
    return pl.pallas_call(
        functools.partial(_fox_kernel, tk=tk),
        grid=(batch, FOX_HEADS, nq),
        in_specs=[
            pl.BlockSpec((tq, FOX_DH), lambda b, h, i: (b * nq + i, h)),
            pl.BlockSpec((seq, FOX_DH), lambda b, h, i: (b, FOX_HEADS + h)),
            pl.BlockSpec((seq, FOX_DH), lambda b, h, i: (b, 2 * FOX_HEADS + h)),
            pl.BlockSpec((seq, LANES), lambda b, h, i: (b, 0)),
        ],
        out_specs=pl.BlockSpec((tq, FOX_DH), lambda b, h, i: (b * nq + i, h)),
        out_shape=jax.ShapeDtypeStruct((m, FOX_HEADS * FOX_DH), BF16),
        scratch_shapes=[
            pltpu.VMEM((nk, tk, 2 * FOX_DH), BF16),
            pltpu.VMEM((nk, tk, FOX_DH), BF16),
            pltpu.VMEM((nk, FOX_DH + FOX_SUM_ROWS, tk), BF16),
            pltpu.VMEM((2 * FOX_DH, tq), BF16),
            pltpu.VMEM((tk, tq), F32),
            pltpu.VMEM((tk, tq), F32),
            pltpu.VMEM((1, tq), F32),
            pltpu.VMEM((1, tq), F32),
            pltpu.VMEM((1, tq), F32),
            pltpu.VMEM((FOX_DH + FOX_SUM_ROWS, tq), F32),
        ],
        compiler_params=_params(("parallel", "parallel", "arbitrary")),
        name="fox_attention",
    )(proj_b, proj_b, proj_b, c)


def _merge_body(x, oa_ref, ga_ref, ob_ref, g0_ref, g1_ref, hg_ref, wa_ref, wb_ref, wo_ref, post_ref):
    oa = _rms(oa_ref[...].astype(F32), hg_ref[...]) * ga_ref[...].astype(F32)
    ya = _dot(oa.astype(BF16), wa_ref[...])
    yb = _dot(ob_ref[...], wb_ref[...])
    y = g0_ref[...].astype(F32) * ya + g1_ref[...].astype(F32) * yb
    z = _dot(y.astype(BF16), wo_ref[...])
    return x + _rms(z, post_ref[...])


def _memattn_body(x, pre_ref, kv_ref, wq_ref, wo_ref, post_ref):
    h = _rms(x, pre_ref[...]).astype(BF16)
    q = _dot(h, wq_ref[...]).astype(BF16)
    heads = []
    for hd in range(MEM_HEADS):
        lo = hd * MEM_DH
        kh = kv_ref[:, lo:lo + MEM_DH]
        vh = kv_ref[:, D_MODEL + lo:D_MODEL + lo + MEM_DH]
        s = lax.dot_general(q[:, lo:lo + MEM_DH], kh, NT_DIMS, preferred_element_type=F32)
        p = jnp.exp(s - jnp.max(s, axis=1, keepdims=True))
        inv = 1.0 / jnp.sum(p, axis=1, keepdims=True)
        heads.append((_dot(p.astype(BF16), vh) * inv).astype(BF16))
    o = jnp.concatenate(heads, axis=1)
    return x + _rms(_dot(o, wo_ref[...]), post_ref[...])


def _tail_kernel(x_ref, oa_ref, ga_ref, ob_ref, g0_ref, g1_ref, kv_ref, hg_ref, wa_ref, wb_ref,
                 wo_ref, mix_post_ref, mem_pre_ref, wq_ref, wmo_ref, mem_post_ref,
                 ffn_pre_ref, w_in_ref, w_down_ref, ffn_post_ref, o_ref, acc_ref):
    x = _merge_body(x_ref[...], oa_ref, ga_ref, ob_ref, g0_ref, g1_ref, hg_ref, wa_ref, wb_ref,
                    wo_ref, mix_post_ref)
    x = _memattn_body(x, mem_pre_ref, kv_ref, wq_ref, wmo_ref, mem_post_ref)
    o_ref[...] = _ffn_body(x, ffn_pre_ref, w_in_ref, w_down_ref, ffn_post_ref, acc_ref)


def _tail(x, o_a, g_act, o_b, gates, kv, hg_norm_g, w_a, w_b, w_o, mix_post_g, mem_pre_g, w_q,
          w_mo, mem_post_g, ffn_pre_g, w_in, w_down, ffn_post_g, batch, tm=512):
    m, d = x.shape
    ns = m // batch // tm
    row = lambda c: pl.BlockSpec((tm, d), lambda b, s: (b * ns + s, c))
    vec, mat = _resident((1, d)), _resident((d, d))
    r1 = lambda g: g.reshape(1, d)
    return pl.pallas_call(
        _tail_kernel,
        grid=(batch, ns),
        in_specs=[row(0), row(0), row(0), row(0), row(0), row(1),
                  pl.BlockSpec((MEM_LEN, 2 * d), lambda b, s: (b, 0)),
                  vec, mat, mat, mat, vec, vec, mat, mat, vec,
                  vec, _resident(w_in.shape), _resident(w_down.shape), vec],
        out_specs=row(0),
        out_shape=jax.ShapeDtypeStruct((m, d), F32),
        scratch_shapes=[pltpu.VMEM((tm, d), F32)],
        compiler_params=_params(("parallel", "parallel")),
        name="merge_memattn_ffn",
    )(x, o_a, g_act, o_b, gates, gates, kv, r1(hg_norm_g), w_a, w_b, w_o, r1(mix_post_g),
      r1(mem_pre_g), w_q, w_mo, r1(mem_post_g), r1(ffn_pre_g), w_in, w_down, r1(ffn_post_g))


def kernel(x, mem, ffn1_pre_g, ffn1_w_in, ffn1_w_down, ffn1_post_g, mix_pre_g, w_in, hg_lb_logits, hg_norm_g, fox_f_bias, w_branch_a, w_branch_b, b_gate, w_out, mix_post_g, mem_pre_g, mem_kv_g, w_mq, w_mkv, w_mo, mem_post_g, ffn2_pre_g, ffn2_w_in, ffn2_w_down, ffn2_post_g):
    batch, seq, d = x.shape
    depth = ffn1_w_in.shape[0]
    xf = x.reshape(batch * seq, d)
    memf = mem.reshape(batch * MEM_LEN, d)
    kw = HG_HEADS * HG_DK
    off_b = 4 * kw
    off_f = off_b + 3 * FOX_HEADS * FOX_DH
    off_g = off_f + FOX_HEADS
    for l in range(depth):
        xf = _ffn(xf, ffn1_pre_g[l], ffn1_w_in[l].astype(BF16), ffn1_w_down[l].astype(BF16),
                  ffn1_post_g[l])

        wt = jnp.swapaxes(w_in[l], 0, 1)
        fox_w = FOX_HEADS * FOX_DH
        w_cat = jnp.concatenate([
            wt[:off_b],
            wt[off_b:off_b + fox_w] * (LOG2E / math.sqrt(FOX_DH)),
            wt[off_b + fox_w:off_f],
            jnp.pad(wt[off_f:off_g], ((0, LANES - FOX_HEADS), (0, 0))),
            wt[off_g:],
        ], axis=0).astype(BF16)
        q_act, f_logit, inp, g_act, proj_b, fb, gates = _inproj(
            xf, mix_pre_g[l], w_cat, b_gate[l].reshape(1, 2 * d),
            (kw, kw, kw, kw, 3 * fox_w, LANES, 2 * d),
            (BF16, F32, BF16, BF16, BF16, F32, BF16),
            ("silu", None, None, "silu", None, None, "sigmoid_bias"))

        bias = jnp.pad(fox_f_bias[l], (0, LANES - FOX_HEADS)).reshape(1, LANES)
        c = _fcum(fb, bias, batch)

        o_a = _hgrn2(q_act, f_logit, inp, hg_lb_logits, l, batch)
        o_b = _fox(proj_b, c, batch, seq)

        kv = _norm_matmul(memf, mem_kv_g[l], w_mkv[l].astype(BF16), BF16, MEM_LEN, 1024)
        xf = _tail(xf, o_a, g_act, o_b, gates, kv, hg_norm_g[l],
                   w_branch_a[l].astype(BF16), w_branch_b[l].astype(BF16), w_out[l].astype(BF16),
                   mix_post_g[l], mem_pre_g[l],
                   (w_mq[l] * (1.0 / math.sqrt(MEM_DH))).astype(BF16), w_mo[l].astype(BF16),
                   mem_post_g[l], ffn2_pre_g[l], ffn2_w_in[l].astype(BF16),
                   ffn2_w_down[l].astype(BF16), ffn2_post_g[l], batch)
    return xf.reshape(batch, seq, d)
```

```python
import functools
import math

import jax
import jax.numpy as jnp
import numpy as np
from jax import lax
from jax.experimental import pallas as pl
from jax.experimental.pallas import tpu as pltpu

F32 = jnp.float32
BF16 = jnp.bfloat16

D_MODEL = 1024
HG_HEADS = 8
HG_DK = 128
FOX_HEADS = 8
FOX_DH = 128
MEM_LEN = 256
MEM_HEADS = 4
MEM_DH = D_MODEL // MEM_HEADS
D_FF = 2816
EPS = 1e-6
LANES = 128

VMEM_LIMIT = 56 * 1024 * 1024

NT_DIMS = (((1,), (1,)), ((), ()))
TN_DIMS = (((0,), (0,)), ((), ()))


def _params(sem):
    return pltpu.CompilerParams(dimension_semantics=sem, vmem_limit_bytes=VMEM_LIMIT)


def _rms(x, g):
    ms = jnp.mean(x * x, axis=-1, keepdims=True)
    return x * lax.rsqrt(ms + EPS) * g


def _sigmoid(x):
    return 0.5 * jnp.tanh(0.5 * x) + 0.5


def _dot(a, b):
    return jnp.dot(a, b, preferred_element_type=F32)


def _norm_matmul_kernel(x_ref, g_ref, w_ref, o_ref, h_ref):
    @pl.when(pl.program_id(1) == 0)
    def _():
        h_ref[...] = _rms(x_ref[...], g_ref[...]).astype(BF16)

    o_ref[...] = _dot(h_ref[...], w_ref[...]).astype(o_ref.dtype)


def _norm_matmul(x, g, w, out_dtype, tm, tn):
    m, d = x.shape
    n = w.shape[1]
    return pl.pallas_call(
        _norm_matmul_kernel,
        grid=(m // tm, n // tn),
        in_specs=[
            pl.BlockSpec((tm, d), lambda i, j: (i, 0)),
            pl.BlockSpec((1, d), lambda i, j: (0, 0)),
            pl.BlockSpec((d, tn), lambda i, j: (0, j)),
        ],
        out_specs=pl.BlockSpec((tm, tn), lambda i, j: (i, j)),
        out_shape=jax.ShapeDtypeStruct((m, n), out_dtype),
        scratch_shapes=[pltpu.VMEM((tm, d), BF16)],
        compiler_params=_params(("parallel", "arbitrary")),
        name="norm_matmul",
    )(x, g.reshape(1, d), w)


def _resident(shape):
    return pl.BlockSpec(shape, lambda *_: (0,) * len(shape), pipeline_mode=pl.Buffered(1))


def _inproj_kernel(x_ref, g_ref, wt_ref, bias_ref, *out_refs, tn, acts):
    h = _rms(x_ref[...], g_ref[...]).astype(BF16)
    off = 0
    for o_ref, act in zip(out_refs, acts):
        n = o_ref.shape[1]
        for j in range(0, n, tn):
            width = min(tn, n - j)
            y = lax.dot_general(h, wt_ref[off + j:off + j + width, :], NT_DIMS,
                                preferred_element_type=F32)
            if act == "silu":
                y = y * _sigmoid(y)
            elif act == "sigmoid_bias":
                y = _sigmoid(y + bias_ref[:, j:j + width])
            o_ref[:, j:j + width] = y.astype(o_ref.dtype)
        off += n


def _inproj(x, g, w, bias, widths, dtypes, acts, tm=512, tn=512):
    m, d = x.shape
    return pl.pallas_call(
        functools.partial(_inproj_kernel, tn=tn, acts=acts),
        grid=(m // tm,),
        in_specs=[pl.BlockSpec((tm, d), lambda i: (i, 0)), _resident((1, d)), _resident(w.shape),
                  _resident(bias.shape)],
        out_specs=[pl.BlockSpec((tm, n), lambda i: (i, 0)) for n in widths],
        out_shape=[jax.ShapeDtypeStruct((m, n), dt) for n, dt in zip(widths, dtypes)],
        compiler_params=_params(("parallel",)),
        name="input_projection",
    )(x, g.reshape(1, d), w, bias)


FFN_CHUNK = 256


def _ffn_body(x, pre_ref, w_in_ref, w_down_ref, post_ref, acc_ref):
    h = _rms(x, pre_ref[...]).astype(BF16)
    for f in range(D_FF // FFN_CHUNK):
        cols = slice(f * FFN_CHUNK, (f + 1) * FFN_CHUNK)
        gate = _dot(h, w_in_ref[:, cols])
        up = _dot(h, w_in_ref[:, D_FF + f * FFN_CHUNK:D_FF + (f + 1) * FFN_CHUNK])
        act = (gate * _sigmoid(gate) * up).astype(BF16)
        part = _dot(act, w_down_ref[cols, :])
        if f == 0:
            acc_ref[...] = part
        else:
            acc_ref[...] += part
    return x + 0.5 * _rms(acc_ref[...], post_ref[...])


def _ffn_kernel(x_ref, pre_ref, w_in_ref, w_down_ref, post_ref, o_ref, acc_ref):
    o_ref[...] = _ffn_body(x_ref[...], pre_ref, w_in_ref, w_down_ref, post_ref, acc_ref)


def _ffn(x, pre_g, w_in, w_down, post_g, tm=512):
    m, d = x.shape
    return pl.pallas_call(
        _ffn_kernel,
        grid=(m // tm,),
        in_specs=[
            pl.BlockSpec((tm, d), lambda i: (i, 0)),
            _resident((1, d)),
            _resident(w_in.shape),
            _resident(w_down.shape),
            _resident((1, d)),
        ],
        out_specs=pl.BlockSpec((tm, d), lambda i: (i, 0)),
        out_shape=jax.ShapeDtypeStruct((m, d), F32),
        scratch_shapes=[pltpu.VMEM((tm, d), F32)],
        compiler_params=_params(("parallel",)),
        name="ffn",
    )(x, pre_g.reshape(1, d), w_in, w_down, post_g.reshape(1, d))


def _split3(x):
    hi = x.astype(BF16)
    r = x - hi.astype(F32)
    mid = r.astype(BF16)
    lo = (r - mid.astype(F32)).astype(BF16)
    return hi, mid, lo


def _fcum_kernel(fb_ref, bias_ref, c_ref, carry_ref):
    @pl.when(pl.program_id(1) == 0)
    def _():
        carry_ref[...] = jnp.zeros_like(carry_ref)

    z = fb_ref[...] + bias_ref[...]
    ls = jnp.minimum(z, 0.0) - jnp.log(1.0 + jnp.exp(-jnp.abs(z)))
    tb = z.shape[0]
    row = lax.broadcasted_iota(jnp.int32, (tb, tb), 0)
    col = lax.broadcasted_iota(jnp.int32, (tb, tb), 1)
    tril = (col <= row).astype(BF16)
    hi, mid, lo = _split3(ls)
    c = _dot(tril, hi) + _dot(tril, mid) + _dot(tril, lo) + carry_ref[...]
    c_ref[...] = c
    carry_ref[...] = c[tb - 1:tb, :]


def _fcum(fb, bias, batch, tb=512):
    m = fb.shape[0]
    ns = m // batch // tb
    return pl.pallas_call(
        _fcum_kernel,
        grid=(batch, ns),
        in_specs=[
            pl.BlockSpec((tb, LANES), lambda b, s: (b * ns + s, 0)),
            pl.BlockSpec((1, LANES), lambda b, s: (0, 0)),
        ],
        out_specs=pl.BlockSpec((tb, LANES), lambda b, s: (b * ns + s, 0)),
        out_shape=jax.ShapeDtypeStruct((m, LANES), F32),
        scratch_shapes=[pltpu.VMEM((1, LANES), F32)],
        compiler_params=_params(("parallel", "arbitrary")),
        name="fox_decay_cumsum",
    )(fb, bias)


def _hgrn_level_table(c):
    t = np.arange(c)[:, None]
    s = np.arange(c)[None, :]
    level = np.full((c, c), -2, np.int32)
    level[t == s] = -1
    for l in range(int(math.log2(c))):
        h = c >> (l + 1)
        mid = (t // (2 * h)) * (2 * h) + h
        sel = (t // (2 * h) == s // (2 * h)) & (t >= mid) & (s < mid)
        level[sel & (level == -2)] = l
    return level


SUBLANES = 8


def _prefix_sum_rows(x, row):
    k = 1
    while k < x.shape[0]:
        x = x + jnp.where(row >= k, pltpu.roll(x, k, axis=0), 0.0)
        k *= 2
    return x


def _hgrn_sign_table(c):
    t = np.arange(c)[:, None]
    halves = [c >> (l + 1) for l in range(int(math.log2(c)))]
    sign = [np.where((t % (2 * h)) >= h, 1.0, -1.0) * np.ones((1, HG_DK)) for h in halves]
    return np.stack(sign).astype(np.float32)


def _midpoint_rows(b, h):
    c, w = b.shape
    if h >= SUBLANES:
        pieces = [jnp.broadcast_to(b[j * 2 * h + h - 1:j * 2 * h + h, :], (2 * h, w))
                  for j in range(c // (2 * h))]
        return pieces[0] if len(pieces) == 1 else jnp.concatenate(pieces, axis=0)
    groups = c // SUBLANES
    b3 = b.reshape(groups, SUBLANES, w)
    sub = lax.broadcasted_iota(jnp.int32, (1, SUBLANES, w), 1)
    r3 = None
    for j in range(SUBLANES // (2 * h)):
        src = j * 2 * h + h - 1
        piece = jnp.broadcast_to(b3[:, src:src + 1, :], (groups, SUBLANES, w))
        r3 = piece if r3 is None else jnp.where(sub >= j * 2 * h, piece, r3)
    return r3.reshape(c, w)


def _hgrn_kernel(q_ref, f_ref, i_ref, lb_ref, lvl_ref, sign_ref, o_ref, st_ref, *, chunk, layer):
    @pl.when(pl.program_id(2) == 0)
    def _():
        st_ref[...] = jnp.zeros_like(st_ref)

    c = chunk
    n_lvl = int(math.log2(c))
    ts = q_ref.shape[0]
    logits = lb_ref[...]
    e = jnp.exp(logits - jnp.max(logits, axis=0, keepdims=True))
    lb = jnp.sum(e[0:layer + 1, :], axis=0, keepdims=True) / jnp.sum(e, axis=0, keepdims=True)
    lvl = lvl_ref[...]
    st = st_ref[...]

    row = lax.broadcasted_iota(jnp.int32, (c, HG_DK), 0)
    ks, bs = [], []
    for ci in range(ts // c):
        f = lb + (1.0 - lb) * _sigmoid(f_ref[pl.ds(ci * c, c), :])
        bs.append(_prefix_sum_rows(jnp.log2(f), row))
        ks.append(1.0 - f)

    chunks = range(ts // c)
    q16s = [q_ref[pl.ds(ci * c, c), :] for ci in chunks]
    vs = [i_ref[pl.ds(ci * c, c), :] for ci in chunks]
    k16s = [k.astype(BF16) for k in ks]

    a = [jnp.where(lvl == -1, lax.dot_general(q16s[ci], k16s[ci], NT_DIMS,
                                              preferred_element_type=F32), 0.0) for ci in chunks]
    for l in range(n_lvl):
        h = c >> (l + 1)
        for ci in chunks:
            d_l = (bs[ci] - _midpoint_rows(bs[ci], h)) * sign_ref[l]
            e_l = jnp.exp2(d_l).astype(BF16)
            p = lax.dot_general(q16s[ci] * e_l, k16s[ci] * e_l, NT_DIMS,
                                preferred_element_type=F32)
            a[ci] = jnp.where(lvl == l, p, a[ci])

    o_intra = [_dot(a[ci].astype(BF16), vs[ci]) for ci in chunks]
    upds = [lax.dot_general(vs[ci], (ks[ci] * jnp.exp2(bs[ci][c - 1:c, :] - bs[ci])).astype(BF16),
                            TN_DIMS, preferred_element_type=F32) for ci in chunks]

    for ci in chunks:
        o = o_intra[ci] + lax.dot_general(q16s[ci] * jnp.exp2(bs[ci]).astype(BF16),
                                          st.astype(BF16), NT_DIMS, preferred_element_type=F32)
        o_ref[pl.ds(ci * c, c), :] = o.astype(o_ref.dtype)
        st = st * jnp.exp2(bs[ci][c - 1:c, :]) + upds[ci]

    st_ref[...] = st


def _hgrn2(q_act, f_logit, inp, lb_logits, layer, batch, ts=2048, chunk=128):
    m = q_act.shape[0]
    ns = m // batch // ts
    level = _hgrn_level_table(chunk)
    sign = _hgrn_sign_table(chunk)
    nl = lb_logits.shape[0]
    blk = pl.BlockSpec((ts, HG_DK), lambda b, h, s: (b * ns + s, h))
    return pl.pallas_call(
        functools.partial(_hgrn_kernel, chunk=chunk, layer=layer),
        grid=(batch, HG_HEADS, ns),
        in_specs=[
            blk, blk, blk,
            pl.BlockSpec((nl, HG_DK), lambda b, h, s: (0, h)),
            _resident(level.shape), _resident(sign.shape),
        ],
        out_specs=blk,
        out_shape=jax.ShapeDtypeStruct((m, HG_HEADS * HG_DK), BF16),
        scratch_shapes=[pltpu.VMEM((HG_DK, HG_DK), F32)],
        compiler_params=_params(("parallel", "parallel", "arbitrary")),
        name="hgrn2",
    )(q_act, f_logit, inp, lb_logits.reshape(nl, HG_HEADS * HG_DK), jnp.asarray(level),
      jnp.asarray(sign))


LOG2E = 1.4426950408889634
FOX_FOLD = 8
FOX_SUM_ROWS = 16


def _head_column(c_blk, h):
    lane = lax.broadcasted_iota(jnp.int32, c_blk.shape, 1)
    return jnp.sum(jnp.where(lane == h, c_blk, 0.0), axis=1, keepdims=True)


def _decay_features(c_col):
    rows = c_col.shape[0]
    lane = lax.broadcasted_iota(jnp.int32, (rows, LANES), 1)
    hi, mid, lo = (part.astype(F32) for part in _split3(jnp.broadcast_to(c_col, (rows, LANES))))

    def place(base, sign):
        ones = (lane >= 3 - base) & (lane < 6 - base)
        feat = jnp.where(lane == base, sign * hi,
                         jnp.where(lane == base + 1, sign * mid,
                                   jnp.where(lane == base + 2, sign * lo,
                                             jnp.where(ones, 1.0, 0.0))))
        return feat.astype(BF16)

    return place(3, 1.0), place(0, -1.0)


def _fox_kernel(q_ref, k_ref, v_ref, c_ref, o_ref, kx_ref, qf_ref, vt_ref, qxt_ref,
                s0_ref, s1_ref, x0_ref, x1_ref, m_ref, acc_ref, *, tk):
    h = pl.program_id(1)
    qi = pl.program_id(2)
    tq = q_ref.shape[0]
    nk = kx_ref.shape[0]

    @pl.when(qi == 0)
    def _():
        def setup(kb, carry):
            rows = pl.ds(pl.multiple_of(kb * tk, tk), tk)
            q_feat, k_feat = _decay_features(_head_column(c_ref[rows, :], h) * LOG2E)
            kx_ref[kb, :, 0:FOX_DH] = k_ref[rows, :]
            kx_ref[kb, :, FOX_DH:2 * FOX_DH] = k_feat
            qf_ref[kb] = q_feat
            vt_ref[kb, 0:FOX_DH, :] = v_ref[rows, :].T
            ones_row = lax.broadcasted_iota(jnp.int32, (FOX_SUM_ROWS, tk), 0) == 0
            vt_ref[kb, FOX_DH:FOX_DH + FOX_SUM_ROWS, :] = ones_row.astype(F32).astype(BF16)
            return carry

        lax.fori_loop(0, nk, setup, 0)

    ratio = tq // tk
    n_full = ratio * qi
    q_feat = [qf_ref[n_full + j] for j in range(ratio)]
    q_feat = q_feat[0] if ratio == 1 else jnp.concatenate(q_feat, axis=0)
    qxt_ref[...] = jnp.concatenate([q_ref[...], q_feat], axis=1).T
    m_ref[...] = jnp.full_like(m_ref, -jnp.inf)
    acc_ref[...] = jnp.zeros_like(acc_ref)

    def fold(x, op):
        part = op(x.reshape(FOX_FOLD, tk // FOX_FOLD, x.shape[1]), axis=0)
        return op(part, axis=0, keepdims=True)

    def scores(kb, s_ref, smax_ref):
        st = _dot(kx_ref[kb], qxt_ref[...])
        s_ref[...] = st
        smax_ref[...] = fold(st, jnp.max)

    def causal(st):
        key = lax.broadcasted_iota(jnp.int32, st.shape, 0)
        qry = lax.broadcasted_iota(jnp.int32, st.shape, 1)
        return jnp.where(key <= qry, st, -jnp.inf)

    def softmax_step(kb, st, blk_max, cols):
        m_prev = m_ref[:, cols]
        m_new = jnp.maximum(m_prev, blk_max)
        alpha = jnp.exp2(m_prev - m_new)
        p = jnp.exp2(st - m_new)
        m_ref[:, cols] = m_new
        acc_ref[:, cols] = alpha * acc_ref[:, cols] + _dot(vt_ref[kb], p.astype(BF16))

    def update(kb, s_ref, smax_ref):
        softmax_step(kb, s_ref[...], smax_ref[...], slice(None))

    def update_diagonal(kb, s_ref):
        st = s_ref[...]
        head = causal(st[:, 0:tk])
        st = head if tq == tk else jnp.concatenate([head, st[:, tk:]], axis=1)
        softmax_step(kb, st, fold(st, jnp.max), slice(None))

    scores(0, s0_ref, x0_ref)

    def pair(kb):
        scores(kb + 1, s1_ref, x1_ref)
        update(kb, s0_ref, x0_ref)
        scores(kb + 2, s0_ref, x0_ref)
        update(kb + 1, s1_ref, x1_ref)

    def body4(j, carry):
        pair(4 * j)
        pair(4 * j + 2)
        return carry

    def body2(j, carry):
        pair(2 * j)
        return carry

    n_pairs = n_full // 2
    lax.fori_loop(0, n_pairs // 2, body4, 0)
    lax.fori_loop(2 * (n_pairs // 2), n_pairs, body2, 0)

    if ratio == 2:
        upper = slice(tk, tq)
        tile = causal(_dot(kx_ref[n_full + 1], qxt_ref[:, upper]))
        update_diagonal(n_full, s0_ref)
        softmax_step(n_full + 1, tile, fold(tile, jnp.max), upper)
    else:
        @pl.when(qi % 2 == 1)
        def _():
            scores(qi, s1_ref, x1_ref)
            update(qi - 1, s0_ref, x0_ref)
            update_diagonal(qi, s1_ref)

        @pl.when(qi % 2 == 0)
        def _():
            update_diagonal(qi, s0_ref)

    out = acc_ref[0:FOX_DH, :] / acc_ref[FOX_DH:FOX_DH + 1, :]
    o_ref[...] = out.astype(o_ref.dtype).T


def _fox(proj_b, c, batch, seq, tq=1024, tk=512):
    assert tq in (tk, 2 * tk)
    m = proj_b.shape[0]
    nq = seq // tq
    nk = seq // tk
    return pl.pallas_call(
        functools.partial(_fox_kernel, tk=tk),
        grid=(batch, FOX_HEADS, nq),
        in_specs=[
            pl.BlockSpec((tq, FOX_DH), lambda b, h, i: (b * nq + i, h)),
            pl.BlockSpec((seq, FOX_DH), lambda b, h, i: (b, FOX_HEADS + h)),
            pl.BlockSpec((seq, FOX_DH), lambda b, h, i: (b, 2 * FOX_HEADS + h)),
            pl.BlockSpec((seq, LANES), lambda b, h, i: (b, 0)),
        ],
        out_specs=pl.BlockSpec((tq, FOX_DH), lambda b, h, i: (b * nq + i, h)),
        out_shape=jax.ShapeDtypeStruct((m, FOX_HEADS * FOX_DH), BF16),
        scratch_shapes=[
            pltpu.VMEM((nk, tk, 2 * FOX_DH), BF16),
            pltpu.VMEM((nk, tk, FOX_DH), BF16),
            pltpu.VMEM((nk, FOX_DH + FOX_SUM_ROWS, tk), BF16),
            pltpu.VMEM((2 * FOX_DH, tq), BF16),
            pltpu.VMEM((tk, tq), F32),
            pltpu.VMEM((tk, tq), F32),
            pltpu.VMEM((1, tq), F32),
            pltpu.VMEM((1, tq), F32),
            pltpu.VMEM((1, tq), F32),
            pltpu.VMEM((FOX_DH + FOX_SUM_ROWS, tq), F32),
        ],
        compiler_params=_params(("parallel", "parallel", "arbitrary")),
        name="fox_attention",
    )(proj_b, proj_b, proj_b, c)


def _merge_body(x, oa_ref, ga_ref, ob_ref, g0_ref, g1_ref, hg_ref, wa_ref, wb_ref, wo_ref, post_ref):
    oa = _rms(oa_ref[...].astype(F32), hg_ref[...]) * ga_ref[...].astype(F32)
    ya = _dot(oa.astype(BF16), wa_ref[...])
    yb = _dot(ob_ref[...], wb_ref[...])
    y = g0_ref[...].astype(F32) * ya + g1_ref[...].astype(F32) * yb
    z = _dot(y.astype(BF16), wo_ref[...])
    return x + _rms(z, post_ref[...])


def _memattn_body(x, pre_ref, kv_ref, wq_ref, wo_ref, post_ref):
    h = _rms(x, pre_ref[...]).astype(BF16)
    q = _dot(h, wq_ref[...]).astype(BF16)
    heads = []
    for hd in range(MEM_HEADS):
        lo = hd * MEM_DH
        kh = kv_ref[:, lo:lo + MEM_DH]
        vh = kv_ref[:, D_MODEL + lo:D_MODEL + lo + MEM_DH]
        s = lax.dot_general(q[:, lo:lo + MEM_DH], kh, NT_DIMS, preferred_element_type=F32)
        p = jnp.exp(s - jnp.max(s, axis=1, keepdims=True))
        inv = 1.0 / jnp.sum(p, axis=1, keepdims=True)
        heads.append((_dot(p.astype(BF16), vh) * inv).astype(BF16))
    o = jnp.concatenate(heads, axis=1)
    return x + _rms(_dot(o, wo_ref[...]), post_ref[...])


def _tail_kernel(x_ref, oa_ref, ga_ref, ob_ref, g0_ref, g1_ref, kv_ref, hg_ref, wa_ref, wb_ref,
                 wo_ref, mix_post_ref, mem_pre_ref, wq_ref, wmo_ref, mem_post_ref,
                 ffn_pre_ref, w_in_ref, w_down_ref, ffn_post_ref, o_ref, acc_ref):
    x = _merge_body(x_ref[...], oa_ref, ga_ref, ob_ref, g0_ref, g1_ref, hg_ref, wa_ref, wb_ref,
                    wo_ref, mix_post_ref)
    x = _memattn_body(x, mem_pre_ref, kv_ref, wq_ref, wmo_ref, mem_post_ref)
    o_ref[...] = _ffn_body(x, ffn_pre_ref, w_in_ref, w_down_ref, ffn_post_ref, acc_ref)


def _tail(x, o_a, g_act, o_b, gates, kv, hg_norm_g, w_a, w_b, w_o, mix_post_g, mem_pre_g, w_q,
          w_mo, mem_post_g, ffn_pre_g, w_in, w_down, ffn_post_g, batch, tm=512):
    m, d = x.shape
    ns = m // batch // tm
    row = lambda c: pl.BlockSpec((tm, d), lambda b, s: (b * ns + s, c))
    vec, mat = _resident((1, d)), _resident((d, d))
    r1 = lambda g: g.reshape(1, d)
    return pl.pallas_call(
        _tail_kernel,
        grid=(batch, ns),
        in_specs=[row(0), row(0), row(0), row(0), row(0), row(1),
                  pl.BlockSpec((MEM_LEN, 2 * d), lambda b, s: (b, 0)),
                  vec, mat, mat, mat, vec, vec, mat, mat, vec,
                  vec, _resident(w_in.shape), _resident(w_down.shape), vec],
        out_specs=row(0),
        out_shape=jax.ShapeDtypeStruct((m, d), F32),
        scratch_shapes=[pltpu.VMEM((tm, d), F32)],
        compiler_params=_params(("parallel", "parallel")),
        name="merge_memattn_ffn",
    )(x, o_a, g_act, o_b, gates, gates, kv, r1(hg_norm_g), w_a, w_b, w_o, r1(mix_post_g),
      r1(mem_pre_g), w_q, w_mo, r1(mem_post_g), r1(ffn_pre_g), w_in, w_down, r1(ffn_post_g))


def kernel(x, mem, ffn1_pre_g, ffn1_w_in, ffn1_w_down, ffn1_post_g, mix_pre_g, w_in, hg_lb_logits, hg_norm_g, fox_f_bias, w_branch_a, w_branch_b, b_gate, w_out, mix_post_g, mem_pre_g, mem_kv_g, w_mq, w_mkv, w_mo, mem_post_g, ffn2_pre_g, ffn2_w_in, ffn2_w_down, ffn2_post_g):
    batch, seq, d = x.shape
    depth = ffn1_w_in.shape[0]
    xf = x.reshape(batch * seq, d)
    memf = mem.reshape(batch * MEM_LEN, d)
    kw = HG_HEADS * HG_DK
    off_b = 4 * kw
    off_f = off_b + 3 * FOX_HEADS * FOX_DH
    off_g = off_f + FOX_HEADS
    for l in range(depth):
        xf = _ffn(xf, ffn1_pre_g[l], ffn1_w_in[l].astype(BF16), ffn1_w_down[l].astype(BF16),
                  ffn1_post_g[l])

        wt = jnp.swapaxes(w_in[l], 0, 1)
        fox_w = FOX_HEADS * FOX_DH
        w_cat = jnp.concatenate([
            wt[:off_b],
            wt[off_b:off_b + fox_w] * (LOG2E / math.sqrt(FOX_DH)),
            wt[off_b + fox_w:off_f],
            jnp.pad(wt[off_f:off_g], ((0, LANES - FOX_HEADS), (0, 0))),
            wt[off_g:],
        ], axis=0).astype(BF16)
        q_act, f_logit, inp, g_act, proj_b, fb, gates = _inproj(
            xf, mix_pre_g[l], w_cat, b_gate[l].reshape(1, 2 * d),
            (kw, kw, kw, kw, 3 * fox_w, LANES, 2 * d),
            (BF16, F32, BF16, BF16, BF16, F32, BF16),
            ("silu", None, None, "silu", None, None, "sigmoid_bias"))

        bias = jnp.pad(fox_f_bias[l], (0, LANES - FOX_HEADS)).reshape(1, LANES)
        c = _fcum(fb, bias, batch)

        o_a = _hgrn2(q_act, f_logit, inp, hg_lb_logits, l, batch)
        o_b = _fox(proj_b, c, batch, seq)

        kv = _norm_matmul(memf, mem_kv_g[l], w_mkv[l].astype(BF16), BF16, MEM_LEN, 1024)
        xf = _tail(xf, o_a, g_act, o_b, gates, kv, hg_norm_g[l],
                   w_branch_a[l].astype(BF16), w_branch_b[l].astype(BF16), w_out[l].astype(BF16),
                   mix_post_g[l], mem_pre_g[l],
                   (w_mq[l] * (1.0 / math.sqrt(MEM_DH))).astype(BF16), w_mo[l].astype(BF16),
                   mem_post_g[l], ffn2_pre_g[l], ffn2_w_in[l].astype(BF16),
                   ffn2_w_down[l].astype(BF16), ffn2_post_g[l], batch)
    return xf.reshape(batch, seq, d)
```

```python
import functools
import math

import jax
import jax.numpy as jnp
import numpy as np
from jax import lax
from jax.experimental import pallas as pl
from jax.experimental.pallas import tpu as pltpu

F32 = jnp.float32
BF16 = jnp.bfloat16

D_MODEL = 1024
HG_HEADS = 8
HG_DK = 128
FOX_HEADS = 8
FOX_DH = 128
MEM_LEN = 256
MEM_HEADS = 4
MEM_DH = D_MODEL // MEM_HEADS
D_FF = 2816
EPS = 1e-6
LANES = 128

VMEM_LIMIT = 56 * 1024 * 1024

NT_DIMS = (((1,), (1,)), ((), ()))
TN_DIMS = (((0,), (0,)), ((), ()))


def _params(sem):
    return pltpu.CompilerParams(dimension_semantics=sem, vmem_limit_bytes=VMEM_LIMIT)


def _rms(x, g):
    ms = jnp.mean(x * x, axis=-1, keepdims=True)
    return x * lax.rsqrt(ms + EPS) * g


def _sigmoid(x):
    return 0.5 * jnp.tanh(0.5 * x) + 0.5


def _dot(a, b):
    return jnp.dot(a, b, preferred_element_type=F32)


def _norm_matmul_kernel(x_ref, g_ref, w_ref, o_ref, h_ref):
    @pl.when(pl.program_id(1) == 0)
    def _():
        h_ref[...] = _rms(x_ref[...], g_ref[...]).astype(BF16)

    o_ref[...] = _dot(h_ref[...], w_ref[...]).astype(o_ref.dtype)


def _norm_matmul(x, g, w, out_dtype, tm, tn):
    m, d = x.shape
    n = w.shape[1]
    return pl.pallas_call(
        _norm_matmul_kernel,
        grid=(m // tm, n // tn),
        in_specs=[
            pl.BlockSpec((tm, d), lambda i, j: (i, 0)),
            pl.BlockSpec((1, d), lambda i, j: (0, 0)),
            pl.BlockSpec((d, tn), lambda i, j: (0, j)),
        ],
        out_specs=pl.BlockSpec((tm, tn), lambda i, j: (i, j)),
        out_shape=jax.ShapeDtypeStruct((m, n), out_dtype),
        scratch_shapes=[pltpu.VMEM((tm, d), BF16)],
        compiler_params=_params(("parallel", "arbitrary")),
        name="norm_matmul",
    )(x, g.reshape(1, d), w)


def _resident(shape):
    return pl.BlockSpec(shape, lambda *_: (0,) * len(shape), pipeline_mode=pl.Buffered(1))


def _inproj_kernel(x_ref, g_ref, *refs, groups, tn):
    n_w = 1 + max(grp[0] for grp in groups)
    wt_refs, bias_ref, out_refs = refs[:n_w], refs[n_w], refs[n_w + 1:]
    h = _rms(x_ref[...], g_ref[...]).astype(BF16)
    for o_ref, (wi, row0, act, scaled, scale) in zip(out_refs, groups):
        n = o_ref.shape[1]
        for j in range(0, n, tn):
            width = min(tn, n - j)
            y = lax.dot_general(h, wt_refs[wi][row0 + j:row0 + j + width, :], NT_DIMS,
                                preferred_element_type=F32)
            if j < scaled:
                y = y * scale
            if act == "silu":
                y = y * _sigmoid(y)
            elif act == "sigmoid_bias":
                y = _sigmoid(y + bias_ref[:, j:j + width])
            o_ref[:, j:j + width] = y.astype(o_ref.dtype)


def _inproj(x, g, weights, bias, groups, widths, dtypes, tm=512, tn=512):
    m, d = x.shape
    assert all(scaled % tn == 0 for (_, _, _, scaled, _) in groups)
    return pl.pallas_call(
        functools.partial(_inproj_kernel, groups=groups, tn=tn),
        grid=(m // tm,),
        in_specs=[pl.BlockSpec((tm, d), lambda i: (i, 0)), _resident((1, d))]
        + [_resident(w.shape) for w in weights] + [_resident(bias.shape)],
        out_specs=[pl.BlockSpec((tm, n), lambda i: (i, 0)) for n in widths],
        out_shape=[jax.ShapeDtypeStruct((m, n), dt) for n, dt in zip(widths, dtypes)],
        compiler_params=_params(("parallel",)),
        name="input_projection",
    )(x, g.reshape(1, d), *weights, bias)


FFN_CHUNK = 256


def _ffn_body(x, pre_ref, w_in_ref, w_down_ref, post_ref, acc_ref):
    h = _rms(x, pre_ref[...]).astype(BF16)
    for f in range(D_FF // FFN_CHUNK):
        cols = slice(f * FFN_CHUNK, (f + 1) * FFN_CHUNK)
        gate = _dot(h, w_in_ref[:, cols])
        up = _dot(h, w_in_ref[:, D_FF + f * FFN_CHUNK:D_FF + (f + 1) * FFN_CHUNK])
        act = (gate * _sigmoid(gate) * up).astype(BF16)
        part = _dot(act, w_down_ref[cols, :])
        if f == 0:
            acc_ref[...] = part
        else:
            acc_ref[...] += part
    return x + 0.5 * _rms(acc_ref[...], post_ref[...])


def _ffn_kernel(x_ref, pre_ref, w_in_ref, w_down_ref, post_ref, o_ref, acc_ref):
    o_ref[...] = _ffn_body(x_ref[...], pre_ref, w_in_ref, w_down_ref, post_ref, acc_ref)


def _ffn(x, pre_g, w_in, w_down, post_g, tm=1024):
    m, d = x.shape
    return pl.pallas_call(
        _ffn_kernel,
        grid=(m // tm,),
        in_specs=[
            pl.BlockSpec((tm, d), lambda i: (i, 0)),
            _resident((1, d)),
            _resident(w_in.shape),
            _resident(w_down.shape),
            _resident((1, d)),
        ],
        out_specs=pl.BlockSpec((tm, d), lambda i: (i, 0)),
        out_shape=jax.ShapeDtypeStruct((m, d), F32),
        scratch_shapes=[pltpu.VMEM((tm, d), F32)],
        compiler_params=_params(("parallel",)),
        name="ffn",
    )(x, pre_g.reshape(1, d), w_in, w_down, post_g.reshape(1, d))


def _split3(x):
    hi = x.astype(BF16)
    r = x - hi.astype(F32)
    mid = r.astype(BF16)
    lo = (r - mid.astype(F32)).astype(BF16)
    return hi, mid, lo


def _fcum_kernel(fb_ref, bias_ref, c_ref, carry_ref):
    @pl.when(pl.program_id(1) == 0)
    def _():
        carry_ref[...] = jnp.zeros_like(carry_ref)

    z = fb_ref[...] + bias_ref[...]
    ls = jnp.minimum(z, 0.0) - jnp.log(1.0 + jnp.exp(-jnp.abs(z)))
    tb = z.shape[0]
    row = lax.broadcasted_iota(jnp.int32, (tb, tb), 0)
    col = lax.broadcasted_iota(jnp.int32, (tb, tb), 1)
    tril = (col <= row).astype(BF16)
    hi, mid, lo = _split3(ls)
    c = _dot(tril, hi) + _dot(tril, mid) + _dot(tril, lo) + carry_ref[...]
    c_ref[...] = c
    carry_ref[...] = c[tb - 1:tb, :]


def _fcum(fb, bias, batch, tb=512):
    m = fb.shape[0]
    ns = m // batch // tb
    return pl.pallas_call(
        _fcum_kernel,
        grid=(batch, ns),
        in_specs=[
            pl.BlockSpec((tb, LANES), lambda b, s: (b * ns + s, 0)),
            pl.BlockSpec((1, LANES), lambda b, s: (0, 0)),
        ],
        out_specs=pl.BlockSpec((tb, LANES), lambda b, s: (b * ns + s, 0)),
        out_shape=jax.ShapeDtypeStruct((m, LANES), F32),
        scratch_shapes=[pltpu.VMEM((1, LANES), F32)],
        compiler_params=_params(("parallel", "arbitrary")),
        name="fox_decay_cumsum",
    )(fb, bias)


def _hgrn_level_table(c):
    t = np.arange(c)[:, None]
    s = np.arange(c)[None, :]
    level = np.full((c, c), -2, np.int32)
    level[t == s] = -1
    for l in range(int(math.log2(c))):
        h = c >> (l + 1)
        mid = (t // (2 * h)) * (2 * h) + h
        sel = (t // (2 * h) == s // (2 * h)) & (t >= mid) & (s < mid)
        level[sel & (level == -2)] = l
    return level


SUBLANES = 8


def _prefix_sum_rows(x, row):
    k = 1
    while k < x.shape[0]:
        x = x + jnp.where(row >= k, pltpu.roll(x, k, axis=0), 0.0)
        k *= 2
    return x


def _hgrn_sign_table(c):
    t = np.arange(c)[:, None]
    halves = [c >> (l + 1) for l in range(int(math.log2(c)))]
    sign = [np.where((t % (2 * h)) >= h, 1.0, -1.0) * np.ones((1, HG_DK)) for h in halves]
    return np.stack(sign).astype(np.float32)


def _midpoint_rows(b, h):
    c, w = b.shape
    if h >= SUBLANES:
        pieces = [jnp.broadcast_to(b[j * 2 * h + h - 1:j * 2 * h + h, :], (2 * h, w))
                  for j in range(c // (2 * h))]
        return pieces[0] if len(pieces) == 1 else jnp.concatenate(pieces, axis=0)
    groups = c // SUBLANES
    b3 = b.reshape(groups, SUBLANES, w)
    sub = lax.broadcasted_iota(jnp.int32, (1, SUBLANES, w), 1)
    r3 = None
    for j in range(SUBLANES // (2 * h)):
        src = j * 2 * h + h - 1
        piece = jnp.broadcast_to(b3[:, src:src + 1, :], (groups, SUBLANES, w))
        r3 = piece if r3 is None else jnp.where(sub >= j * 2 * h, piece, r3)
    return r3.reshape(c, w)


def _hgrn_kernel(q_ref, f_ref, i_ref, lb_ref, lvl_ref, sign_ref, o_ref, st_ref, *, chunk, layer):
    @pl.when(pl.program_id(2) == 0)
    def _():
        st_ref[...] = jnp.zeros_like(st_ref)

    c = chunk
    n_lvl = int(math.log2(c))
    ts = q_ref.shape[0]
    logits = lb_ref[...]
    e = jnp.exp(logits - jnp.max(logits, axis=0, keepdims=True))
    lb = jnp.sum(e[0:layer + 1, :], axis=0, keepdims=True) / jnp.sum(e, axis=0, keepdims=True)
    lvl = lvl_ref[...]
    st = st_ref[...]

    row = lax.broadcasted_iota(jnp.int32, (c, HG_DK), 0)
    ks, bs = [], []
    for ci in range(ts // c):
        f = lb + (1.0 - lb) * _sigmoid(f_ref[pl.ds(ci * c, c), :])
        bs.append(_prefix_sum_rows(jnp.log2(f), row))
        ks.append(1.0 - f)

    chunks = range(ts // c)
    q16s = [q_ref[pl.ds(ci * c, c), :] for ci in chunks]
    vs = [i_ref[pl.ds(ci * c, c), :] for ci in chunks]
    k16s = [k.astype(BF16) for k in ks]

    a = [jnp.where(lvl == -1, lax.dot_general(q16s[ci], k16s[ci], NT_DIMS,
                                              preferred_element_type=F32), 0.0) for ci in chunks]
    for l in range(n_lvl):
        h = c >> (l + 1)
        for ci in chunks:
            d_l = (bs[ci] - _midpoint_rows(bs[ci], h)) * sign_ref[l]
            e_l = jnp.exp2(d_l).astype(BF16)
            p = lax.dot_general(q16s[ci] * e_l, k16s[ci] * e_l, NT_DIMS,
                                preferred_element_type=F32)
            a[ci] = jnp.where(lvl == l, p, a[ci])

    o_intra = [_dot(a[ci].astype(BF16), vs[ci]) for ci in chunks]
    upds = [lax.dot_general(vs[ci], (ks[ci] * jnp.exp2(bs[ci][c - 1:c, :] - bs[ci])).astype(BF16),
                            TN_DIMS, preferred_element_type=F32) for ci in chunks]

    for ci in chunks:
        o = o_intra[ci] + lax.dot_general(q16s[ci] * jnp.exp2(bs[ci]).astype(BF16),
                                          st.astype(BF16), NT_DIMS, preferred_element_type=F32)
        o_ref[pl.ds(ci * c, c), :] = o.astype(o_ref.dtype)
        st = st * jnp.exp2(bs[ci][c - 1:c, :]) + upds[ci]

    st_ref[...] = st


def _hgrn2(q_act, f_logit, inp, lb_logits, layer, batch, ts=2048, chunk=128):
    m = q_act.shape[0]
    ns = m // batch // ts
    level = _hgrn_level_table(chunk)
    sign = _hgrn_sign_table(chunk)
    nl = lb_logits.shape[0]
    blk = pl.BlockSpec((ts, HG_DK), lambda b, h, s: (b * ns + s, h))
    return pl.pallas_call(
        functools.partial(_hgrn_kernel, chunk=chunk, layer=layer),
        grid=(batch, HG_HEADS, ns),
        in_specs=[
            blk, blk, blk,
            pl.BlockSpec((nl, HG_DK), lambda b, h, s: (0, h)),
            _resident(level.shape), _resident(sign.shape),
        ],
        out_specs=blk,
        out_shape=jax.ShapeDtypeStruct((m, HG_HEADS * HG_DK), BF16),
        scratch_shapes=[pltpu.VMEM((HG_DK, HG_DK), F32)],
        compiler_params=_params(("parallel", "parallel", "arbitrary")),
        name="hgrn2",
    )(q_act, f_logit, inp, lb_logits.reshape(nl, HG_HEADS * HG_DK), jnp.asarray(level),
      jnp.asarray(sign))


LOG2E = 1.4426950408889634
FOX_FOLD = 8
FOX_SUM_ROWS = 16


def _decay_features_t(c_blk, h):
    rows = c_blk.shape[0]
    heads = c_blk.T[0:SUBLANES, :]
    sub = lax.broadcasted_iota(jnp.int32, (SUBLANES, rows), 0)
    c_row = jnp.sum(jnp.where(sub == h, heads, 0.0), axis=0, keepdims=True)
    hi, mid, lo = (jnp.broadcast_to(part.astype(F32), (SUBLANES, rows))
                   for part in _split3(c_row * LOG2E))

    def place(base, sign):
        ones = (sub >= 3 - base) & (sub < 6 - base)
        feat = jnp.where(sub == base, sign * hi,
                         jnp.where(sub == base + 1, sign * mid,
                                   jnp.where(sub == base + 2, sign * lo,
                                             jnp.where(ones, 1.0, 0.0))))
        pad = jnp.zeros((LANES - SUBLANES, rows), F32)
        return jnp.concatenate([feat, pad], axis=0).astype(BF16)

    return place(3, 1.0), place(0, -1.0)


def _fox_kernel(q_ref, k_ref, v_ref, c_ref, o_ref, kx_ref, qft_ref, vt_ref, qxt_ref,
                s0_ref, s1_ref, x0_ref, x1_ref, m_ref, acc_ref, *, tk):
    h = pl.program_id(1)
    qi = pl.program_id(2)
    tq = q_ref.shape[0]
    nk = kx_ref.shape[0]

    @pl.when(qi == 0)
    def _():
        def setup(kb, carry):
            rows = pl.ds(pl.multiple_of(kb * tk, tk), tk)
            q_feat_t, k_feat_t = _decay_features_t(c_ref[rows, :], h)
            kx_ref[kb, :, 0:FOX_DH] = k_ref[rows, :]
            kx_ref[kb, :, FOX_DH:2 * FOX_DH] = k_feat_t.T
            qft_ref[kb] = q_feat_t
            vt_ref[kb, 0:FOX_DH, :] = v_ref[rows, :].T
            ones_row = lax.broadcasted_iota(jnp.int32, (FOX_SUM_ROWS, tk), 0) == 0
            vt_ref[kb, FOX_DH:FOX_DH + FOX_SUM_ROWS, :] = ones_row.astype(F32).astype(BF16)
            return carry

        lax.fori_loop(0, nk, setup, 0)

    ratio = tq // tk
    n_full = ratio * qi
    qxt_ref[0:FOX_DH, :] = q_ref[...].T
    for j in range(ratio):
        qxt_ref[FOX_DH:2 * FOX_DH, j * tk:(j + 1) * tk] = qft_ref[n_full + j]
    m_ref[...] = jnp.full_like(m_ref, -jnp.inf)
    acc_ref[...] = jnp.zeros_like(acc_ref)

    def fold(x, op):
        part = op(x.reshape(FOX_FOLD, tk // FOX_FOLD, x.shape[1]), axis=0)
        return op(part, axis=0, keepdims=True)

    def scores(kb, s_ref, smax_ref):
        st = _dot(kx_ref[kb], qxt_ref[...])
        s_ref[...] = st
        smax_ref[...] = fold(st, jnp.max)

    def causal(st):
        key = lax.broadcasted_iota(jnp.int32, st.shape, 0)
        qry = lax.broadcasted_iota(jnp.int32, st.shape, 1)
        return jnp.where(key <= qry, st, -jnp.inf)

    def softmax_step(kb, st, blk_max, cols):
        m_prev = m_ref[:, cols]
        m_new = jnp.maximum(m_prev, blk_max)
        alpha = jnp.exp2(m_prev - m_new)
        p = jnp.exp2(st - m_new)
        m_ref[:, cols] = m_new
        acc_ref[:, cols] = alpha * acc_ref[:, cols] + _dot(vt_ref[kb], p.astype(BF16))

    def update(kb, s_ref, smax_ref):
        softmax_step(kb, s_ref[...], smax_ref[...], slice(None))

    def update_diagonal(kb, s_ref):
        st = s_ref[...]
        head = causal(st[:, 0:tk])
        st = head if tq == tk else jnp.concatenate([head, st[:, tk:]], axis=1)
        softmax_step(kb, st, fold(st, jnp.max), slice(None))

    scores(0, s0_ref, x0_ref)

    def pair(kb):
        scores(kb + 1, s1_ref, x1_ref)
        update(kb, s0_ref, x0_ref)
        scores(kb + 2, s0_ref, x0_ref)
        update(kb + 1, s1_ref, x1_ref)

    def body4(j, carry):
        pair(4 * j)
        pair(4 * j + 2)
        return carry

    def body2(j, carry):
        pair(2 * j)
        return carry

    n_pairs = n_full // 2
    lax.fori_loop(0, n_pairs // 2, body4, 0)
    lax.fori_loop(2 * (n_pairs // 2), n_pairs, body2, 0)

    if ratio == 2:
        upper = slice(tk, tq)
        tile = causal(_dot(kx_ref[n_full + 1], qxt_ref[:, upper]))
        update_diagonal(n_full, s0_ref)
        softmax_step(n_full + 1, tile, fold(tile, jnp.max), upper)
    else:
        @pl.when(qi % 2 == 1)
        def _():
            scores(qi, s1_ref, x1_ref)
            update(qi - 1, s0_ref, x0_ref)
            update_diagonal(qi, s1_ref)

        @pl.when(qi % 2 == 0)
        def _():
            update_diagonal(qi, s0_ref)

    out = acc_ref[0:FOX_DH, :] / acc_ref[FOX_DH:FOX_DH + 1, :]
    o_ref[...] = out.astype(o_ref.dtype).T


def _fox(proj_b, c, batch, seq, tq=1024, tk=512):
    assert tq in (tk, 2 * tk)
    m = proj_b.shape[0]
    nq = seq // tq
    nk = seq // tk
    return pl.pallas_call(
        functools.partial(_fox_kernel, tk=tk),
        grid=(batch, FOX_HEADS, nq),
        in_specs=[
            pl.BlockSpec((tq, FOX_DH), lambda b, h, i: (b * nq + i, h)),
            pl.BlockSpec((seq, FOX_DH), lambda b, h, i: (b, FOX_HEADS + h)),
            pl.BlockSpec((seq, FOX_DH), lambda b, h, i: (b, 2 * FOX_HEADS + h)),
            pl.BlockSpec((seq, LANES), lambda b, h, i: (b, 0)),
        ],
        out_specs=pl.BlockSpec((tq, FOX_DH), lambda b, h, i: (b * nq + i, h)),
        out_shape=jax.ShapeDtypeStruct((m, FOX_HEADS * FOX_DH), BF16),
        scratch_shapes=[
            pltpu.VMEM((nk, tk, 2 * FOX_DH), BF16),
            pltpu.VMEM((nk, FOX_DH, tk), BF16),
            pltpu.VMEM((nk, FOX_DH + FOX_SUM_ROWS, tk), BF16),
            pltpu.VMEM((2 * FOX_DH, tq), BF16),
            pltpu.VMEM((tk, tq), F32),
            pltpu.VMEM((tk, tq), F32),
            pltpu.VMEM((1, tq), F32),
            pltpu.VMEM((1, tq), F32),
            pltpu.VMEM((1, tq), F32),
            pltpu.VMEM((FOX_DH + FOX_SUM_ROWS, tq), F32),
        ],
        compiler_params=_params(("parallel", "parallel", "arbitrary")),
        name="fox_attention",
    )(proj_b, proj_b, proj_b, c)


def _merge_body(x, oa_ref, ga_ref, ob_ref, g0_ref, g1_ref, hg_ref, wa_ref, wb_ref, wo_ref, post_ref):
    oa = _rms(oa_ref[...].astype(F32), hg_ref[...]) * ga_ref[...].astype(F32)
    ya = _dot(oa.astype(BF16), wa_ref[...])
    yb = _dot(ob_ref[...], wb_ref[...])
    y = g0_ref[...].astype(F32) * ya + g1_ref[...].astype(F32) * yb
    z = _dot(y.astype(BF16), wo_ref[...])
    return x + _rms(z, post_ref[...])


def _memattn_body(x, pre_ref, kv_ref, wq_ref, wo_ref, post_ref):
    h = _rms(x, pre_ref[...]).astype(BF16)
    q = _dot(h, wq_ref[...]).astype(BF16)
    heads = []
    for hd in range(MEM_HEADS):
        lo = hd * MEM_DH
        kh = kv_ref[:, lo:lo + MEM_DH]
        vh = kv_ref[:, D_MODEL + lo:D_MODEL + lo + MEM_DH]
        s = lax.dot_general(q[:, lo:lo + MEM_DH], kh, NT_DIMS, preferred_element_type=F32)
        p = jnp.exp(s - jnp.max(s, axis=1, keepdims=True))
        inv = 1.0 / jnp.sum(p, axis=1, keepdims=True)
        heads.append((_dot(p.astype(BF16), vh) * inv).astype(BF16))
    o = jnp.concatenate(heads, axis=1)
    return x + _rms(_dot(o, wo_ref[...]), post_ref[...])


def _tail_kernel(x_ref, oa_ref, ga_ref, ob_ref, g0_ref, g1_ref, kv_ref, hg_ref, wa_ref, wb_ref,
                 wo_ref, mix_post_ref, mem_pre_ref, wq_ref, wmo_ref, mem_post_ref,
                 ffn_pre_ref, w_in_ref, w_down_ref, ffn_post_ref, o_ref, acc_ref):
    x = _merge_body(x_ref[...], oa_ref, ga_ref, ob_ref, g0_ref, g1_ref, hg_ref, wa_ref, wb_ref,
                    wo_ref, mix_post_ref)
    x = _memattn_body(x, mem_pre_ref, kv_ref, wq_ref, wmo_ref, mem_post_ref)
    o_ref[...] = _ffn_body(x, ffn_pre_ref, w_in_ref, w_down_ref, ffn_post_ref, acc_ref)


def _tail(x, o_a, g_act, o_b, gates, kv, hg_norm_g, w_a, w_b, w_o, mix_post_g, mem_pre_g, w_q,
          w_mo, mem_post_g, ffn_pre_g, w_in, w_down, ffn_post_g, batch, tm=512):
    m, d = x.shape
    ns = m // batch // tm
    row = lambda c: pl.BlockSpec((tm, d), lambda b, s: (b * ns + s, c))
    vec, mat = _resident((1, d)), _resident((d, d))
    r1 = lambda g: g.reshape(1, d)
    return pl.pallas_call(
        _tail_kernel,
        grid=(batch, ns),
        in_specs=[row(0), row(0), row(0), row(0), row(0), row(1),
                  pl.BlockSpec((MEM_LEN, 2 * d), lambda b, s: (b, 0)),
                  vec, mat, mat, mat, vec, vec, mat, mat, vec,
                  vec, _resident(w_in.shape), _resident(w_down.shape), vec],
        out_specs=row(0),
        out_shape=jax.ShapeDtypeStruct((m, d), F32),
        scratch_shapes=[pltpu.VMEM((tm, d), F32)],
        compiler_params=_params(("parallel", "parallel")),
        name="merge_memattn_ffn",
    )(x, o_a, g_act, o_b, gates, gates, kv, r1(hg_norm_g), w_a, w_b, w_o, r1(mix_post_g),
      r1(mem_pre_g), w_q, w_mo, r1(mem_post_g), r1(ffn_pre_g), w_in, w_down, r1(ffn_post_g))


def kernel(x, mem, ffn1_pre_g, ffn1_w_in, ffn1_w_down, ffn1_post_g, mix_pre_g, w_in, hg_lb_logits, hg_norm_g, fox_f_bias, w_branch_a, w_branch_b, b_gate, w_out, mix_post_g, mem_pre_g, mem_kv_g, w_mq, w_mkv, w_mo, mem_post_g, ffn2_pre_g, ffn2_w_in, ffn2_w_down, ffn2_post_g):
    batch, seq, d = x.shape
    depth = ffn1_w_in.shape[0]
    xf = x.reshape(batch * seq, d)
    memf = mem.reshape(batch * MEM_LEN, d)
    kw = HG_HEADS * HG_DK
    off_b = 4 * kw
    off_f = off_b + 3 * FOX_HEADS * FOX_DH
    off_g = off_f + FOX_HEADS
    for l in range(depth):
        xf = _ffn(xf, ffn1_pre_g[l], ffn1_w_in[l].astype(BF16), ffn1_w_down[l].astype(BF16),
                  ffn1_post_g[l])

        wt = jnp.swapaxes(w_in[l], 0, 1)
        fox_w = FOX_HEADS * FOX_DH
        weights = (wt[:off_f].astype(BF16),
                   jnp.pad(wt[off_f:off_g], ((0, LANES - FOX_HEADS), (0, 0))).astype(BF16),
                   wt[off_g:].astype(BF16))
        q_scale = LOG2E / math.sqrt(FOX_DH)
        groups = ((0, 0, "silu", 0, 1.0), (0, kw, None, 0, 1.0), (0, 2 * kw, None, 0, 1.0),
                  (0, 3 * kw, "silu", 0, 1.0), (0, off_b, None, fox_w, q_scale),
                  (1, 0, None, 0, 1.0), (2, 0, "sigmoid_bias", 0, 1.0))
        q_act, f_logit, inp, g_act, proj_b, fb, gates = _inproj(
            xf, mix_pre_g[l], weights, b_gate[l].reshape(1, 2 * d), groups,
            (kw, kw, kw, kw, 3 * fox_w, LANES, 2 * d),
            (BF16, F32, BF16, BF16, BF16, F32, BF16))

        bias = jnp.pad(fox_f_bias[l], (0, LANES - FOX_HEADS)).reshape(1, LANES)
        c = _fcum(fb, bias, batch)

        o_a = _hgrn2(q_act, f_logit, inp, hg_lb_logits, l, batch)
        o_b = _fox(proj_b, c, batch, seq)

        kv = _norm_matmul(memf, mem_kv_g[l], w_mkv[l].astype(BF16), BF16, MEM_LEN, 1024)
        xf = _tail(xf, o_a, g_act, o_b, gates, kv, hg_norm_g[l],
                   w_branch_a[l].astype(BF16), w_branch_b[l].astype(BF16), w_out[l].astype(BF16),
                   mix_post_g[l], mem_pre_g[l],
                   (w_mq[l] * (1.0 / math.sqrt(MEM_DH))).astype(BF16), w_mo[l].astype(BF16),
                   mem_post_g[l], ffn2_pre_g[l], ffn2_w_in[l].astype(BF16),
                   ffn2_w_down[l].astype(BF16), ffn2_post_g[l], batch)
    return xf.reshape(batch, seq, d)
```

```python
import functools
import math

import jax
import jax.numpy as jnp
import numpy as np
from jax import lax
from jax.experimental import pallas as pl
from jax.experimental.pallas import tpu as pltpu

F32 = jnp.float32
BF16 = jnp.bfloat16

D_MODEL = 1024
HG_HEADS = 8
HG_DK = 128
FOX_HEADS = 8
FOX_DH = 128
MEM_LEN = 256
MEM_HEADS = 4
MEM_DH = D_MODEL // MEM_HEADS
D_FF = 2816
EPS = 1e-6
LANES = 128

VMEM_LIMIT = 56 * 1024 * 1024

NT_DIMS = (((1,), (1,)), ((), ()))
TN_DIMS = (((0,), (0,)), ((), ()))


def _params(sem):
    return pltpu.CompilerParams(dimension_semantics=sem, vmem_limit_bytes=VMEM_LIMIT)


def _rms(x, g):
    ms = jnp.mean(x * x, axis=-1, keepdims=True)
    return x * lax.rsqrt(ms + EPS) * g


def _sigmoid(x):
    return 0.5 * jnp.tanh(0.5 * x) + 0.5


def _dot(a, b):
    return jnp.dot(a, b, preferred_element_type=F32)


def _norm_matmul_kernel(x_ref, g_ref, w_ref, o_ref, h_ref):
    @pl.when(pl.program_id(1) == 0)
    def _():
        h_ref[...] = _rms(x_ref[...], g_ref[...]).astype(BF16)

    o_ref[...] = _dot(h_ref[...], w_ref[...]).astype(o_ref.dtype)


def _norm_matmul(x, g, w, out_dtype, tm, tn):
    m, d = x.shape
    n = w.shape[1]
    return pl.pallas_call(
        _norm_matmul_kernel,
        grid=(m // tm, n // tn),
        in_specs=[
            pl.BlockSpec((tm, d), lambda i, j: (i, 0)),
            pl.BlockSpec((1, d), lambda i, j: (0, 0)),
            pl.BlockSpec((d, tn), lambda i, j: (0, j)),
        ],
        out_specs=pl.BlockSpec((tm, tn), lambda i, j: (i, j)),
        out_shape=jax.ShapeDtypeStruct((m, n), out_dtype),
        scratch_shapes=[pltpu.VMEM((tm, d), BF16)],
        compiler_params=_params(("parallel", "arbitrary")),
        name="norm_matmul",
    )(x, g.reshape(1, d), w)


def _resident(shape):
    return pl.BlockSpec(shape, lambda *_: (0,) * len(shape), pipeline_mode=pl.Buffered(1))


def _inproj_kernel(x_ref, g_ref, *refs, groups, tn):
    n_w = 1 + max(grp[0] for grp in groups)
    wt_refs, bias_ref, out_refs = refs[:n_w], refs[n_w], refs[n_w + 1:]
    h = _rms(x_ref[...], g_ref[...]).astype(BF16)
    for o_ref, (wi, row0, act, scale) in zip(out_refs, groups):
        n = o_ref.shape[0] if act == "transpose" else o_ref.shape[1]
        for j in range(0, n, tn):
            width = min(tn, n - j)
            w_rows = wt_refs[wi][row0 + j:row0 + j + width, :]
            if act == "transpose":
                y = lax.dot_general(w_rows, h, NT_DIMS, preferred_element_type=F32)
                o_ref[j:j + width, :] = (y * scale).astype(o_ref.dtype)
                continue
            y = lax.dot_general(h, w_rows, NT_DIMS, preferred_element_type=F32)
            if act == "silu":
                y = y * _sigmoid(y)
            elif act == "sigmoid_bias":
                y = _sigmoid(y + bias_ref[:, j:j + width])
            o_ref[:, j:j + width] = y.astype(o_ref.dtype)


def _inproj(x, g, weights, bias, groups, widths, dtypes, tm=512, tn=512):
    m, d = x.shape
    transposed = [grp[2] == "transpose" for grp in groups]
    return pl.pallas_call(
        functools.partial(_inproj_kernel, groups=groups, tn=tn),
        grid=(m // tm,),
        in_specs=[pl.BlockSpec((tm, d), lambda i: (i, 0)), _resident((1, d))]
        + [_resident(w.shape) for w in weights] + [_resident(bias.shape)],
        out_specs=[pl.BlockSpec((n, tm), lambda i: (0, i)) if t else
                   pl.BlockSpec((tm, n), lambda i: (i, 0)) for n, t in zip(widths, transposed)],
        out_shape=[jax.ShapeDtypeStruct((n, m) if t else (m, n), dt)
                   for n, dt, t in zip(widths, dtypes, transposed)],
        compiler_params=_params(("parallel",)),
        name="input_projection",
    )(x, g.reshape(1, d), *weights, bias)


FFN_CHUNK = 256


def _ffn_body(x, pre_ref, w_in_ref, w_down_ref, post_ref, acc_ref):
    h = _rms(x, pre_ref[...]).astype(BF16)
    for f in range(D_FF // FFN_CHUNK):
        cols = slice(f * FFN_CHUNK, (f + 1) * FFN_CHUNK)
        gate = _dot(h, w_in_ref[:, cols])
        up = _dot(h, w_in_ref[:, D_FF + f * FFN_CHUNK:D_FF + (f + 1) * FFN_CHUNK])
        act = (gate * _sigmoid(gate) * up).astype(BF16)
        part = _dot(act, w_down_ref[cols, :])
        if f == 0:
            acc_ref[...] = part
        else:
            acc_ref[...] += part
    return x + 0.5 * _rms(acc_ref[...], post_ref[...])


def _ffn_kernel(x_ref, pre_ref, w_in_ref, w_down_ref, post_ref, o_ref, acc_ref):
    o_ref[...] = _ffn_body(x_ref[...], pre_ref, w_in_ref, w_down_ref, post_ref, acc_ref)


def _ffn(x, pre_g, w_in, w_down, post_g, tm=1024):
    m, d = x.shape
    return pl.pallas_call(
        _ffn_kernel,
        grid=(m // tm,),
        in_specs=[
            pl.BlockSpec((tm, d), lambda i: (i, 0)),
            _resident((1, d)),
            _resident(w_in.shape),
            _resident(w_down.shape),
            _resident((1, d)),
        ],
        out_specs=pl.BlockSpec((tm, d), lambda i: (i, 0)),
        out_shape=jax.ShapeDtypeStruct((m, d), F32),
        scratch_shapes=[pltpu.VMEM((tm, d), F32)],
        compiler_params=_params(("parallel",)),
        name="ffn",
    )(x, pre_g.reshape(1, d), w_in, w_down, post_g.reshape(1, d))


def _split3(x):
    hi = x.astype(BF16)
    r = x - hi.astype(F32)
    mid = r.astype(BF16)
    lo = (r - mid.astype(F32)).astype(BF16)
    return hi, mid, lo


def _fcum_kernel(fb_ref, bias_ref, c_ref, carry_ref):
    @pl.when(pl.program_id(1) == 0)
    def _():
        carry_ref[...] = jnp.zeros_like(carry_ref)

    z = fb_ref[...] + bias_ref[...]
    ls = jnp.minimum(z, 0.0) - jnp.log(1.0 + jnp.exp(-jnp.abs(z)))
    tb = z.shape[0]
    row = lax.broadcasted_iota(jnp.int32, (tb, tb), 0)
    col = lax.broadcasted_iota(jnp.int32, (tb, tb), 1)
    tril = (col <= row).astype(BF16)
    hi, mid, lo = _split3(ls)
    c = _dot(tril, hi) + _dot(tril, mid) + _dot(tril, lo) + carry_ref[...]
    c_ref[...] = c
    carry_ref[...] = c[tb - 1:tb, :]


def _fcum(fb, bias, batch, tb=512):
    m = fb.shape[0]
    ns = m // batch // tb
    return pl.pallas_call(
        _fcum_kernel,
        grid=(batch, ns),
        in_specs=[
            pl.BlockSpec((tb, LANES), lambda b, s: (b * ns + s, 0)),
            pl.BlockSpec((1, LANES), lambda b, s: (0, 0)),
        ],
        out_specs=pl.BlockSpec((tb, LANES), lambda b, s: (b * ns + s, 0)),
        out_shape=jax.ShapeDtypeStruct((m, LANES), F32),
        scratch_shapes=[pltpu.VMEM((1, LANES), F32)],
        compiler_params=_params(("parallel", "arbitrary")),
        name="fox_decay_cumsum",
    )(fb, bias)


def _hgrn_level_table(c):
    t = np.arange(c)[:, None]
    s = np.arange(c)[None, :]
    level = np.full((c, c), -2, np.int32)
    level[t == s] = -1
    for l in range(int(math.log2(c))):
        h = c >> (l + 1)
        mid = (t // (2 * h)) * (2 * h) + h
        sel = (t // (2 * h) == s // (2 * h)) & (t >= mid) & (s < mid)
        level[sel & (level == -2)] = l
    return level


SUBLANES = 8


def _prefix_sum_rows(x, row):
    k = 1
    while k < x.shape[0]:
        x = x + jnp.where(row >= k, pltpu.roll(x, k, axis=0), 0.0)
        k *= 2
    return x


def _hgrn_sign_table(c):
    t = np.arange(c)[:, None]
    halves = [c >> (l + 1) for l in range(int(math.log2(c)))]
    sign = [np.where((t % (2 * h)) >= h, 1.0, -1.0) * np.ones((1, HG_DK)) for h in halves]
    return np.stack(sign).astype(np.float32)


def _midpoint_rows(b, h):
    c, w = b.shape
    if h >= SUBLANES:
        pieces = [jnp.broadcast_to(b[j * 2 * h + h - 1:j * 2 * h + h, :], (2 * h, w))
                  for j in range(c // (2 * h))]
        return pieces[0] if len(pieces) == 1 else jnp.concatenate(pieces, axis=0)
    groups = c // SUBLANES
    b3 = b.reshape(groups, SUBLANES, w)
    sub = lax.broadcasted_iota(jnp.int32, (1, SUBLANES, w), 1)
    r3 = None
    for j in range(SUBLANES // (2 * h)):
        src = j * 2 * h + h - 1
        piece = jnp.broadcast_to(b3[:, src:src + 1, :], (groups, SUBLANES, w))
        r3 = piece if r3 is None else jnp.where(sub >= j * 2 * h, piece, r3)
    return r3.reshape(c, w)


def _hgrn_kernel(q_ref, f_ref, i_ref, lb_ref, lvl_ref, sign_ref, o_ref, st_ref, *, chunk, layer):
    @pl.when(pl.program_id(2) == 0)
    def _():
        st_ref[...] = jnp.zeros_like(st_ref)

    c = chunk
    n_lvl = int(math.log2(c))
    ts = q_ref.shape[0]
    logits = lb_ref[...]
    e = jnp.exp(logits - jnp.max(logits, axis=0, keepdims=True))
    lb = jnp.sum(e[0:layer + 1, :], axis=0, keepdims=True) / jnp.sum(e, axis=0, keepdims=True)
    lvl = lvl_ref[...]
    st = st_ref[...]

    row = lax.broadcasted_iota(jnp.int32, (c, HG_DK), 0)
    ks, bs = [], []
    for ci in range(ts // c):
        f = lb + (1.0 - lb) * _sigmoid(f_ref[pl.ds(ci * c, c), :])
        bs.append(_prefix_sum_rows(jnp.log2(f), row))
        ks.append(1.0 - f)

    chunks = range(ts // c)
    q16s = [q_ref[pl.ds(ci * c, c), :] for ci in chunks]
    vs = [i_ref[pl.ds(ci * c, c), :] for ci in chunks]
    k16s = [k.astype(BF16) for k in ks]

    a = [jnp.where(lvl == -1, lax.dot_general(q16s[ci], k16s[ci], NT_DIMS,
                                              preferred_element_type=F32), 0.0) for ci in chunks]
    for l in range(n_lvl):
        h = c >> (l + 1)
        for ci in chunks:
            d_l = (bs[ci] - _midpoint_rows(bs[ci], h)) * sign_ref[l]
            e_l = jnp.exp2(d_l).astype(BF16)
            p = lax.dot_general(q16s[ci] * e_l, k16s[ci] * e_l, NT_DIMS,
                                preferred_element_type=F32)
            a[ci] = jnp.where(lvl == l, p, a[ci])

    o_intra = [_dot(a[ci].astype(BF16), vs[ci]) for ci in chunks]
    upds = [lax.dot_general(vs[ci], (ks[ci] * jnp.exp2(bs[ci][c - 1:c, :] - bs[ci])).astype(BF16),
                            TN_DIMS, preferred_element_type=F32) for ci in chunks]

    for ci in chunks:
        o = o_intra[ci] + lax.dot_general(q16s[ci] * jnp.exp2(bs[ci]).astype(BF16),
                                          st.astype(BF16), NT_DIMS, preferred_element_type=F32)
        o_ref[pl.ds(ci * c, c), :] = o.astype(o_ref.dtype)
        st = st * jnp.exp2(bs[ci][c - 1:c, :]) + upds[ci]

    st_ref[...] = st


def _hgrn2(q_act, f_logit, inp, lb_logits, layer, batch, ts=2048, chunk=128):
    m = q_act.shape[0]
    ns = m // batch // ts
    level = _hgrn_level_table(chunk)
    sign = _hgrn_sign_table(chunk)
    nl = lb_logits.shape[0]
    blk = pl.BlockSpec((ts, HG_DK), lambda b, h, s: (b * ns + s, h))
    return pl.pallas_call(
        functools.partial(_hgrn_kernel, chunk=chunk, layer=layer),
        grid=(batch, HG_HEADS, ns),
        in_specs=[
            blk, blk, blk,
            pl.BlockSpec((nl, HG_DK), lambda b, h, s: (0, h)),
            _resident(level.shape), _resident(sign.shape),
        ],
        out_specs=blk,
        out_shape=jax.ShapeDtypeStruct((m, HG_HEADS * HG_DK), BF16),
        scratch_shapes=[pltpu.VMEM((HG_DK, HG_DK), F32)],
        compiler_params=_params(("parallel", "parallel", "arbitrary")),
        name="hgrn2",
    )(q_act, f_logit, inp, lb_logits.reshape(nl, HG_HEADS * HG_DK), jnp.asarray(level),
      jnp.asarray(sign))


LOG2E = 1.4426950408889634
FOX_FOLD = 8
FOX_SUM_ROWS = 16


def _decay_features_t(c_blk, h):
    rows = c_blk.shape[0]
    heads = c_blk.T[0:SUBLANES, :]
    sub = lax.broadcasted_iota(jnp.int32, (SUBLANES, rows), 0)
    c_row = jnp.sum(jnp.where(sub == h, heads, 0.0), axis=0, keepdims=True)
    hi, mid, lo = (jnp.broadcast_to(part.astype(F32), (SUBLANES, rows))
                   for part in _split3(c_row * LOG2E))

    def place(base, sign):
        ones = (sub >= 3 - base) & (sub < 6 - base)
        feat = jnp.where(sub == base, sign * hi,
                         jnp.where(sub == base + 1, sign * mid,
                                   jnp.where(sub == base + 2, sign * lo,
                                             jnp.where(ones, 1.0, 0.0))))
        pad = jnp.zeros((LANES - SUBLANES, rows), F32)
        return jnp.concatenate([feat, pad], axis=0).astype(BF16)

    return place(3, 1.0), place(0, -1.0)


def _fox_kernel(q_ref, k_ref, v_ref, c_ref, o_ref, kx_ref, qft_ref, vt_ref, qxt_ref,
                s0_ref, s1_ref, x0_ref, x1_ref, m_ref, acc_ref, *, tk):
    h = pl.program_id(1)
    qi = pl.program_id(2)
    tq = q_ref.shape[1]
    nk = kx_ref.shape[0]

    @pl.when(qi == 0)
    def _():
        ones_row = (lax.broadcasted_iota(jnp.int32, (FOX_SUM_ROWS, tk), 0) == 0)
        ones_row = ones_row.astype(F32).astype(BF16)
        for kb in range(nk):
            rows = slice(kb * tk, (kb + 1) * tk)
            q_feat_t, k_feat_t = _decay_features_t(c_ref[rows, :], h)
            kx_ref[kb, :, 0:FOX_DH] = k_ref[rows, :]
            kx_ref[kb, :, FOX_DH:2 * FOX_DH] = k_feat_t.T
            qft_ref[kb] = q_feat_t
            vt_ref[kb, 0:FOX_DH, :] = v_ref[:, rows]
            vt_ref[kb, FOX_DH:FOX_DH + FOX_SUM_ROWS, :] = ones_row

    ratio = tq // tk
    n_full = ratio * qi
    qxt_ref[0:FOX_DH, :] = q_ref[...]
    for j in range(ratio):
        qxt_ref[FOX_DH:2 * FOX_DH, j * tk:(j + 1) * tk] = qft_ref[n_full + j]
    m_ref[...] = jnp.full_like(m_ref, -jnp.inf)
    acc_ref[...] = jnp.zeros_like(acc_ref)

    def fold(x, op):
        part = op(x.reshape(FOX_FOLD, tk // FOX_FOLD, x.shape[1]), axis=0)
        return op(part, axis=0, keepdims=True)

    def scores(kb, s_ref, smax_ref):
        st = _dot(kx_ref[kb], qxt_ref[...])
        s_ref[...] = st
        smax_ref[...] = fold(st, jnp.max)

    def causal(st):
        key = lax.broadcasted_iota(jnp.int32, st.shape, 0)
        qry = lax.broadcasted_iota(jnp.int32, st.shape, 1)
        return jnp.where(key <= qry, st, -jnp.inf)

    def softmax_step(kb, st, blk_max, cols):
        m_prev = m_ref[:, cols]
        m_new = jnp.maximum(m_prev, blk_max)
        alpha = jnp.exp2(m_prev - m_new)
        p = jnp.exp2(st - m_new)
        m_ref[:, cols] = m_new
        acc_ref[:, cols] = alpha * acc_ref[:, cols] + _dot(vt_ref[kb], p.astype(BF16))

    def update(kb, s_ref, smax_ref):
        softmax_step(kb, s_ref[...], smax_ref[...], slice(None))

    def update_diagonal(kb, s_ref):
        st = s_ref[...]
        head = causal(st[:, 0:tk])
        st = head if tq == tk else jnp.concatenate([head, st[:, tk:]], axis=1)
        softmax_step(kb, st, fold(st, jnp.max), slice(None))

    scores(0, s0_ref, x0_ref)

    def pair(kb):
        scores(kb + 1, s1_ref, x1_ref)
        update(kb, s0_ref, x0_ref)
        scores(kb + 2, s0_ref, x0_ref)
        update(kb + 1, s1_ref, x1_ref)

    def body4(j, carry):
        pair(4 * j)
        pair(4 * j + 2)
        return carry

    def body2(j, carry):
        pair(2 * j)
        return carry

    n_pairs = n_full // 2
    lax.fori_loop(0, n_pairs // 2, body4, 0)
    lax.fori_loop(2 * (n_pairs // 2), n_pairs, body2, 0)

    if ratio == 2:
        upper = slice(tk, tq)
        tile = causal(_dot(kx_ref[n_full + 1], qxt_ref[:, upper]))
        update_diagonal(n_full, s0_ref)
        softmax_step(n_full + 1, tile, fold(tile, jnp.max), upper)
    else:
        @pl.when(qi % 2 == 1)
        def _():
            scores(qi, s1_ref, x1_ref)
            update(qi - 1, s0_ref, x0_ref)
            update_diagonal(qi, s1_ref)

        @pl.when(qi % 2 == 0)
        def _():
            update_diagonal(qi, s0_ref)

    out = acc_ref[0:FOX_DH, :] / acc_ref[FOX_DH:FOX_DH + 1, :]
    o_ref[...] = out.astype(o_ref.dtype).T


def _fox(q_t, k, v_t, c, batch, seq, tq=1024, tk=512):
    assert tq in (tk, 2 * tk)
    m = k.shape[0]
    nq = seq // tq
    nk = seq // tk
    return pl.pallas_call(
        functools.partial(_fox_kernel, tk=tk),
        grid=(batch, FOX_HEADS, nq),
        in_specs=[
            pl.BlockSpec((FOX_DH, tq), lambda b, h, i: (h, b * nq + i)),
            pl.BlockSpec((seq, FOX_DH), lambda b, h, i: (b, h)),
            pl.BlockSpec((FOX_DH, seq), lambda b, h, i: (h, b)),
            pl.BlockSpec((seq, LANES), lambda b, h, i: (b, 0)),
        ],
        out_specs=pl.BlockSpec((tq, FOX_DH), lambda b, h, i: (b * nq + i, h)),
        out_shape=jax.ShapeDtypeStruct((m, FOX_HEADS * FOX_DH), BF16),
        scratch_shapes=[
            pltpu.VMEM((nk, tk, 2 * FOX_DH), BF16),
            pltpu.VMEM((nk, FOX_DH, tk), BF16),
            pltpu.VMEM((nk, FOX_DH + FOX_SUM_ROWS, tk), BF16),
            pltpu.VMEM((2 * FOX_DH, tq), BF16),
            pltpu.VMEM((tk, tq), F32),
            pltpu.VMEM((tk, tq), F32),
            pltpu.VMEM((1, tq), F32),
            pltpu.VMEM((1, tq), F32),
            pltpu.VMEM((1, tq), F32),
            pltpu.VMEM((FOX_DH + FOX_SUM_ROWS, tq), F32),
        ],
        compiler_params=_params(("parallel", "parallel", "arbitrary")),
        name="fox_attention",
    )(q_t, k, v_t, c)


def _merge_body(x, oa_ref, ga_ref, ob_ref, g0_ref, g1_ref, hg_ref, wa_ref, wb_ref, wo_ref, post_ref):
    oa = _rms(oa_ref[...].astype(F32), hg_ref[...]) * ga_ref[...].astype(F32)
    ya = _dot(oa.astype(BF16), wa_ref[...])
    yb = _dot(ob_ref[...], wb_ref[...])
    y = g0_ref[...].astype(F32) * ya + g1_ref[...].astype(F32) * yb
    z = _dot(y.astype(BF16), wo_ref[...])
    return x + _rms(z, post_ref[...])


def _memattn_body(x, pre_ref, kv_ref, wq_ref, wo_ref, post_ref):
    h = _rms(x, pre_ref[...]).astype(BF16)
    q = _dot(h, wq_ref[...]).astype(BF16)
    heads = []
    for hd in range(MEM_HEADS):
        lo = hd * MEM_DH
        kh = kv_ref[:, lo:lo + MEM_DH]
        vh = kv_ref[:, D_MODEL + lo:D_MODEL + lo + MEM_DH]
        s = lax.dot_general(q[:, lo:lo + MEM_DH], kh, NT_DIMS, preferred_element_type=F32)
        p = jnp.exp(s - jnp.max(s, axis=1, keepdims=True))
        inv = 1.0 / jnp.sum(p, axis=1, keepdims=True)
        heads.append((_dot(p.astype(BF16), vh) * inv).astype(BF16))
    o = jnp.concatenate(heads, axis=1)
    return x + _rms(_dot(o, wo_ref[...]), post_ref[...])


def _tail_kernel(x_ref, oa_ref, ga_ref, ob_ref, g0_ref, g1_ref, kv_ref, hg_ref, wa_ref, wb_ref,
                 wo_ref, mix_post_ref, mem_pre_ref, wq_ref, wmo_ref, mem_post_ref,
                 ffn_pre_ref, w_in_ref, w_down_ref, ffn_post_ref, o_ref, acc_ref):
    x = _merge_body(x_ref[...], oa_ref, ga_ref, ob_ref, g0_ref, g1_ref, hg_ref, wa_ref, wb_ref,
                    wo_ref, mix_post_ref)
    x = _memattn_body(x, mem_pre_ref, kv_ref, wq_ref, wmo_ref, mem_post_ref)
    o_ref[...] = _ffn_body(x, ffn_pre_ref, w_in_ref, w_down_ref, ffn_post_ref, acc_ref)


def _tail(x, o_a, g_act, o_b, gates, kv, hg_norm_g, w_a, w_b, w_o, mix_post_g, mem_pre_g, w_q,
          w_mo, mem_post_g, ffn_pre_g, w_in, w_down, ffn_post_g, batch, tm=512):
    m, d = x.shape
    ns = m // batch // tm
    row = lambda c: pl.BlockSpec((tm, d), lambda b, s: (b * ns + s, c))
    vec, mat = _resident((1, d)), _resident((d, d))
    r1 = lambda g: g.reshape(1, d)
    return pl.pallas_call(
        _tail_kernel,
        grid=(batch, ns),
        in_specs=[row(0), row(0), row(0), row(0), row(0), row(1),
                  pl.BlockSpec((MEM_LEN, 2 * d), lambda b, s: (b, 0)),
                  vec, mat, mat, mat, vec, vec, mat, mat, vec,
                  vec, _resident(w_in.shape), _resident(w_down.shape), vec],
        out_specs=row(0),
        out_shape=jax.ShapeDtypeStruct((m, d), F32),
        scratch_shapes=[pltpu.VMEM((tm, d), F32)],
        compiler_params=_params(("parallel", "parallel")),
        name="merge_memattn_ffn",
    )(x, o_a, g_act, o_b, gates, gates, kv, r1(hg_norm_g), w_a, w_b, w_o, r1(mix_post_g),
      r1(mem_pre_g), w_q, w_mo, r1(mem_post_g), r1(ffn_pre_g), w_in, w_down, r1(ffn_post_g))


def kernel(x, mem, ffn1_pre_g, ffn1_w_in, ffn1_w_down, ffn1_post_g, mix_pre_g, w_in, hg_lb_logits, hg_norm_g, fox_f_bias, w_branch_a, w_branch_b, b_gate, w_out, mix_post_g, mem_pre_g, mem_kv_g, w_mq, w_mkv, w_mo, mem_post_g, ffn2_pre_g, ffn2_w_in, ffn2_w_down, ffn2_post_g):
    batch, seq, d = x.shape
    depth = ffn1_w_in.shape[0]
    xf = x.reshape(batch * seq, d)
    memf = mem.reshape(batch * MEM_LEN, d)
    kw = HG_HEADS * HG_DK
    off_b = 4 * kw
    off_f = off_b + 3 * FOX_HEADS * FOX_DH
    off_g = off_f + FOX_HEADS
    for l in range(depth):
        xf = _ffn(xf, ffn1_pre_g[l], ffn1_w_in[l].astype(BF16), ffn1_w_down[l].astype(BF16),
                  ffn1_post_g[l])

        wt = jnp.swapaxes(w_in[l], 0, 1)
        fox_w = FOX_HEADS * FOX_DH
        weights = (wt[:off_f].astype(BF16),
                   jnp.pad(wt[off_f:off_g], ((0, LANES - FOX_HEADS), (0, 0))).astype(BF16),
                   wt[off_g:].astype(BF16))
        q_scale = LOG2E / math.sqrt(FOX_DH)
        groups = ((0, 0, "silu", 1.0), (0, kw, None, 1.0), (0, 2 * kw, None, 1.0),
                  (0, 3 * kw, "silu", 1.0), (0, off_b, "transpose", q_scale),
                  (0, off_b + fox_w, None, 1.0), (0, off_b + 2 * fox_w, "transpose", 1.0),
                  (1, 0, None, 1.0), (2, 0, "sigmoid_bias", 1.0))
        q_act, f_logit, inp, g_act, q_t, k_b, v_t, fb, gates = _inproj(
            xf, mix_pre_g[l], weights, b_gate[l].reshape(1, 2 * d), groups,
            (kw, kw, kw, kw, fox_w, fox_w, fox_w, LANES, 2 * d),
            (BF16, F32, BF16, BF16, BF16, BF16, BF16, F32, BF16))

        bias = jnp.pad(fox_f_bias[l], (0, LANES - FOX_HEADS)).reshape(1, LANES)
        c = _fcum(fb, bias, batch)

        o_a = _hgrn2(q_act, f_logit, inp, hg_lb_logits, l, batch)
        o_b = _fox(q_t, k_b, v_t, c, batch, seq)

        kv = _norm_matmul(memf, mem_kv_g[l], w_mkv[l].astype(BF16), BF16, MEM_LEN, 1024)
        xf = _tail(xf, o_a, g_act, o_b, gates, kv, hg_norm_g[l],
                   w_branch_a[l].astype(BF16), w_branch_b[l].astype(BF16), w_out[l].astype(BF16),
                   mix_post_g[l], mem_pre_g[l],
                   (w_mq[l] * (1.0 / math.sqrt(MEM_DH))).astype(BF16), w_mo[l].astype(BF16),
                   mem_post_g[l], ffn2_pre_g[l], ffn2_w_in[l].astype(BF16),
                   ffn2_w_down[l].astype(BF16), ffn2_post_g[l], batch)
    return xf.reshape(batch, seq, d)
```

```python
import functools
import math

import jax
import jax.numpy as jnp
import numpy as np
from jax import lax
from jax.experimental import pallas as pl
from jax.experimental.pallas import tpu as pltpu

F32 = jnp.float32
BF16 = jnp.bfloat16

D_MODEL = 1024
HG_HEADS = 8
HG_DK = 128
FOX_HEADS = 8
FOX_DH = 128
MEM_LEN = 256
MEM_HEADS = 4
MEM_DH = D_MODEL // MEM_HEADS
D_FF = 2816
EPS = 1e-6
LANES = 128

VMEM_LIMIT = 56 * 1024 * 1024

NT_DIMS = (((1,), (1,)), ((), ()))
TN_DIMS = (((0,), (0,)), ((), ()))


def _params(sem):
    return pltpu.CompilerParams(dimension_semantics=sem, vmem_limit_bytes=VMEM_LIMIT)


def _rms(x, g):
    ms = jnp.mean(x * x, axis=-1, keepdims=True)
    return x * lax.rsqrt(ms + EPS) * g


def _sigmoid(x):
    return 0.5 * jnp.tanh(0.5 * x) + 0.5


def _dot(a, b):
    return jnp.dot(a, b, preferred_element_type=F32)


def _norm_matmul_kernel(x_ref, g_ref, w_ref, o_ref, h_ref):
    @pl.when(pl.program_id(1) == 0)
    def _():
        h_ref[...] = _rms(x_ref[...], g_ref[...]).astype(BF16)

    o_ref[...] = _dot(h_ref[...], w_ref[...]).astype(o_ref.dtype)


def _norm_matmul(x, g, w, out_dtype, tm, tn):
    m, d = x.shape
    n = w.shape[1]
    return pl.pallas_call(
        _norm_matmul_kernel,
        grid=(m // tm, n // tn),
        in_specs=[
            pl.BlockSpec((tm, d), lambda i, j: (i, 0)),
            pl.BlockSpec((1, d), lambda i, j: (0, 0)),
            pl.BlockSpec((d, tn), lambda i, j: (0, j)),
        ],
        out_specs=pl.BlockSpec((tm, tn), lambda i, j: (i, j)),
        out_shape=jax.ShapeDtypeStruct((m, n), out_dtype),
        scratch_shapes=[pltpu.VMEM((tm, d), BF16)],
        compiler_params=_params(("parallel", "arbitrary")),
        name="norm_matmul",
    )(x, g.reshape(1, d), w)


def _resident(shape):
    return pl.BlockSpec(shape, lambda *_: (0,) * len(shape), pipeline_mode=pl.Buffered(1))


def _inproj_kernel(x_ref, g_ref, *refs, groups, tn):
    n_w = 1 + max(grp[0] for grp in groups)
    wt_refs, bias_ref, out_refs = refs[:n_w], refs[n_w], refs[n_w + 1:]
    h = _rms(x_ref[...], g_ref[...]).astype(BF16)
    for o_ref, (wi, row0, act, scale) in zip(out_refs, groups):
        n = o_ref.shape[0] if act == "transpose" else o_ref.shape[1]
        for j in range(0, n, tn):
            width = min(tn, n - j)
            w_rows = wt_refs[wi][row0 + j:row0 + j + width, :]
            if act == "transpose":
                y = lax.dot_general(w_rows, h, NT_DIMS, preferred_element_type=F32)
                o_ref[j:j + width, :] = (y * scale).astype(o_ref.dtype)
                continue
            y = lax.dot_general(h, w_rows, NT_DIMS, preferred_element_type=F32)
            if act == "silu":
                y = y * _sigmoid(y)
            elif act == "sigmoid_bias":
                y = _sigmoid(y + bias_ref[:, j:j + width])
            o_ref[:, j:j + width] = y.astype(o_ref.dtype)


def _inproj(x, g, weights, bias, groups, widths, dtypes, tm=512, tn=512):
    m, d = x.shape
    transposed = [grp[2] == "transpose" for grp in groups]
    return pl.pallas_call(
        functools.partial(_inproj_kernel, groups=groups, tn=tn),
        grid=(m // tm,),
        in_specs=[pl.BlockSpec((tm, d), lambda i: (i, 0)), _resident((1, d))]
        + [_resident(w.shape) for w in weights] + [_resident(bias.shape)],
        out_specs=[pl.BlockSpec((n, tm), lambda i: (0, i)) if t else
                   pl.BlockSpec((tm, n), lambda i: (i, 0)) for n, t in zip(widths, transposed)],
        out_shape=[jax.ShapeDtypeStruct((n, m) if t else (m, n), dt)
                   for n, dt, t in zip(widths, dtypes, transposed)],
        compiler_params=_params(("parallel",)),
        name="input_projection",
    )(x, g.reshape(1, d), *weights, bias)


FFN_CHUNK = 256


def _ffn_body(x, pre_ref, w_in_ref, w_down_ref, post_ref, acc_ref):
    h = _rms(x, pre_ref[...]).astype(BF16)
    for f in range(D_FF // FFN_CHUNK):
        cols = slice(f * FFN_CHUNK, (f + 1) * FFN_CHUNK)
        gate = _dot(h, w_in_ref[:, cols])
        up = _dot(h, w_in_ref[:, D_FF + f * FFN_CHUNK:D_FF + (f + 1) * FFN_CHUNK])
        act = (gate * _sigmoid(gate) * up).astype(BF16)
        part = _dot(act, w_down_ref[cols, :])
        if f == 0:
            acc_ref[...] = part
        else:
            acc_ref[...] += part
    return x + 0.5 * _rms(acc_ref[...], post_ref[...])


def _ffn_kernel(x_ref, pre_ref, w_in_ref, w_down_ref, post_ref, o_ref, acc_ref):
    o_ref[...] = _ffn_body(x_ref[...], pre_ref, w_in_ref, w_down_ref, post_ref, acc_ref)


def _ffn(x, pre_g, w_in, w_down, post_g, tm=1024):
    m, d = x.shape
    return pl.pallas_call(
        _ffn_kernel,
        grid=(m // tm,),
        in_specs=[
            pl.BlockSpec((tm, d), lambda i: (i, 0)),
            _resident((1, d)),
            _resident(w_in.shape),
            _resident(w_down.shape),
            _resident((1, d)),
        ],
        out_specs=pl.BlockSpec((tm, d), lambda i: (i, 0)),
        out_shape=jax.ShapeDtypeStruct((m, d), F32),
        scratch_shapes=[pltpu.VMEM((tm, d), F32)],
        compiler_params=_params(("parallel",)),
        name="ffn",
    )(x, pre_g.reshape(1, d), w_in, w_down, post_g.reshape(1, d))


def _split3(x):
    hi = x.astype(BF16)
    r = x - hi.astype(F32)
    mid = r.astype(BF16)
    lo = (r - mid.astype(F32)).astype(BF16)
    return hi, mid, lo


def _fcum_kernel(fb_ref, bias_ref, c_ref, carry_ref):
    @pl.when(pl.program_id(1) == 0)
    def _():
        carry_ref[...] = jnp.zeros_like(carry_ref)

    z = fb_ref[0:FOX_HEADS, :] + bias_ref[...]
    ls = jnp.minimum(z, 0.0) - jnp.log(1.0 + jnp.exp(-jnp.abs(z)))
    tb = z.shape[1]
    row = lax.broadcasted_iota(jnp.int32, (tb, tb), 0)
    col = lax.broadcasted_iota(jnp.int32, (tb, tb), 1)
    triu = (row <= col).astype(BF16)
    stacked = jnp.concatenate([p.astype(F32) for p in _split3(ls)], axis=0).astype(BF16)
    parts = _dot(stacked, triu)
    c = (parts[0:FOX_HEADS] + parts[FOX_HEADS:2 * FOX_HEADS] + parts[2 * FOX_HEADS:]
         + carry_ref[...])
    c_ref[...] = c
    carry_ref[...] = jnp.broadcast_to(c[:, tb - 1:tb], carry_ref.shape)


def _fcum(fb_t, bias, batch, seq, tb=512):
    ns = seq // tb
    return pl.pallas_call(
        _fcum_kernel,
        grid=(batch, ns),
        in_specs=[
            pl.BlockSpec((LANES, tb), lambda b, s: (0, b * ns + s)),
            pl.BlockSpec((FOX_HEADS, 1), lambda b, s: (0, 0)),
        ],
        out_specs=pl.BlockSpec((None, FOX_HEADS, tb), lambda b, s: (b, 0, s)),
        out_shape=jax.ShapeDtypeStruct((batch, FOX_HEADS, seq), F32),
        scratch_shapes=[pltpu.VMEM((FOX_HEADS, tb), F32)],
        compiler_params=_params(("parallel", "arbitrary")),
        name="fox_decay_cumsum",
    )(fb_t, bias)


def _hgrn_level_table(c):
    t = np.arange(c)[:, None]
    s = np.arange(c)[None, :]
    level = np.full((c, c), -2, np.int32)
    level[t == s] = -1
    for l in range(int(math.log2(c))):
        h = c >> (l + 1)
        mid = (t // (2 * h)) * (2 * h) + h
        sel = (t // (2 * h) == s // (2 * h)) & (t >= mid) & (s < mid)
        level[sel & (level == -2)] = l
    return level


SUBLANES = 8


def _prefix_sum_rows(x, row):
    k = 1
    while k < x.shape[0]:
        x = x + jnp.where(row >= k, pltpu.roll(x, k, axis=0), 0.0)
        k *= 2
    return x


def _hgrn_sign_table(c):
    t = np.arange(c)[:, None]
    halves = [c >> (l + 1) for l in range(int(math.log2(c)))]
    sign = [np.where((t % (2 * h)) >= h, 1.0, -1.0) * np.ones((1, HG_DK)) for h in halves]
    return np.stack(sign).astype(np.float32)


def _midpoint_rows(b, h):
    c, w = b.shape
    if h >= SUBLANES:
        pieces = [jnp.broadcast_to(b[j * 2 * h + h - 1:j * 2 * h + h, :], (2 * h, w))
                  for j in range(c // (2 * h))]
        return pieces[0] if len(pieces) == 1 else jnp.concatenate(pieces, axis=0)
    groups = c // SUBLANES
    b3 = b.reshape(groups, SUBLANES, w)
    sub = lax.broadcasted_iota(jnp.int32, (1, SUBLANES, w), 1)
    r3 = None
    for j in range(SUBLANES // (2 * h)):
        src = j * 2 * h + h - 1
        piece = jnp.broadcast_to(b3[:, src:src + 1, :], (groups, SUBLANES, w))
        r3 = piece if r3 is None else jnp.where(sub >= j * 2 * h, piece, r3)
    return r3.reshape(c, w)


def _hgrn_kernel(q_ref, f_ref, i_ref, lb_ref, lvl_ref, sign_ref, o_ref, st_ref, *, chunk, layer):
    @pl.when(pl.program_id(2) == 0)
    def _():
        st_ref[...] = jnp.zeros_like(st_ref)

    c = chunk
    n_lvl = int(math.log2(c))
    ts = q_ref.shape[0]
    logits = lb_ref[...]
    e = jnp.exp(logits - jnp.max(logits, axis=0, keepdims=True))
    lb = jnp.sum(e[0:layer + 1, :], axis=0, keepdims=True) / jnp.sum(e, axis=0, keepdims=True)
    lvl = lvl_ref[...]
    st = st_ref[...]

    row = lax.broadcasted_iota(jnp.int32, (c, HG_DK), 0)
    ks, bs = [], []
    for ci in range(ts // c):
        f = lb + (1.0 - lb) * _sigmoid(f_ref[pl.ds(ci * c, c), :])
        bs.append(_prefix_sum_rows(jnp.log2(f), row))
        ks.append(1.0 - f)

    chunks = range(ts // c)
    q16s = [q_ref[pl.ds(ci * c, c), :] for ci in chunks]
    vs = [i_ref[pl.ds(ci * c, c), :] for ci in chunks]
    k16s = [k.astype(BF16) for k in ks]

    a = [jnp.where(lvl == -1, lax.dot_general(q16s[ci], k16s[ci], NT_DIMS,
                                              preferred_element_type=F32), 0.0) for ci in chunks]
    for l in range(n_lvl):
        h = c >> (l + 1)
        for ci in chunks:
            d_l = (bs[ci] - _midpoint_rows(bs[ci], h)) * sign_ref[l]
            e_l = jnp.exp2(d_l).astype(BF16)
            p = lax.dot_general(q16s[ci] * e_l, k16s[ci] * e_l, NT_DIMS,
                                preferred_element_type=F32)
            a[ci] = jnp.where(lvl == l, p, a[ci])

    o_intra = [_dot(a[ci].astype(BF16), vs[ci]) for ci in chunks]
    upds = [lax.dot_general(vs[ci], (ks[ci] * jnp.exp2(bs[ci][c - 1:c, :] - bs[ci])).astype(BF16),
                            TN_DIMS, preferred_element_type=F32) for ci in chunks]

    for ci in chunks:
        o = o_intra[ci] + lax.dot_general(q16s[ci] * jnp.exp2(bs[ci]).astype(BF16),
                                          st.astype(BF16), NT_DIMS, preferred_element_type=F32)
        o_ref[pl.ds(ci * c, c), :] = o.astype(o_ref.dtype)
        st = st * jnp.exp2(bs[ci][c - 1:c, :]) + upds[ci]

    st_ref[...] = st


def _hgrn2(q_act, f_logit, inp, lb_logits, layer, batch, ts=2048, chunk=128):
    m = q_act.shape[0]
    ns = m // batch // ts
    level = _hgrn_level_table(chunk)
    sign = _hgrn_sign_table(chunk)
    nl = lb_logits.shape[0]
    blk = pl.BlockSpec((ts, HG_DK), lambda b, h, s: (b * ns + s, h))
    return pl.pallas_call(
        functools.partial(_hgrn_kernel, chunk=chunk, layer=layer),
        grid=(batch, HG_HEADS, ns),
        in_specs=[
            blk, blk, blk,
            pl.BlockSpec((nl, HG_DK), lambda b, h, s: (0, h)),
            _resident(level.shape), _resident(sign.shape),
        ],
        out_specs=blk,
        out_shape=jax.ShapeDtypeStruct((m, HG_HEADS * HG_DK), BF16),
        scratch_shapes=[pltpu.VMEM((HG_DK, HG_DK), F32)],
        compiler_params=_params(("parallel", "parallel", "arbitrary")),
        name="hgrn2",
    )(q_act, f_logit, inp, lb_logits.reshape(nl, HG_HEADS * HG_DK), jnp.asarray(level),
      jnp.asarray(sign))


LOG2E = 1.4426950408889634
FOX_FOLD = 8
FOX_SUM_ROWS = 16


def _decay_features_t(heads, h):
    rows = heads.shape[1]
    sub = lax.broadcasted_iota(jnp.int32, (SUBLANES, rows), 0)
    c_row = jnp.sum(jnp.where(sub == h, heads, 0.0), axis=0, keepdims=True)
    hi, mid, lo = (jnp.broadcast_to(part.astype(F32), (SUBLANES, rows))
                   for part in _split3(c_row * LOG2E))

    def place(base, sign):
        ones = (sub >= 3 - base) & (sub < 6 - base)
        feat = jnp.where(sub == base, sign * hi,
                         jnp.where(sub == base + 1, sign * mid,
                                   jnp.where(sub == base + 2, sign * lo,
                                             jnp.where(ones, 1.0, 0.0))))
        pad = jnp.zeros((LANES - SUBLANES, rows), F32)
        return jnp.concatenate([feat, pad], axis=0).astype(BF16)

    return place(3, 1.0), place(0, -1.0)


def _fox_kernel(q_ref, k_ref, v_ref, c_ref, o_ref, kx_ref, qft_ref, vt_ref, qxt_ref,
                s0_ref, s1_ref, x0_ref, x1_ref, m_ref, acc_ref, *, tk):
    h = pl.program_id(1)
    qi = pl.program_id(2)
    tq = q_ref.shape[1]
    nk = kx_ref.shape[0]

    @pl.when(qi == 0)
    def _():
        ones_row = (lax.broadcasted_iota(jnp.int32, (FOX_SUM_ROWS, tk), 0) == 0)
        ones_row = ones_row.astype(F32).astype(BF16)
        for kb in range(nk):
            rows = slice(kb * tk, (kb + 1) * tk)
            q_feat_t, k_feat_t = _decay_features_t(c_ref[:, rows], h)
            kx_ref[kb, :, 0:FOX_DH] = k_ref[rows, :]
            kx_ref[kb, :, FOX_DH:2 * FOX_DH] = k_feat_t.T
            qft_ref[kb] = q_feat_t
            vt_ref[kb, 0:FOX_DH, :] = v_ref[:, rows]
            vt_ref[kb, FOX_DH:FOX_DH + FOX_SUM_ROWS, :] = ones_row

    ratio = tq // tk
    n_full = ratio * qi
    qxt_ref[0:FOX_DH, :] = q_ref[...]
    for j in range(ratio):
        qxt_ref[FOX_DH:2 * FOX_DH, j * tk:(j + 1) * tk] = qft_ref[n_full + j]
    m_ref[...] = jnp.full_like(m_ref, -jnp.inf)
    acc_ref[...] = jnp.zeros_like(acc_ref)

    def fold(x, op):
        part = op(x.reshape(FOX_FOLD, tk // FOX_FOLD, x.shape[1]), axis=0)
        return op(part, axis=0, keepdims=True)

    def scores(kb, s_ref, smax_ref):
        st = _dot(kx_ref[kb], qxt_ref[...])
        s_ref[...] = st
        smax_ref[...] = fold(st, jnp.max)

    def causal(st):
        key = lax.broadcasted_iota(jnp.int32, st.shape, 0)
        qry = lax.broadcasted_iota(jnp.int32, st.shape, 1)
        return jnp.where(key <= qry, st, -jnp.inf)

    def softmax_step(kb, st, blk_max, cols):
        m_prev = m_ref[:, cols]
        m_new = jnp.maximum(m_prev, blk_max)
        alpha = jnp.exp2(m_prev - m_new)
        p = jnp.exp2(st - m_new)
        m_ref[:, cols] = m_new
        acc_ref[:, cols] = alpha * acc_ref[:, cols] + _dot(vt_ref[kb], p.astype(BF16))

    def update(kb, s_ref, smax_ref):
        softmax_step(kb, s_ref[...], smax_ref[...], slice(None))

    def update_diagonal(kb, s_ref):
        st = s_ref[...]
        head = causal(st[:, 0:tk])
        st = head if tq == tk else jnp.concatenate([head, st[:, tk:]], axis=1)
        softmax_step(kb, st, fold(st, jnp.max), slice(None))

    scores(0, s0_ref, x0_ref)

    def pair(kb):
        scores(kb + 1, s1_ref, x1_ref)
        update(kb, s0_ref, x0_ref)
        scores(kb + 2, s0_ref, x0_ref)
        update(kb + 1, s1_ref, x1_ref)

    def body4(j, carry):
        pair(4 * j)
        pair(4 * j + 2)
        return carry

    def body2(j, carry):
        pair(2 * j)
        return carry

    n_pairs = n_full // 2
    lax.fori_loop(0, n_pairs // 2, body4, 0)
    lax.fori_loop(2 * (n_pairs // 2), n_pairs, body2, 0)

    if ratio == 2:
        upper = slice(tk, tq)
        tile = causal(_dot(kx_ref[n_full + 1], qxt_ref[:, upper]))
        update_diagonal(n_full, s0_ref)
        softmax_step(n_full + 1, tile, fold(tile, jnp.max), upper)
    else:
        @pl.when(qi % 2 == 1)
        def _():
            scores(qi, s1_ref, x1_ref)
            update(qi - 1, s0_ref, x0_ref)
            update_diagonal(qi, s1_ref)

        @pl.when(qi % 2 == 0)
        def _():
            update_diagonal(qi, s0_ref)

    out = acc_ref[0:FOX_DH, :] / acc_ref[FOX_DH:FOX_DH + 1, :]
    o_ref[...] = out.astype(o_ref.dtype).T


def _fox(q_t, k, v_t, c, batch, seq, tq=1024, tk=512):
    assert tq in (tk, 2 * tk)
    m = k.shape[0]
    nq = seq // tq
    nk = seq // tk
    return pl.pallas_call(
        functools.partial(_fox_kernel, tk=tk),
        grid=(batch, FOX_HEADS, nq),
        in_specs=[
            pl.BlockSpec((FOX_DH, tq), lambda b, h, i: (h, b * nq + i)),
            pl.BlockSpec((seq, FOX_DH), lambda b, h, i: (b, h)),
            pl.BlockSpec((FOX_DH, seq), lambda b, h, i: (h, b)),
            pl.BlockSpec((None, FOX_HEADS, seq), lambda b, h, i: (b, 0, 0)),
        ],
        out_specs=pl.BlockSpec((tq, FOX_DH), lambda b, h, i: (b * nq + i, h)),
        out_shape=jax.ShapeDtypeStruct((m, FOX_HEADS * FOX_DH), BF16),
        scratch_shapes=[
            pltpu.VMEM((nk, tk, 2 * FOX_DH), BF16),
            pltpu.VMEM((nk, FOX_DH, tk), BF16),
            pltpu.VMEM((nk, FOX_DH + FOX_SUM_ROWS, tk), BF16),
            pltpu.VMEM((2 * FOX_DH, tq), BF16),
            pltpu.VMEM((tk, tq), F32),
            pltpu.VMEM((tk, tq), F32),
            pltpu.VMEM((1, tq), F32),
            pltpu.VMEM((1, tq), F32),
            pltpu.VMEM((1, tq), F32),
            pltpu.VMEM((FOX_DH + FOX_SUM_ROWS, tq), F32),
        ],
        compiler_params=_params(("parallel", "parallel", "arbitrary")),
        name="fox_attention",
    )(q_t, k, v_t, c)


def _merge_body(x, oa_ref, ga_ref, ob_ref, g0_ref, g1_ref, hg_ref, wa_ref, wb_ref, wo_ref, post_ref):
    oa = _rms(oa_ref[...].astype(F32), hg_ref[...]) * ga_ref[...].astype(F32)
    ya = _dot(oa.astype(BF16), wa_ref[...])
    yb = _dot(ob_ref[...], wb_ref[...])
    y = g0_ref[...].astype(F32) * ya + g1_ref[...].astype(F32) * yb
    z = _dot(y.astype(BF16), wo_ref[...])
    return x + _rms(z, post_ref[...])


def _memattn_body(x, pre_ref, kv_ref, wq_ref, wo_ref, post_ref):
    h = _rms(x, pre_ref[...]).astype(BF16)
    q = _dot(h, wq_ref[...]).astype(BF16)
    heads = []
    for hd in range(MEM_HEADS):
        lo = hd * MEM_DH
        kh = kv_ref[:, lo:lo + MEM_DH]
        vh = kv_ref[:, D_MODEL + lo:D_MODEL + lo + MEM_DH]
        s = lax.dot_general(q[:, lo:lo + MEM_DH], kh, NT_DIMS, preferred_element_type=F32)
        p = jnp.exp(s - jnp.max(s, axis=1, keepdims=True))
        inv = 1.0 / jnp.sum(p, axis=1, keepdims=True)
        heads.append((_dot(p.astype(BF16), vh) * inv).astype(BF16))
    o = jnp.concatenate(heads, axis=1)
    return x + _rms(_dot(o, wo_ref[...]), post_ref[...])


def _tail_kernel(x_ref, oa_ref, ga_ref, ob_ref, g0_ref, g1_ref, kv_ref, hg_ref, wa_ref, wb_ref,
                 wo_ref, mix_post_ref, mem_pre_ref, wq_ref, wmo_ref, mem_post_ref,
                 ffn_pre_ref, w_in_ref, w_down_ref, ffn_post_ref, o_ref, acc_ref):
    x = _merge_body(x_ref[...], oa_ref, ga_ref, ob_ref, g0_ref, g1_ref, hg_ref, wa_ref, wb_ref,
                    wo_ref, mix_post_ref)
    x = _memattn_body(x, mem_pre_ref, kv_ref, wq_ref, wmo_ref, mem_post_ref)
    o_ref[...] = _ffn_body(x, ffn_pre_ref, w_in_ref, w_down_ref, ffn_post_ref, acc_ref)


def _tail(x, o_a, g_act, o_b, gates, kv, hg_norm_g, w_a, w_b, w_o, mix_post_g, mem_pre_g, w_q,
          w_mo, mem_post_g, ffn_pre_g, w_in, w_down, ffn_post_g, batch, tm=512):
    m, d = x.shape
    ns = m // batch // tm
    row = lambda c: pl.BlockSpec((tm, d), lambda b, s: (b * ns + s, c))
    vec, mat = _resident((1, d)), _resident((d, d))
    r1 = lambda g: g.reshape(1, d)
    return pl.pallas_call(
        _tail_kernel,
        grid=(batch, ns),
        in_specs=[row(0), row(0), row(0), row(0), row(0), row(1),
                  pl.BlockSpec((MEM_LEN, 2 * d), lambda b, s: (b, 0)),
                  vec, mat, mat, mat, vec, vec, mat, mat, vec,
                  vec, _resident(w_in.shape), _resident(w_down.shape), vec],
        out_specs=row(0),
        out_shape=jax.ShapeDtypeStruct((m, d), F32),
        scratch_shapes=[pltpu.VMEM((tm, d), F32)],
        compiler_params=_params(("parallel", "parallel")),
        name="merge_memattn_ffn",
    )(x, o_a, g_act, o_b, gates, gates, kv, r1(hg_norm_g), w_a, w_b, w_o, r1(mix_post_g),
      r1(mem_pre_g), w_q, w_mo, r1(mem_post_g), r1(ffn_pre_g), w_in, w_down, r1(ffn_post_g))


def kernel(x, mem, ffn1_pre_g, ffn1_w_in, ffn1_w_down, ffn1_post_g, mix_pre_g, w_in, hg_lb_logits, hg_norm_g, fox_f_bias, w_branch_a, w_branch_b, b_gate, w_out, mix_post_g, mem_pre_g, mem_kv_g, w_mq, w_mkv, w_mo, mem_post_g, ffn2_pre_g, ffn2_w_in, ffn2_w_down, ffn2_post_g):
    batch, seq, d = x.shape
    depth = ffn1_w_in.shape[0]
    xf = x.reshape(batch * seq, d)
    memf = mem.reshape(batch * MEM_LEN, d)
    kw = HG_HEADS * HG_DK
    off_b = 4 * kw
    off_f = off_b + 3 * FOX_HEADS * FOX_DH
    off_g = off_f + FOX_HEADS
    for l in range(depth):
        xf = _ffn(xf, ffn1_pre_g[l], ffn1_w_in[l].astype(BF16), ffn1_w_down[l].astype(BF16),
                  ffn1_post_g[l])

        wt = jnp.swapaxes(w_in[l], 0, 1)
        fox_w = FOX_HEADS * FOX_DH
        weights = (wt[:off_f].astype(BF16),
                   jnp.pad(wt[off_f:off_g], ((0, LANES - FOX_HEADS), (0, 0))).astype(BF16),
                   wt[off_g:].astype(BF16))
        q_scale = LOG2E / math.sqrt(FOX_DH)
        groups = ((0, 0, "silu", 1.0), (0, kw, None, 1.0), (0, 2 * kw, None, 1.0),
                  (0, 3 * kw, "silu", 1.0), (0, off_b, "transpose", q_scale),
                  (0, off_b + fox_w, None, 1.0), (0, off_b + 2 * fox_w, "transpose", 1.0),
                  (1, 0, "transpose", 1.0), (2, 0, "sigmoid_bias", 1.0))
        q_act, f_logit, inp, g_act, q_t, k_b, v_t, fb_t, gates = _inproj(
            xf, mix_pre_g[l], weights, b_gate[l].reshape(1, 2 * d), groups,
            (kw, kw, kw, kw, fox_w, fox_w, fox_w, LANES, 2 * d),
            (BF16, F32, BF16, BF16, BF16, BF16, BF16, F32, BF16))

        c = _fcum(fb_t, fox_f_bias[l].reshape(FOX_HEADS, 1), batch, seq)

        o_a = _hgrn2(q_act, f_logit, inp, hg_lb_logits, l, batch)
        o_b = _fox(q_t, k_b, v_t, c, batch, seq)

        kv = _norm_matmul(memf, mem_kv_g[l], w_mkv[l].astype(BF16), BF16, MEM_LEN, 1024)
        xf = _tail(xf, o_a, g_act, o_b, gates, kv, hg_norm_g[l],
                   w_branch_a[l].astype(BF16), w_branch_b[l].astype(BF16), w_out[l].astype(BF16),
                   mix_post_g[l], mem_pre_g[l],
                   (w_mq[l] * (1.0 / math.sqrt(MEM_DH))).astype(BF16), w_mo[l].astype(BF16),
                   mem_post_g[l], ffn2_pre_g[l], ffn2_w_in[l].astype(BF16),
                   ffn2_w_down[l].astype(BF16), ffn2_post_g[l], batch)
    return xf.reshape(batch, seq, d)
```

```python
import functools
import math

import jax
import jax.numpy as jnp
import numpy as np
from jax import lax
from jax.experimental import pallas as pl
from jax.experimental.pallas import tpu as pltpu

F32 = jnp.float32
BF16 = jnp.bfloat16

D_MODEL = 1024
HG_HEADS = 8
HG_DK = 128
FOX_HEADS = 8
FOX_DH = 128
MEM_LEN = 256
MEM_HEADS = 4
MEM_DH = D_MODEL // MEM_HEADS
D_FF = 2816
EPS = 1e-6
LANES = 128

VMEM_LIMIT = 56 * 1024 * 1024

NT_DIMS = (((1,), (1,)), ((), ()))
TN_DIMS = (((0,), (0,)), ((), ()))


def _params(sem):
    return pltpu.CompilerParams(dimension_semantics=sem, vmem_limit_bytes=VMEM_LIMIT)


def _rms(x, g):
    ms = jnp.mean(x * x, axis=-1, keepdims=True)
    return x * lax.rsqrt(ms + EPS) * g


def _sigmoid(x):
    return 0.5 * jnp.tanh(0.5 * x) + 0.5


def _dot(a, b):
    return jnp.dot(a, b, preferred_element_type=F32)


def _norm_matmul_kernel(x_ref, g_ref, w_ref, o_ref, h_ref):
    @pl.when(pl.program_id(1) == 0)
    def _():
        h_ref[...] = _rms(x_ref[...], g_ref[...]).astype(BF16)

    o_ref[...] = _dot(h_ref[...], w_ref[...]).astype(o_ref.dtype)


def _norm_matmul(x, g, w, out_dtype, tm, tn):
    m, d = x.shape
    n = w.shape[1]
    return pl.pallas_call(
        _norm_matmul_kernel,
        grid=(m // tm, n // tn),
        in_specs=[
            pl.BlockSpec((tm, d), lambda i, j: (i, 0)),
            pl.BlockSpec((1, d), lambda i, j: (0, 0)),
            pl.BlockSpec((d, tn), lambda i, j: (0, j)),
        ],
        out_specs=pl.BlockSpec((tm, tn), lambda i, j: (i, j)),
        out_shape=jax.ShapeDtypeStruct((m, n), out_dtype),
        scratch_shapes=[pltpu.VMEM((tm, d), BF16)],
        compiler_params=_params(("parallel", "arbitrary")),
        name="norm_matmul",
    )(x, g.reshape(1, d), w)


def _resident(shape):
    return pl.BlockSpec(shape, lambda *_: (0,) * len(shape), pipeline_mode=pl.Buffered(1))


def _inproj_kernel(x_ref, g_ref, *refs, groups, tn):
    n_w = 1 + max(grp[0] for grp in groups)
    wt_refs, bias_ref, out_refs = refs[:n_w], refs[n_w], refs[n_w + 1:]
    h = _rms(x_ref[...], g_ref[...]).astype(BF16)
    for o_ref, (wi, row0, act, scale) in zip(out_refs, groups):
        columns_first = act in ("transpose", "transpose_result")
        n = o_ref.shape[0] if columns_first else o_ref.shape[1]
        for j in range(0, n, tn):
            width = min(tn, n - j)
            w_rows = wt_refs[wi][row0 + j:row0 + j + width, :]
            if act == "transpose":
                y = lax.dot_general(w_rows, h, NT_DIMS, preferred_element_type=F32)
                o_ref[j:j + width, :] = (y * scale).astype(o_ref.dtype)
                continue
            y = lax.dot_general(h, w_rows, NT_DIMS, preferred_element_type=F32)
            if act == "transpose_result":
                o_ref[j:j + width, :] = y.T.astype(o_ref.dtype)
                continue
            if act == "silu":
                y = y * _sigmoid(y)
            elif act == "sigmoid_bias":
                y = _sigmoid(y + bias_ref[:, j:j + width])
            o_ref[:, j:j + width] = y.astype(o_ref.dtype)


def _inproj(x, g, weights, bias, groups, widths, dtypes, tm=512, tn=512):
    m, d = x.shape
    transposed = [grp[2] in ("transpose", "transpose_result") for grp in groups]
    return pl.pallas_call(
        functools.partial(_inproj_kernel, groups=groups, tn=tn),
        grid=(m // tm,),
        in_specs=[pl.BlockSpec((tm, d), lambda i: (i, 0)), _resident((1, d))]
        + [_resident(w.shape) for w in weights] + [_resident(bias.shape)],
        out_specs=[pl.BlockSpec((n, tm), lambda i: (0, i)) if t else
                   pl.BlockSpec((tm, n), lambda i: (i, 0)) for n, t in zip(widths, transposed)],
        out_shape=[jax.ShapeDtypeStruct((n, m) if t else (m, n), dt)
                   for n, dt, t in zip(widths, dtypes, transposed)],
        compiler_params=_params(("parallel",)),
        name="input_projection",
    )(x, g.reshape(1, d), *weights, bias)


FFN_CHUNK = 256


def _ffn_body(x, pre_ref, w_in_ref, w_down_ref, post_ref, acc_ref):
    h = _rms(x, pre_ref[...]).astype(BF16)
    for f in range(D_FF // FFN_CHUNK):
        cols = slice(f * FFN_CHUNK, (f + 1) * FFN_CHUNK)
        gate = _dot(h, w_in_ref[:, cols])
        up = _dot(h, w_in_ref[:, D_FF + f * FFN_CHUNK:D_FF + (f + 1) * FFN_CHUNK])
        act = (gate * _sigmoid(gate) * up).astype(BF16)
        part = _dot(act, w_down_ref[cols, :])
        if f == 0:
            acc_ref[...] = part
        else:
            acc_ref[...] += part
    return x + 0.5 * _rms(acc_ref[...], post_ref[...])


def _ffn_kernel(x_ref, pre_ref, w_in_ref, w_down_ref, post_ref, o_ref, acc_ref):
    o_ref[...] = _ffn_body(x_ref[...], pre_ref, w_in_ref, w_down_ref, post_ref, acc_ref)


def _ffn(x, pre_g, w_in, w_down, post_g, tm=1024):
    m, d = x.shape
    return pl.pallas_call(
        _ffn_kernel,
        grid=(m // tm,),
        in_specs=[
            pl.BlockSpec((tm, d), lambda i: (i, 0)),
            _resident((1, d)),
            _resident(w_in.shape),
            _resident(w_down.shape),
            _resident((1, d)),
        ],
        out_specs=pl.BlockSpec((tm, d), lambda i: (i, 0)),
        out_shape=jax.ShapeDtypeStruct((m, d), F32),
        scratch_shapes=[pltpu.VMEM((tm, d), F32)],
        compiler_params=_params(("parallel",)),
        name="ffn",
    )(x, pre_g.reshape(1, d), w_in, w_down, post_g.reshape(1, d))


def _split3(x):
    hi = x.astype(BF16)
    r = x - hi.astype(F32)
    mid = r.astype(BF16)
    lo = (r - mid.astype(F32)).astype(BF16)
    return hi, mid, lo


def _fcum_kernel(fb_ref, bias_ref, c_ref, carry_ref):
    @pl.when(pl.program_id(1) == 0)
    def _():
        carry_ref[...] = jnp.zeros_like(carry_ref)

    z = fb_ref[0:FOX_HEADS, :] + bias_ref[...]
    ls = jnp.minimum(z, 0.0) - jnp.log(1.0 + jnp.exp(-jnp.abs(z)))
    tb = z.shape[1]
    row = lax.broadcasted_iota(jnp.int32, (tb, tb), 0)
    col = lax.broadcasted_iota(jnp.int32, (tb, tb), 1)
    triu = (row <= col).astype(BF16)
    stacked = jnp.concatenate([p.astype(F32) for p in _split3(ls)], axis=0).astype(BF16)
    parts = _dot(stacked, triu)
    c = (parts[0:FOX_HEADS] + parts[FOX_HEADS:2 * FOX_HEADS] + parts[2 * FOX_HEADS:]
         + carry_ref[...])
    c_ref[...] = c
    carry_ref[...] = jnp.broadcast_to(c[:, tb - 1:tb], carry_ref.shape)


def _fcum(fb_t, bias, batch, seq, tb=512):
    ns = seq // tb
    return pl.pallas_call(
        _fcum_kernel,
        grid=(batch, ns),
        in_specs=[
            pl.BlockSpec((LANES, tb), lambda b, s: (0, b * ns + s)),
            pl.BlockSpec((FOX_HEADS, 1), lambda b, s: (0, 0)),
        ],
        out_specs=pl.BlockSpec((None, FOX_HEADS, tb), lambda b, s: (b, 0, s)),
        out_shape=jax.ShapeDtypeStruct((batch, FOX_HEADS, seq), F32),
        scratch_shapes=[pltpu.VMEM((FOX_HEADS, tb), F32)],
        compiler_params=_params(("parallel", "arbitrary")),
        name="fox_decay_cumsum",
    )(fb_t, bias)


def _hgrn_level_table(c):
    t = np.arange(c)[:, None]
    s = np.arange(c)[None, :]
    level = np.full((c, c), -2, np.int32)
    level[t == s] = -1
    for l in range(int(math.log2(c))):
        h = c >> (l + 1)
        mid = (t // (2 * h)) * (2 * h) + h
        sel = (t // (2 * h) == s // (2 * h)) & (t >= mid) & (s < mid)
        level[sel & (level == -2)] = l
    return level


SUBLANES = 8


def _prefix_sum_rows(x, row):
    k = 1
    while k < x.shape[0]:
        x = x + jnp.where(row >= k, pltpu.roll(x, k, axis=0), 0.0)
        k *= 2
    return x


def _hgrn_sign_table(c):
    t = np.arange(c)[:, None]
    halves = [c >> (l + 1) for l in range(int(math.log2(c)))]
    sign = [np.where((t % (2 * h)) >= h, 1.0, -1.0) * np.ones((1, HG_DK)) for h in halves]
    return np.stack(sign).astype(np.float32)


def _midpoint_rows(b, h):
    c, w = b.shape
    if h >= SUBLANES:
        pieces = [jnp.broadcast_to(b[j * 2 * h + h - 1:j * 2 * h + h, :], (2 * h, w))
                  for j in range(c // (2 * h))]
        return pieces[0] if len(pieces) == 1 else jnp.concatenate(pieces, axis=0)
    groups = c // SUBLANES
    b3 = b.reshape(groups, SUBLANES, w)
    sub = lax.broadcasted_iota(jnp.int32, (1, SUBLANES, w), 1)
    r3 = None
    for j in range(SUBLANES // (2 * h)):
        src = j * 2 * h + h - 1
        piece = jnp.broadcast_to(b3[:, src:src + 1, :], (groups, SUBLANES, w))
        r3 = piece if r3 is None else jnp.where(sub >= j * 2 * h, piece, r3)
    return r3.reshape(c, w)


def _hgrn_kernel(q_ref, f_ref, i_ref, lb_ref, lvl_ref, sign_ref, o_ref, st_ref, *, chunk, layer):
    @pl.when(pl.program_id(2) == 0)
    def _():
        st_ref[...] = jnp.zeros_like(st_ref)

    c = chunk
    n_lvl = int(math.log2(c))
    ts = q_ref.shape[0]
    logits = lb_ref[...]
    e = jnp.exp(logits - jnp.max(logits, axis=0, keepdims=True))
    lb = jnp.sum(e[0:layer + 1, :], axis=0, keepdims=True) / jnp.sum(e, axis=0, keepdims=True)
    lvl = lvl_ref[...]
    st = st_ref[...]

    row = lax.broadcasted_iota(jnp.int32, (c, HG_DK), 0)
    ks, bs = [], []
    for ci in range(ts // c):
        f = lb + (1.0 - lb) * _sigmoid(f_ref[pl.ds(ci * c, c), :])
        bs.append(_prefix_sum_rows(jnp.log2(f), row))
        ks.append(1.0 - f)

    chunks = range(ts // c)
    q16s = [q_ref[pl.ds(ci * c, c), :] for ci in chunks]
    vs = [i_ref[pl.ds(ci * c, c), :] for ci in chunks]
    k16s = [k.astype(BF16) for k in ks]

    a = [jnp.where(lvl == -1, lax.dot_general(q16s[ci], k16s[ci], NT_DIMS,
                                              preferred_element_type=F32), 0.0) for ci in chunks]
    for l in range(n_lvl):
        h = c >> (l + 1)
        for ci in chunks:
            d_l = (bs[ci] - _midpoint_rows(bs[ci], h)) * sign_ref[l]
            e_l = jnp.exp2(d_l).astype(BF16)
            p = lax.dot_general(q16s[ci] * e_l, k16s[ci] * e_l, NT_DIMS,
                                preferred_element_type=F32)
            a[ci] = jnp.where(lvl == l, p, a[ci])

    o_intra = [_dot(a[ci].astype(BF16), vs[ci]) for ci in chunks]
    upds = [lax.dot_general(vs[ci], (ks[ci] * jnp.exp2(bs[ci][c - 1:c, :] - bs[ci])).astype(BF16),
                            TN_DIMS, preferred_element_type=F32) for ci in chunks]

    for ci in chunks:
        o = o_intra[ci] + lax.dot_general(q16s[ci] * jnp.exp2(bs[ci]).astype(BF16),
                                          st.astype(BF16), NT_DIMS, preferred_element_type=F32)
        o_ref[pl.ds(ci * c, c), :] = o.astype(o_ref.dtype)
        st = st * jnp.exp2(bs[ci][c - 1:c, :]) + upds[ci]

    st_ref[...] = st


def _hgrn2(q_act, f_logit, inp, lb_logits, layer, batch, ts=2048, chunk=128):
    m = q_act.shape[0]
    ns = m // batch // ts
    level = _hgrn_level_table(chunk)
    sign = _hgrn_sign_table(chunk)
    nl = lb_logits.shape[0]
    blk = pl.BlockSpec((ts, HG_DK), lambda b, h, s: (b * ns + s, h))
    return pl.pallas_call(
        functools.partial(_hgrn_kernel, chunk=chunk, layer=layer),
        grid=(batch, HG_HEADS, ns),
        in_specs=[
            blk, blk, blk,
            pl.BlockSpec((nl, HG_DK), lambda b, h, s: (0, h)),
            _resident(level.shape), _resident(sign.shape),
        ],
        out_specs=blk,
        out_shape=jax.ShapeDtypeStruct((m, HG_HEADS * HG_DK), BF16),
        scratch_shapes=[pltpu.VMEM((HG_DK, HG_DK), F32)],
        compiler_params=_params(("parallel", "parallel", "arbitrary")),
        name="hgrn2",
    )(q_act, f_logit, inp, lb_logits.reshape(nl, HG_HEADS * HG_DK), jnp.asarray(level),
      jnp.asarray(sign))


LOG2E = 1.4426950408889634
FOX_FOLD = 8
FOX_SUM_ROWS = 16


def _decay_features_t(heads, h):
    rows = heads.shape[1]
    sub = lax.broadcasted_iota(jnp.int32, (SUBLANES, rows), 0)
    c_row = jnp.sum(jnp.where(sub == h, heads, 0.0), axis=0, keepdims=True)
    hi, mid, lo = (jnp.broadcast_to(part.astype(F32), (SUBLANES, rows))
                   for part in _split3(c_row * LOG2E))

    def place(base, sign):
        ones = (sub >= 3 - base) & (sub < 6 - base)
        feat = jnp.where(sub == base, sign * hi,
                         jnp.where(sub == base + 1, sign * mid,
                                   jnp.where(sub == base + 2, sign * lo,
                                             jnp.where(ones, 1.0, 0.0))))
        pad = jnp.zeros((LANES - SUBLANES, rows), F32)
        return jnp.concatenate([feat, pad], axis=0).astype(BF16)

    return place(3, 1.0), place(0, -1.0)


def _fox_kernel(q_ref, k_ref, v_ref, c_ref, o_ref, kx_ref, qft_ref, vt_ref, qxt_ref,
                s0_ref, s1_ref, x0_ref, x1_ref, m_ref, acc_ref, *, tk):
    h = pl.program_id(1)
    qi = pl.program_id(2)
    tq = q_ref.shape[1]
    nk = kx_ref.shape[0]

    @pl.when(qi == 0)
    def _():
        ones_row = (lax.broadcasted_iota(jnp.int32, (FOX_SUM_ROWS, tk), 0) == 0)
        ones_row = ones_row.astype(F32).astype(BF16)
        for kb in range(nk):
            rows = slice(kb * tk, (kb + 1) * tk)
            q_feat_t, k_feat_t = _decay_features_t(c_ref[:, rows], h)
            kx_ref[kb, :, 0:FOX_DH] = k_ref[rows, :]
            kx_ref[kb, :, FOX_DH:2 * FOX_DH] = k_feat_t.T
            qft_ref[kb] = q_feat_t
            vt_ref[kb, 0:FOX_DH, :] = v_ref[:, rows]
            vt_ref[kb, FOX_DH:FOX_DH + FOX_SUM_ROWS, :] = ones_row

    ratio = tq // tk
    n_full = ratio * qi
    qxt_ref[0:FOX_DH, :] = q_ref[...]
    for j in range(ratio):
        qxt_ref[FOX_DH:2 * FOX_DH, j * tk:(j + 1) * tk] = qft_ref[n_full + j]
    m_ref[...] = jnp.full_like(m_ref, -jnp.inf)
    acc_ref[...] = jnp.zeros_like(acc_ref)

    def fold(x, op):
        part = op(x.reshape(FOX_FOLD, tk // FOX_FOLD, x.shape[1]), axis=0)
        return op(part, axis=0, keepdims=True)

    def scores(kb, s_ref, smax_ref):
        st = _dot(kx_ref[kb], qxt_ref[...])
        s_ref[...] = st
        smax_ref[...] = fold(st, jnp.max)

    def causal(st):
        key = lax.broadcasted_iota(jnp.int32, st.shape, 0)
        qry = lax.broadcasted_iota(jnp.int32, st.shape, 1)
        return jnp.where(key <= qry, st, -jnp.inf)

    def softmax_step(kb, st, blk_max, cols):
        m_prev = m_ref[:, cols]
        m_new = jnp.maximum(m_prev, blk_max)
        alpha = jnp.exp2(m_prev - m_new)
        p = jnp.exp2(st - m_new)
        m_ref[:, cols] = m_new
        acc_ref[:, cols] = alpha * acc_ref[:, cols] + _dot(vt_ref[kb], p.astype(BF16))

    def update(kb, s_ref, smax_ref):
        softmax_step(kb, s_ref[...], smax_ref[...], slice(None))

    def update_diagonal(kb, s_ref):
        st = s_ref[...]
        head = causal(st[:, 0:tk])
        st = head if tq == tk else jnp.concatenate([head, st[:, tk:]], axis=1)
        softmax_step(kb, st, fold(st, jnp.max), slice(None))

    scores(0, s0_ref, x0_ref)

    def pair(kb):
        scores(kb + 1, s1_ref, x1_ref)
        update(kb, s0_ref, x0_ref)
        scores(kb + 2, s0_ref, x0_ref)
        update(kb + 1, s1_ref, x1_ref)

    def body4(j, carry):
        pair(4 * j)
        pair(4 * j + 2)
        return carry

    def body2(j, carry):
        pair(2 * j)
        return carry

    n_pairs = n_full // 2
    lax.fori_loop(0, n_pairs // 2, body4, 0)
    lax.fori_loop(2 * (n_pairs // 2), n_pairs, body2, 0)

    if ratio == 2:
        upper = slice(tk, tq)
        tile = causal(_dot(kx_ref[n_full + 1], qxt_ref[:, upper]))
        update_diagonal(n_full, s0_ref)
        softmax_step(n_full + 1, tile, fold(tile, jnp.max), upper)
    else:
        @pl.when(qi % 2 == 1)
        def _():
            scores(qi, s1_ref, x1_ref)
            update(qi - 1, s0_ref, x0_ref)
            update_diagonal(qi, s1_ref)

        @pl.when(qi % 2 == 0)
        def _():
            update_diagonal(qi, s0_ref)

    out = acc_ref[0:FOX_DH, :] / acc_ref[FOX_DH:FOX_DH + 1, :]
    o_ref[...] = out.astype(o_ref.dtype).T


def _fox(q_t, k, v_t, c, batch, seq, tq=1024, tk=512):
    assert tq in (tk, 2 * tk)
    m = k.shape[0]
    nq = seq // tq
    nk = seq // tk
    return pl.pallas_call(
        functools.partial(_fox_kernel, tk=tk),
        grid=(batch, FOX_HEADS, nq),
        in_specs=[
            pl.BlockSpec((FOX_DH, tq), lambda b, h, i: (h, b * nq + i)),
            pl.BlockSpec((seq, FOX_DH), lambda b, h, i: (b, h)),
            pl.BlockSpec((FOX_DH, seq), lambda b, h, i: (h, b)),
            pl.BlockSpec((None, FOX_HEADS, seq), lambda b, h, i: (b, 0, 0)),
        ],
        out_specs=pl.BlockSpec((tq, FOX_DH), lambda b, h, i: (b * nq + i, h)),
        out_shape=jax.ShapeDtypeStruct((m, FOX_HEADS * FOX_DH), BF16),
        scratch_shapes=[
            pltpu.VMEM((nk, tk, 2 * FOX_DH), BF16),
            pltpu.VMEM((nk, FOX_DH, tk), BF16),
            pltpu.VMEM((nk, FOX_DH + FOX_SUM_ROWS, tk), BF16),
            pltpu.VMEM((2 * FOX_DH, tq), BF16),
            pltpu.VMEM((tk, tq), F32),
            pltpu.VMEM((tk, tq), F32),
            pltpu.VMEM((1, tq), F32),
            pltpu.VMEM((1, tq), F32),
            pltpu.VMEM((1, tq), F32),
            pltpu.VMEM((FOX_DH + FOX_SUM_ROWS, tq), F32),
        ],
        compiler_params=_params(("parallel", "parallel", "arbitrary")),
        name="fox_attention",
    )(q_t, k, v_t, c)


def _merge_body(x, oa_ref, ga_ref, ob_ref, g0_ref, g1_ref, hg_ref, wa_ref, wb_ref, wo_ref, post_ref):
    oa = _rms(oa_ref[...].astype(F32), hg_ref[...]) * ga_ref[...].astype(F32)
    ya = _dot(oa.astype(BF16), wa_ref[...])
    yb = _dot(ob_ref[...], wb_ref[...])
    y = g0_ref[...].astype(F32) * ya + g1_ref[...].astype(F32) * yb
    z = _dot(y.astype(BF16), wo_ref[...])
    return x + _rms(z, post_ref[...])


def _memattn_body(x, pre_ref, kv_ref, wq_ref, wo_ref, post_ref):
    h = _rms(x, pre_ref[...]).astype(BF16)
    q = _dot(h, wq_ref[...]).astype(BF16)
    heads = []
    for hd in range(MEM_HEADS):
        lo = hd * MEM_DH
        kh = kv_ref[:, lo:lo + MEM_DH]
        vh = kv_ref[:, D_MODEL + lo:D_MODEL + lo + MEM_DH]
        s = lax.dot_general(q[:, lo:lo + MEM_DH], kh, NT_DIMS, preferred_element_type=F32)
        p = jnp.exp(s - jnp.max(s, axis=1, keepdims=True))
        inv = 1.0 / jnp.sum(p, axis=1, keepdims=True)
        heads.append((_dot(p.astype(BF16), vh) * inv).astype(BF16))
    o = jnp.concatenate(heads, axis=1)
    return x + _rms(_dot(o, wo_ref[...]), post_ref[...])


def _tail_kernel(x_ref, oa_ref, ga_ref, ob_ref, g0_ref, g1_ref, kv_ref, hg_ref, wa_ref, wb_ref,
                 wo_ref, mix_post_ref, mem_pre_ref, wq_ref, wmo_ref, mem_post_ref,
                 ffn_pre_ref, w_in_ref, w_down_ref, ffn_post_ref, o_ref, acc_ref):
    x = _merge_body(x_ref[...], oa_ref, ga_ref, ob_ref, g0_ref, g1_ref, hg_ref, wa_ref, wb_ref,
                    wo_ref, mix_post_ref)
    x = _memattn_body(x, mem_pre_ref, kv_ref, wq_ref, wmo_ref, mem_post_ref)
    o_ref[...] = _ffn_body(x, ffn_pre_ref, w_in_ref, w_down_ref, ffn_post_ref, acc_ref)


def _tail(x, o_a, g_act, o_b, gates, kv, hg_norm_g, w_a, w_b, w_o, mix_post_g, mem_pre_g, w_q,
          w_mo, mem_post_g, ffn_pre_g, w_in, w_down, ffn_post_g, batch, tm=512):
    m, d = x.shape
    ns = m // batch // tm
    row = lambda c: pl.BlockSpec((tm, d), lambda b, s: (b * ns + s, c))
    vec, mat = _resident((1, d)), _resident((d, d))
    r1 = lambda g: g.reshape(1, d)
    return pl.pallas_call(
        _tail_kernel,
        grid=(batch, ns),
        in_specs=[row(0), row(0), row(0), row(0), row(0), row(1),
                  pl.BlockSpec((MEM_LEN, 2 * d), lambda b, s: (b, 0)),
                  vec, mat, mat, mat, vec, vec, mat, mat, vec,
                  vec, _resident(w_in.shape), _resident(w_down.shape), vec],
        out_specs=row(0),
        out_shape=jax.ShapeDtypeStruct((m, d), F32),
        scratch_shapes=[pltpu.VMEM((tm, d), F32)],
        compiler_params=_params(("parallel", "parallel")),
        name="merge_memattn_ffn",
    )(x, o_a, g_act, o_b, gates, gates, kv, r1(hg_norm_g), w_a, w_b, w_o, r1(mix_post_g),
      r1(mem_pre_g), w_q, w_mo, r1(mem_post_g), r1(ffn_pre_g), w_in, w_down, r1(ffn_post_g))


def kernel(x, mem, ffn1_pre_g, ffn1_w_in, ffn1_w_down, ffn1_post_g, mix_pre_g, w_in, hg_lb_logits, hg_norm_g, fox_f_bias, w_branch_a, w_branch_b, b_gate, w_out, mix_post_g, mem_pre_g, mem_kv_g, w_mq, w_mkv, w_mo, mem_post_g, ffn2_pre_g, ffn2_w_in, ffn2_w_down, ffn2_post_g):
    batch, seq, d = x.shape
    depth = ffn1_w_in.shape[0]
    xf = x.reshape(batch * seq, d)
    memf = mem.reshape(batch * MEM_LEN, d)
    kw = HG_HEADS * HG_DK
    off_b = 4 * kw
    off_f = off_b + 3 * FOX_HEADS * FOX_DH
    off_g = off_f + FOX_HEADS
    for l in range(depth):
        xf = _ffn(xf, ffn1_pre_g[l], ffn1_w_in[l].astype(BF16), ffn1_w_down[l].astype(BF16),
                  ffn1_post_g[l])

        wt = jnp.swapaxes(w_in[l], 0, 1)
        fox_w = FOX_HEADS * FOX_DH
        weights = (wt[:off_f].astype(BF16),
                   jnp.pad(wt[off_f:off_g], ((0, LANES - FOX_HEADS), (0, 0))).astype(BF16),
                   wt[off_g:].astype(BF16))
        q_scale = LOG2E / math.sqrt(FOX_DH)
        groups = ((0, 0, "silu", 1.0), (0, kw, None, 1.0), (0, 2 * kw, None, 1.0),
                  (0, 3 * kw, "silu", 1.0), (0, off_b, "transpose", q_scale),
                  (0, off_b + fox_w, None, 1.0), (0, off_b + 2 * fox_w, "transpose", 1.0),
                  (1, 0, "transpose_result", 1.0), (2, 0, "sigmoid_bias", 1.0))
        q_act, f_logit, inp, g_act, q_t, k_b, v_t, fb_t, gates = _inproj(
            xf, mix_pre_g[l], weights, b_gate[l].reshape(1, 2 * d), groups,
            (kw, kw, kw, kw, fox_w, fox_w, fox_w, LANES, 2 * d),
            (BF16, F32, BF16, BF16, BF16, BF16, BF16, F32, BF16))

        c = _fcum(fb_t, fox_f_bias[l].reshape(FOX_HEADS, 1), batch, seq)

        o_a = _hgrn2(q_act, f_logit, inp, hg_lb_logits, l, batch)
        o_b = _fox(q_t, k_b, v_t, c, batch, seq)

        kv = _norm_matmul(memf, mem_kv_g[l], w_mkv[l].astype(BF16), BF16, MEM_LEN, 1024)
        xf = _tail(xf, o_a, g_act, o_b, gates, kv, hg_norm_g[l],
                   w_branch_a[l].astype(BF16), w_branch_b[l].astype(BF16), w_out[l].astype(BF16),
                   mix_post_g[l], mem_pre_g[l],
                   (w_mq[l] * (1.0 / math.sqrt(MEM_DH))).astype(BF16), w_mo[l].astype(BF16),
                   mem_post_g[l], ffn2_pre_g[l], ffn2_w_in[l].astype(BF16),
                   ffn2_w_down[l].astype(BF16), ffn2_post_g[l], batch)
    return xf.reshape(batch, seq, d)
```

```python
import functools
import math

import jax
import jax.numpy as jnp
import numpy as np
from jax import lax
from jax.experimental import pallas as pl
from jax.experimental.pallas import tpu as pltpu

F32 = jnp.float32
BF16 = jnp.bfloat16

D_MODEL = 1024
HG_HEADS = 8
HG_DK = 128
FOX_HEADS = 8
FOX_DH = 128
MEM_LEN = 256
MEM_HEADS = 4
MEM_DH = D_MODEL // MEM_HEADS
D_FF = 2816
EPS = 1e-6
LANES = 128

VMEM_LIMIT = 56 * 1024 * 1024

NT_DIMS = (((1,), (1,)), ((), ()))
TN_DIMS = (((0,), (0,)), ((), ()))


def _params(sem):
    return pltpu.CompilerParams(dimension_semantics=sem, vmem_limit_bytes=VMEM_LIMIT)


def _rms(x, g):
    ms = jnp.mean(x * x, axis=-1, keepdims=True)
    return x * lax.rsqrt(ms + EPS) * g


def _sigmoid(x):
    return 0.5 * jnp.tanh(0.5 * x) + 0.5


def _dot(a, b):
    return jnp.dot(a, b, preferred_element_type=F32)


def _norm_matmul_kernel(x_ref, g_ref, w_ref, o_ref, h_ref):
    @pl.when(pl.program_id(1) == 0)
    def _():
        h_ref[...] = _rms(x_ref[...], g_ref[...]).astype(BF16)

    o_ref[...] = _dot(h_ref[...], w_ref[...]).astype(o_ref.dtype)


def _norm_matmul(x, g, w, out_dtype, tm, tn):
    m, d = x.shape
    n = w.shape[1]
    return pl.pallas_call(
        _norm_matmul_kernel,
        grid=(m // tm, n // tn),
        in_specs=[
            pl.BlockSpec((tm, d), lambda i, j: (i, 0)),
            pl.BlockSpec((1, d), lambda i, j: (0, 0)),
            pl.BlockSpec((d, tn), lambda i, j: (0, j)),
        ],
        out_specs=pl.BlockSpec((tm, tn), lambda i, j: (i, j)),
        out_shape=jax.ShapeDtypeStruct((m, n), out_dtype),
        scratch_shapes=[pltpu.VMEM((tm, d), BF16)],
        compiler_params=_params(("parallel", "arbitrary")),
        name="norm_matmul",
    )(x, g.reshape(1, d), w)


def _resident(shape):
    return pl.BlockSpec(shape, lambda *_: (0,) * len(shape), pipeline_mode=pl.Buffered(1))


def _inproj_kernel(x_ref, g_ref, *refs, groups, tn):
    n_w = 1 + max(grp[0] for grp in groups)
    wt_refs, bias_ref, out_refs = refs[:n_w], refs[n_w], refs[n_w + 1:]
    h = _rms(x_ref[...], g_ref[...]).astype(BF16)
    for o_ref, (wi, row0, act, scale) in zip(out_refs, groups):
        columns_first = act in ("transpose", "transpose_result")
        n = o_ref.shape[0] if columns_first else o_ref.shape[1]
        for j in range(0, n, tn):
            width = min(tn, n - j)
            w_rows = wt_refs[wi][row0 + j:row0 + j + width, :]
            if act == "transpose":
                y = lax.dot_general(w_rows, h, NT_DIMS, preferred_element_type=F32)
                o_ref[j:j + width, :] = (y * scale).astype(o_ref.dtype)
                continue
            y = lax.dot_general(h, w_rows, NT_DIMS, preferred_element_type=F32)
            if act == "transpose_result":
                o_ref[j:j + width, :] = y.T.astype(o_ref.dtype)
                continue
            if act == "silu":
                y = y * _sigmoid(y)
            elif act == "sigmoid_bias":
                y = _sigmoid(y + bias_ref[:, j:j + width])
            o_ref[:, j:j + width] = y.astype(o_ref.dtype)


def _inproj(x, g, weights, bias, groups, widths, dtypes, tm=512, tn=512):
    m, d = x.shape
    transposed = [grp[2] in ("transpose", "transpose_result") for grp in groups]
    return pl.pallas_call(
        functools.partial(_inproj_kernel, groups=groups, tn=tn),
        grid=(m // tm,),
        in_specs=[pl.BlockSpec((tm, d), lambda i: (i, 0)), _resident((1, d))]
        + [_resident(w.shape) for w in weights] + [_resident(bias.shape)],
        out_specs=[pl.BlockSpec((n, tm), lambda i: (0, i)) if t else
                   pl.BlockSpec((tm, n), lambda i: (i, 0)) for n, t in zip(widths, transposed)],
        out_shape=[jax.ShapeDtypeStruct((n, m) if t else (m, n), dt)
                   for n, dt, t in zip(widths, dtypes, transposed)],
        compiler_params=_params(("parallel",)),
        name="input_projection",
    )(x, g.reshape(1, d), *weights, bias)


FFN_CHUNK = 256


def _ffn_body(x, pre_ref, w_in_ref, w_down_ref, post_ref, acc_ref):
    h = _rms(x, pre_ref[...]).astype(BF16)
    for f in range(D_FF // FFN_CHUNK):
        cols = slice(f * FFN_CHUNK, (f + 1) * FFN_CHUNK)
        gate = _dot(h, w_in_ref[:, cols])
        up = _dot(h, w_in_ref[:, D_FF + f * FFN_CHUNK:D_FF + (f + 1) * FFN_CHUNK])
        act = (gate * _sigmoid(gate) * up).astype(BF16)
        part = _dot(act, w_down_ref[cols, :])
        if f == 0:
            acc_ref[...] = part
        else:
            acc_ref[...] += part
    return x + 0.5 * _rms(acc_ref[...], post_ref[...])


def _ffn_kernel(x_ref, pre_ref, w_in_ref, w_down_ref, post_ref, o_ref, acc_ref):
    o_ref[...] = _ffn_body(x_ref[...], pre_ref, w_in_ref, w_down_ref, post_ref, acc_ref)


def _ffn(x, pre_g, w_in, w_down, post_g, tm=1024):
    m, d = x.shape
    return pl.pallas_call(
        _ffn_kernel,
        grid=(m // tm,),
        in_specs=[
            pl.BlockSpec((tm, d), lambda i: (i, 0)),
            _resident((1, d)),
            _resident(w_in.shape),
            _resident(w_down.shape),
            _resident((1, d)),
        ],
        out_specs=pl.BlockSpec((tm, d), lambda i: (i, 0)),
        out_shape=jax.ShapeDtypeStruct((m, d), F32),
        scratch_shapes=[pltpu.VMEM((tm, d), F32)],
        compiler_params=_params(("parallel",)),
        name="ffn",
    )(x, pre_g.reshape(1, d), w_in, w_down, post_g.reshape(1, d))


def _split3(x):
    hi = x.astype(BF16)
    r = x - hi.astype(F32)
    mid = r.astype(BF16)
    lo = (r - mid.astype(F32)).astype(BF16)
    return hi, mid, lo


def _fcum_kernel(fb_ref, bias_ref, c_ref, carry_ref):
    @pl.when(pl.program_id(1) == 0)
    def _():
        carry_ref[...] = jnp.zeros_like(carry_ref)

    z = fb_ref[0:FOX_HEADS, :] + bias_ref[...]
    ls = jnp.minimum(z, 0.0) - jnp.log(1.0 + jnp.exp(-jnp.abs(z)))
    tb = z.shape[1]
    row = lax.broadcasted_iota(jnp.int32, (tb, tb), 0)
    col = lax.broadcasted_iota(jnp.int32, (tb, tb), 1)
    triu = (row <= col).astype(BF16)
    stacked = jnp.concatenate([p.astype(F32) for p in _split3(ls)], axis=0).astype(BF16)
    parts = _dot(stacked, triu)
    c = (parts[0:FOX_HEADS] + parts[FOX_HEADS:2 * FOX_HEADS] + parts[2 * FOX_HEADS:]
         + carry_ref[...])
    c_ref[...] = c
    carry_ref[...] = jnp.broadcast_to(c[:, tb - 1:tb], carry_ref.shape)


def _fcum(fb_t, bias, batch, seq, tb=512):
    ns = seq // tb
    return pl.pallas_call(
        _fcum_kernel,
        grid=(batch, ns),
        in_specs=[
            pl.BlockSpec((LANES, tb), lambda b, s: (0, b * ns + s)),
            pl.BlockSpec((FOX_HEADS, 1), lambda b, s: (0, 0)),
        ],
        out_specs=pl.BlockSpec((None, FOX_HEADS, tb), lambda b, s: (b, 0, s)),
        out_shape=jax.ShapeDtypeStruct((batch, FOX_HEADS, seq), F32),
        scratch_shapes=[pltpu.VMEM((FOX_HEADS, tb), F32)],
        compiler_params=_params(("parallel", "arbitrary")),
        name="fox_decay_cumsum",
    )(fb_t, bias)


def _hgrn_level_table(c):
    t = np.arange(c)[:, None]
    s = np.arange(c)[None, :]
    level = np.full((c, c), -2, np.int32)
    level[t == s] = -1
    for l in range(int(math.log2(c))):
        h = c >> (l + 1)
        mid = (t // (2 * h)) * (2 * h) + h
        sel = (t // (2 * h) == s // (2 * h)) & (t >= mid) & (s < mid)
        level[sel & (level == -2)] = l
    return level


SUBLANES = 8


def _prefix_sum_rows(x, row):
    k = 1
    while k < x.shape[0]:
        x = x + jnp.where(row >= k, pltpu.roll(x, k, axis=0), 0.0)
        k *= 2
    return x


def _hgrn_sign_table(c):
    t = np.arange(c)[:, None]
    halves = [c >> (l + 1) for l in range(int(math.log2(c)))]
    sign = [np.where((t % (2 * h)) >= h, 1.0, -1.0) * np.ones((1, HG_DK)) for h in halves]
    return np.stack(sign).astype(np.float32)


def _midpoint_rows(b, h):
    c, w = b.shape
    if h >= SUBLANES:
        pieces = [jnp.broadcast_to(b[j * 2 * h + h - 1:j * 2 * h + h, :], (2 * h, w))
                  for j in range(c // (2 * h))]
        return pieces[0] if len(pieces) == 1 else jnp.concatenate(pieces, axis=0)
    groups = c // SUBLANES
    b3 = b.reshape(groups, SUBLANES, w)
    sub = lax.broadcasted_iota(jnp.int32, (1, SUBLANES, w), 1)
    r3 = None
    for j in range(SUBLANES // (2 * h)):
        src = j * 2 * h + h - 1
        piece = jnp.broadcast_to(b3[:, src:src + 1, :], (groups, SUBLANES, w))
        r3 = piece if r3 is None else jnp.where(sub >= j * 2 * h, piece, r3)
    return r3.reshape(c, w)


def _hgrn_kernel(q_ref, f_ref, i_ref, lb_ref, lvl_ref, sign_ref, o_ref, st_ref, *, chunk, layer):
    @pl.when(pl.program_id(2) == 0)
    def _():
        st_ref[...] = jnp.zeros_like(st_ref)

    c = chunk
    n_lvl = int(math.log2(c))
    ts = q_ref.shape[0]
    logits = lb_ref[...]
    e = jnp.exp(logits - jnp.max(logits, axis=0, keepdims=True))
    lb = jnp.sum(e[0:layer + 1, :], axis=0, keepdims=True) / jnp.sum(e, axis=0, keepdims=True)
    lvl = lvl_ref[...]
    st = st_ref[...]

    row = lax.broadcasted_iota(jnp.int32, (c, HG_DK), 0)
    ks, bs = [], []
    for ci in range(ts // c):
        f = lb + (1.0 - lb) * _sigmoid(f_ref[pl.ds(ci * c, c), :])
        bs.append(_prefix_sum_rows(jnp.log2(f), row))
        ks.append(1.0 - f)

    chunks = range(ts // c)
    q16s = [q_ref[pl.ds(ci * c, c), :] for ci in chunks]
    vs = [i_ref[pl.ds(ci * c, c), :] for ci in chunks]
    k16s = [k.astype(BF16) for k in ks]

    a = [jnp.where(lvl == -1, lax.dot_general(q16s[ci], k16s[ci], NT_DIMS,
                                              preferred_element_type=F32), 0.0) for ci in chunks]
    for l in range(n_lvl):
        h = c >> (l + 1)
        for ci in chunks:
            d_l = (bs[ci] - _midpoint_rows(bs[ci], h)) * sign_ref[l]
            e_l = jnp.exp2(d_l).astype(BF16)
            p = lax.dot_general(q16s[ci] * e_l, k16s[ci] * e_l, NT_DIMS,
                                preferred_element_type=F32)
            a[ci] = jnp.where(lvl == l, p, a[ci])

    o_intra = [_dot(a[ci].astype(BF16), vs[ci]) for ci in chunks]
    upds = [lax.dot_general(vs[ci], (ks[ci] * jnp.exp2(bs[ci][c - 1:c, :] - bs[ci])).astype(BF16),
                            TN_DIMS, preferred_element_type=F32) for ci in chunks]

    for ci in chunks:
        o = o_intra[ci] + lax.dot_general(q16s[ci] * jnp.exp2(bs[ci]).astype(BF16),
                                          st.astype(BF16), NT_DIMS, preferred_element_type=F32)
        o_ref[pl.ds(ci * c, c), :] = o.astype(o_ref.dtype)
        st = st * jnp.exp2(bs[ci][c - 1:c, :]) + upds[ci]

    st_ref[...] = st


def _hgrn2(q_act, f_logit, inp, lb_logits, layer, batch, ts=2048, chunk=128):
    m = q_act.shape[0]
    ns = m // batch // ts
    level = _hgrn_level_table(chunk)
    sign = _hgrn_sign_table(chunk)
    nl = lb_logits.shape[0]
    blk = pl.BlockSpec((ts, HG_DK), lambda b, h, s: (b * ns + s, h))
    return pl.pallas_call(
        functools.partial(_hgrn_kernel, chunk=chunk, layer=layer),
        grid=(batch, HG_HEADS, ns),
        in_specs=[
            blk, blk, blk,
            pl.BlockSpec((nl, HG_DK), lambda b, h, s: (0, h)),
            _resident(level.shape), _resident(sign.shape),
        ],
        out_specs=blk,
        out_shape=jax.ShapeDtypeStruct((m, HG_HEADS * HG_DK), BF16),
        scratch_shapes=[pltpu.VMEM((HG_DK, HG_DK), F32)],
        compiler_params=_params(("parallel", "parallel", "arbitrary")),
        name="hgrn2",
    )(q_act, f_logit, inp, lb_logits.reshape(nl, HG_HEADS * HG_DK), jnp.asarray(level),
      jnp.asarray(sign))


LOG2E = 1.4426950408889634
FOX_FOLD = 8
FOX_SUM_ROWS = 16


def _decay_features_t(heads, h):
    rows = heads.shape[1]
    sub = lax.broadcasted_iota(jnp.int32, (SUBLANES, rows), 0)
    c_row = jnp.sum(jnp.where(sub == h, heads, 0.0), axis=0, keepdims=True)
    hi, mid, lo = (jnp.broadcast_to(part.astype(F32), (SUBLANES, rows))
                   for part in _split3(c_row * LOG2E))

    def place(base, sign):
        ones = (sub >= 3 - base) & (sub < 6 - base)
        feat = jnp.where(sub == base, sign * hi,
                         jnp.where(sub == base + 1, sign * mid,
                                   jnp.where(sub == base + 2, sign * lo,
                                             jnp.where(ones, 1.0, 0.0))))
        pad = jnp.zeros((LANES - SUBLANES, rows), F32)
        return jnp.concatenate([feat, pad], axis=0).astype(BF16)

    return place(3, 1.0), place(0, -1.0)


def _fox_kernel(q_ref, k_ref, v_ref, c_ref, o_ref, kx_ref, qft_ref, vt_ref, qxt_ref,
                s0_ref, s1_ref, x0_ref, x1_ref, m_ref, acc_ref, *, tk):
    h = pl.program_id(1)
    qi = pl.program_id(2)
    tq = q_ref.shape[1]
    nk = kx_ref.shape[0]

    @pl.when(qi == 0)
    def _():
        ones_row = (lax.broadcasted_iota(jnp.int32, (FOX_SUM_ROWS, tk), 0) == 0)
        ones_row = ones_row.astype(F32).astype(BF16)
        for kb in range(nk):
            rows = slice(kb * tk, (kb + 1) * tk)
            q_feat_t, k_feat_t = _decay_features_t(c_ref[:, rows], h)
            kx_ref[kb, :, 0:FOX_DH] = k_ref[rows, :]
            kx_ref[kb, :, FOX_DH:2 * FOX_DH] = k_feat_t.T
            qft_ref[kb] = q_feat_t
            vt_ref[kb, 0:FOX_DH, :] = v_ref[:, rows]
            vt_ref[kb, FOX_DH:FOX_DH + FOX_SUM_ROWS, :] = ones_row

    ratio = tq // tk
    n_full = ratio * qi
    qxt_ref[0:FOX_DH, :] = q_ref[...]
    for j in range(ratio):
        qxt_ref[FOX_DH:2 * FOX_DH, j * tk:(j + 1) * tk] = qft_ref[n_full + j]
    m_ref[...] = jnp.full_like(m_ref, -jnp.inf)
    acc_ref[...] = jnp.zeros_like(acc_ref)

    def fold(x, op):
        part = op(x.reshape(FOX_FOLD, tk // FOX_FOLD, x.shape[1]), axis=0)
        return op(part, axis=0, keepdims=True)

    def scores(kb, s_ref, smax_ref):
        st = _dot(kx_ref[kb], qxt_ref[...])
        s_ref[...] = st
        smax_ref[...] = fold(st, jnp.max)

    def causal(st):
        key = lax.broadcasted_iota(jnp.int32, st.shape, 0)
        qry = lax.broadcasted_iota(jnp.int32, st.shape, 1)
        return jnp.where(key <= qry, st, -jnp.inf)

    def softmax_step(kb, st, blk_max, cols):
        m_prev = m_ref[:, cols]
        m_new = jnp.maximum(m_prev, blk_max)
        alpha = jnp.exp2(m_prev - m_new)
        p = jnp.exp2(st - m_new)
        m_ref[:, cols] = m_new
        acc_ref[:, cols] = alpha * acc_ref[:, cols] + _dot(vt_ref[kb], p.astype(BF16))

    def update(kb, s_ref, smax_ref):
        softmax_step(kb, s_ref[...], smax_ref[...], slice(None))

    def update_diagonal(kb, s_ref):
        st = s_ref[...]
        head = causal(st[:, 0:tk])
        st = head if tq == tk else jnp.concatenate([head, st[:, tk:]], axis=1)
        softmax_step(kb, st, fold(st, jnp.max), slice(None))

    scores(0, s0_ref, x0_ref)

    def pair(kb):
        scores(kb + 1, s1_ref, x1_ref)
        update(kb, s0_ref, x0_ref)
        scores(kb + 2, s0_ref, x0_ref)
        update(kb + 1, s1_ref, x1_ref)

    def body4(j, carry):
        pair(4 * j)
        pair(4 * j + 2)
        return carry

    def body2(j, carry):
        pair(2 * j)
        return carry

    n_pairs = n_full // 2
    lax.fori_loop(0, n_pairs // 2, body4, 0)
    lax.fori_loop(2 * (n_pairs // 2), n_pairs, body2, 0)

    if ratio == 2:
        upper = slice(tk, tq)
        tile = causal(_dot(kx_ref[n_full + 1], qxt_ref[:, upper]))
        update_diagonal(n_full, s0_ref)
        softmax_step(n_full + 1, tile, fold(tile, jnp.max), upper)
    else:
        @pl.when(qi % 2 == 1)
        def _():
            scores(qi, s1_ref, x1_ref)
            update(qi - 1, s0_ref, x0_ref)
            update_diagonal(qi, s1_ref)

        @pl.when(qi % 2 == 0)
        def _():
            update_diagonal(qi, s0_ref)

    out = acc_ref[0:FOX_DH, :] / acc_ref[FOX_DH:FOX_DH + 1, :]
    o_ref[...] = out.astype(o_ref.dtype).T


def _fox(q_t, k, v_t, c, batch, seq, tq=1024, tk=512):
    assert tq in (tk, 2 * tk)
    m = k.shape[0]
    nq = seq // tq
    nk = seq // tk
    return pl.pallas_call(
        functools.partial(_fox_kernel, tk=tk),
        grid=(batch, FOX_HEADS, nq),
        in_specs=[
            pl.BlockSpec((FOX_DH, tq), lambda b, h, i: (h, b * nq + i)),
            pl.BlockSpec((seq, FOX_DH), lambda b, h, i: (b, h)),
            pl.BlockSpec((FOX_DH, seq), lambda b, h, i: (h, b)),
            pl.BlockSpec((None, FOX_HEADS, seq), lambda b, h, i: (b, 0, 0)),
        ],
        out_specs=pl.BlockSpec((tq, FOX_DH), lambda b, h, i: (b * nq + i, h)),
        out_shape=jax.ShapeDtypeStruct((m, FOX_HEADS * FOX_DH), BF16),
        scratch_shapes=[
            pltpu.VMEM((nk, tk, 2 * FOX_DH), BF16),
            pltpu.VMEM((nk, FOX_DH, tk), BF16),
            pltpu.VMEM((nk, FOX_DH + FOX_SUM_ROWS, tk), BF16),
            pltpu.VMEM((2 * FOX_DH, tq), BF16),
            pltpu.VMEM((tk, tq), F32),
            pltpu.VMEM((tk, tq), F32),
            pltpu.VMEM((1, tq), F32),
            pltpu.VMEM((1, tq), F32),
            pltpu.VMEM((1, tq), F32),
            pltpu.VMEM((FOX_DH + FOX_SUM_ROWS, tq), F32),
        ],
        compiler_params=_params(("parallel", "parallel", "arbitrary")),
        name="fox_attention",
    )(q_t, k, v_t, c)


def _merge_body(x, oa_ref, ga_ref, ob_ref, g0_ref, g1_ref, hg_ref, wa_ref, wb_ref, wo_ref, post_ref):
    oa = _rms(oa_ref[...].astype(F32), hg_ref[...]) * ga_ref[...].astype(F32)
    ya = _dot(oa.astype(BF16), wa_ref[...])
    yb = _dot(ob_ref[...], wb_ref[...])
    y = g0_ref[...].astype(F32) * ya + g1_ref[...].astype(F32) * yb
    z = _dot(y.astype(BF16), wo_ref[...])
    return x + _rms(z, post_ref[...])


def _memattn_body(x, pre_ref, kv_ref, wq_ref, wo_ref, post_ref):
    h = _rms(x, pre_ref[...]).astype(BF16)
    q = _dot(h, wq_ref[...]).astype(BF16)
    heads = []
    for hd in range(MEM_HEADS):
        lo = hd * MEM_DH
        kh = kv_ref[:, lo:lo + MEM_DH]
        vh = kv_ref[:, D_MODEL + lo:D_MODEL + lo + MEM_DH]
        s = lax.dot_general(q[:, lo:lo + MEM_DH], kh, NT_DIMS, preferred_element_type=F32)
        p = jnp.exp(s - jnp.max(s, axis=1, keepdims=True))
        inv = 1.0 / jnp.sum(p, axis=1, keepdims=True)
        heads.append((_dot(p.astype(BF16), vh) * inv).astype(BF16))
    o = jnp.concatenate(heads, axis=1)
    return x + _rms(_dot(o, wo_ref[...]), post_ref[...])


def _tail_kernel(x_ref, oa_ref, ga_ref, ob_ref, g0_ref, g1_ref, kv_ref, hg_ref, wa_ref, wb_ref,
                 wo_ref, mix_post_ref, mem_pre_ref, wq_ref, wmo_ref, mem_post_ref,
                 ffn_pre_ref, w_in_ref, w_down_ref, ffn_post_ref, o_ref, acc_ref):
    x = _merge_body(x_ref[...], oa_ref, ga_ref, ob_ref, g0_ref, g1_ref, hg_ref, wa_ref, wb_ref,
                    wo_ref, mix_post_ref)
    x = _memattn_body(x, mem_pre_ref, kv_ref, wq_ref, wmo_ref, mem_post_ref)
    o_ref[...] = _ffn_body(x, ffn_pre_ref, w_in_ref, w_down_ref, ffn_post_ref, acc_ref)


def _tail(x, o_a, g_act, o_b, gates, kv, hg_norm_g, w_a, w_b, w_o, mix_post_g, mem_pre_g, w_q,
          w_mo, mem_post_g, ffn_pre_g, w_in, w_down, ffn_post_g, batch, tm=512):
    m, d = x.shape
    ns = m // batch // tm
    row = lambda c: pl.BlockSpec((tm, d), lambda b, s: (b * ns + s, c))
    vec, mat = _resident((1, d)), _resident((d, d))
    r1 = lambda g: g.reshape(1, d)
    return pl.pallas_call(
        _tail_kernel,
        grid=(batch, ns),
        in_specs=[row(0), row(0), row(0), row(0), row(0), row(1),
                  pl.BlockSpec((MEM_LEN, 2 * d), lambda b, s: (b, 0)),
                  vec, mat, mat, mat, vec, vec, mat, mat, vec,
                  vec, _resident(w_in.shape), _resident(w_down.shape), vec],
        out_specs=row(0),
        out_shape=jax.ShapeDtypeStruct((m, d), F32),
        scratch_shapes=[pltpu.VMEM((tm, d), F32)],
        compiler_params=_params(("parallel", "parallel")),
        name="merge_memattn_ffn",
    )(x, o_a, g_act, o_b, gates, gates, kv, r1(hg_norm_g), w_a, w_b, w_o, r1(mix_post_g),
      r1(mem_pre_g), w_q, w_mo, r1(mem_post_g), r1(ffn_pre_g), w_in, w_down, r1(ffn_post_g))


def kernel(x, mem, ffn1_pre_g, ffn1_w_in, ffn1_w_down, ffn1_post_g, mix_pre_g, w_in, hg_lb_logits, hg_norm_g, fox_f_bias, w_branch_a, w_branch_b, b_gate, w_out, mix_post_g, mem_pre_g, mem_kv_g, w_mq, w_mkv, w_mo, mem_post_g, ffn2_pre_g, ffn2_w_in, ffn2_w_down, ffn2_post_g):
    batch, seq, d = x.shape
    depth = ffn1_w_in.shape[0]
    xf = x.reshape(batch * seq, d)
    memf = mem.reshape(batch * MEM_LEN, d)
    kw = HG_HEADS * HG_DK
    off_b = 4 * kw
    off_f = off_b + 3 * FOX_HEADS * FOX_DH
    off_g = off_f + FOX_HEADS
    for l in range(depth):
        xf = _ffn(xf, ffn1_pre_g[l], ffn1_w_in[l].astype(BF16), ffn1_w_down[l].astype(BF16),
                  ffn1_post_g[l])

        wt = jnp.swapaxes(w_in[l], 0, 1)
        fox_w = FOX_HEADS * FOX_DH
        weights = (wt.astype(BF16),)
        q_scale = LOG2E / math.sqrt(FOX_DH)
        groups = ((0, 0, "silu", 1.0), (0, kw, None, 1.0), (0, 2 * kw, None, 1.0),
                  (0, 3 * kw, "silu", 1.0), (0, off_b, "transpose", q_scale),
                  (0, off_b + fox_w, None, 1.0), (0, off_b + 2 * fox_w, "transpose", 1.0),
                  (0, off_f, "transpose_result", 1.0), (0, off_g, "sigmoid_bias", 1.0))
        q_act, f_logit, inp, g_act, q_t, k_b, v_t, fb_t, gates = _inproj(
            xf, mix_pre_g[l], weights, b_gate[l].reshape(1, 2 * d), groups,
            (kw, kw, kw, kw, fox_w, fox_w, fox_w, LANES, 2 * d),
            (BF16, F32, BF16, BF16, BF16, BF16, BF16, F32, BF16))

        c = _fcum(fb_t, fox_f_bias[l].reshape(FOX_HEADS, 1), batch, seq)

        o_a = _hgrn2(q_act, f_logit, inp, hg_lb_logits, l, batch)
        o_b = _fox(q_t, k_b, v_t, c, batch, seq)

        kv = _norm_matmul(memf, mem_kv_g[l], w_mkv[l].astype(BF16), BF16, MEM_LEN, 1024)
        xf = _tail(xf, o_a, g_act, o_b, gates, kv, hg_norm_g[l],
                   w_branch_a[l].astype(BF16), w_branch_b[l].astype(BF16), w_out[l].astype(BF16),
                   mix_post_g[l], mem_pre_g[l],
                   (w_mq[l] * (1.0 / math.sqrt(MEM_DH))).astype(BF16), w_mo[l].astype(BF16),
                   mem_post_g[l], ffn2_pre_g[l], ffn2_w_in[l].astype(BF16),
                   ffn2_w_down[l].astype(BF16), ffn2_post_g[l], batch)
    return xf.reshape(batch, seq, d)
```

```python
import functools
import math

import jax
import jax.numpy as jnp
import numpy as np
from jax import lax
from jax.experimental import pallas as pl
from jax.experimental.pallas import tpu as pltpu

F32 = jnp.float32
BF16 = jnp.bfloat16

D_MODEL = 1024
HG_HEADS = 8
HG_DK = 128
FOX_HEADS = 8
FOX_DH = 128
MEM_LEN = 256
MEM_HEADS = 4
MEM_DH = D_MODEL // MEM_HEADS
D_FF = 2816
EPS = 1e-6
LANES = 128

VMEM_LIMIT = 56 * 1024 * 1024

NT_DIMS = (((1,), (1,)), ((), ()))
TN_DIMS = (((0,), (0,)), ((), ()))


def _params(sem):
    return pltpu.CompilerParams(dimension_semantics=sem, vmem_limit_bytes=VMEM_LIMIT)


def _rms(x, g):
    ms = jnp.mean(x * x, axis=-1, keepdims=True)
    return x * lax.rsqrt(ms + EPS) * g


def _sigmoid(x):
    return 0.5 * jnp.tanh(0.5 * x) + 0.5


def _dot(a, b):
    return jnp.dot(a, b, preferred_element_type=F32)


def _norm_matmul_kernel(x_ref, g_ref, w_ref, o_ref, h_ref):
    @pl.when(pl.program_id(1) == 0)
    def _():
        h_ref[...] = _rms(x_ref[...], g_ref[...]).astype(BF16)

    o_ref[...] = _dot(h_ref[...], w_ref[...]).astype(o_ref.dtype)


def _norm_matmul(x, g, w, out_dtype, tm, tn):
    m, d = x.shape
    n = w.shape[1]
    return pl.pallas_call(
        _norm_matmul_kernel,
        grid=(m // tm, n // tn),
        in_specs=[
            pl.BlockSpec((tm, d), lambda i, j: (i, 0)),
            pl.BlockSpec((1, d), lambda i, j: (0, 0)),
            pl.BlockSpec((d, tn), lambda i, j: (0, j)),
        ],
        out_specs=pl.BlockSpec((tm, tn), lambda i, j: (i, j)),
        out_shape=jax.ShapeDtypeStruct((m, n), out_dtype),
        scratch_shapes=[pltpu.VMEM((tm, d), BF16)],
        compiler_params=_params(("parallel", "arbitrary")),
        name="norm_matmul",
    )(x, g.reshape(1, d), w)


def _resident(shape):
    return pl.BlockSpec(shape, lambda *_: (0,) * len(shape), pipeline_mode=pl.Buffered(1))


def _inproj_kernel(x_ref, g_ref, *refs, groups, tn):
    n_w = 1 + max(grp[0] for grp in groups)
    wt_refs, bias_ref, out_refs = refs[:n_w], refs[n_w], refs[n_w + 1:]
    h = _rms(x_ref[...], g_ref[...]).astype(BF16)
    for o_ref, (wi, row0, act, scale) in zip(out_refs, groups):
        columns_first = act in ("transpose", "transpose_result")
        n = o_ref.shape[0] if columns_first else o_ref.shape[1]
        for j in range(0, n, tn):
            width = min(tn, n - j)
            w_rows = wt_refs[wi][row0 + j:row0 + j + width, :]
            if act == "transpose":
                y = lax.dot_general(w_rows, h, NT_DIMS, preferred_element_type=F32)
                o_ref[j:j + width, :] = (y * scale).astype(o_ref.dtype)
                continue
            y = lax.dot_general(h, w_rows, NT_DIMS, preferred_element_type=F32)
            if act == "transpose_result":
                o_ref[j:j + width, :] = y.T.astype(o_ref.dtype)
                continue
            if act == "silu":
                y = y * _sigmoid(y)
            elif act == "sigmoid_bias":
                y = _sigmoid(y + bias_ref[:, j:j + width])
            o_ref[:, j:j + width] = y.astype(o_ref.dtype)


def _inproj(x, g, weights, bias, groups, widths, dtypes, tm=512, tn=512):
    m, d = x.shape
    transposed = [grp[2] in ("transpose", "transpose_result") for grp in groups]
    return pl.pallas_call(
        functools.partial(_inproj_kernel, groups=groups, tn=tn),
        grid=(m // tm,),
        in_specs=[pl.BlockSpec((tm, d), lambda i: (i, 0)), _resident((1, d))]
        + [_resident(w.shape) for w in weights] + [_resident(bias.shape)],
        out_specs=[pl.BlockSpec((n, tm), lambda i: (0, i)) if t else
                   pl.BlockSpec((tm, n), lambda i: (i, 0)) for n, t in zip(widths, transposed)],
        out_shape=[jax.ShapeDtypeStruct((n, m) if t else (m, n), dt)
                   for n, dt, t in zip(widths, dtypes, transposed)],
        compiler_params=_params(("parallel",)),
        name="input_projection",
    )(x, g.reshape(1, d), *weights, bias)


FFN_CHUNK = 256


def _ffn_body(x, pre_ref, w_in_ref, w_down_ref, post_ref, acc_ref):
    h = _rms(x, pre_ref[...]).astype(BF16)
    for f in range(D_FF // FFN_CHUNK):
        cols = slice(f * FFN_CHUNK, (f + 1) * FFN_CHUNK)
        gate = _dot(h, w_in_ref[:, cols])
        up = _dot(h, w_in_ref[:, D_FF + f * FFN_CHUNK:D_FF + (f + 1) * FFN_CHUNK])
        act = (gate * _sigmoid(gate) * up).astype(BF16)
        part = _dot(act, w_down_ref[cols, :])
        if f == 0:
            acc_ref[...] = part
        else:
            acc_ref[...] += part
    return x + 0.5 * _rms(acc_ref[...], post_ref[...])


def _ffn_kernel(x_ref, pre_ref, w_in_ref, w_down_ref, post_ref, o_ref, acc_ref):
    o_ref[...] = _ffn_body(x_ref[...], pre_ref, w_in_ref, w_down_ref, post_ref, acc_ref)


def _ffn(x, pre_g, w_in, w_down, post_g, tm=1024):
    m, d = x.shape
    return pl.pallas_call(
        _ffn_kernel,
        grid=(m // tm,),
        in_specs=[
            pl.BlockSpec((tm, d), lambda i: (i, 0)),
            _resident((1, d)),
            _resident(w_in.shape),
            _resident(w_down.shape),
            _resident((1, d)),
        ],
        out_specs=pl.BlockSpec((tm, d), lambda i: (i, 0)),
        out_shape=jax.ShapeDtypeStruct((m, d), F32),
        scratch_shapes=[pltpu.VMEM((tm, d), F32)],
        compiler_params=_params(("parallel",)),
        name="ffn",
    )(x, pre_g.reshape(1, d), w_in, w_down, post_g.reshape(1, d))


def _split3(x):
    hi = x.astype(BF16)
    r = x - hi.astype(F32)
    mid = r.astype(BF16)
    lo = (r - mid.astype(F32)).astype(BF16)
    return hi, mid, lo


def _fcum_kernel(fb_ref, bias_ref, c_ref, carry_ref):
    @pl.when(pl.program_id(1) == 0)
    def _():
        carry_ref[...] = jnp.zeros_like(carry_ref)

    z = fb_ref[0:FOX_HEADS, :] + bias_ref[...]
    ls = jnp.minimum(z, 0.0) - jnp.log(1.0 + jnp.exp(-jnp.abs(z)))
    tb = z.shape[1]
    row = lax.broadcasted_iota(jnp.int32, (tb, tb), 0)
    col = lax.broadcasted_iota(jnp.int32, (tb, tb), 1)
    triu = (row <= col).astype(BF16)
    stacked = jnp.concatenate([p.astype(F32) for p in _split3(ls)], axis=0).astype(BF16)
    parts = _dot(stacked, triu)
    c = (parts[0:FOX_HEADS] + parts[FOX_HEADS:2 * FOX_HEADS] + parts[2 * FOX_HEADS:]
         + carry_ref[...])
    c_ref[...] = c
    carry_ref[...] = jnp.broadcast_to(c[:, tb - 1:tb], carry_ref.shape)


def _fcum(fb_t, bias, batch, seq, tb=512):
    ns = seq // tb
    return pl.pallas_call(
        _fcum_kernel,
        grid=(batch, ns),
        in_specs=[
            pl.BlockSpec((LANES, tb), lambda b, s: (0, b * ns + s)),
            pl.BlockSpec((FOX_HEADS, 1), lambda b, s: (0, 0)),
        ],
        out_specs=pl.BlockSpec((None, FOX_HEADS, tb), lambda b, s: (b, 0, s)),
        out_shape=jax.ShapeDtypeStruct((batch, FOX_HEADS, seq), F32),
        scratch_shapes=[pltpu.VMEM((FOX_HEADS, tb), F32)],
        compiler_params=_params(("parallel", "arbitrary")),
        name="fox_decay_cumsum",
    )(fb_t, bias)


def _hgrn_level_table(c):
    t = np.arange(c)[:, None]
    s = np.arange(c)[None, :]
    level = np.full((c, c), -2, np.int32)
    level[t == s] = -1
    for l in range(int(math.log2(c))):
        h = c >> (l + 1)
        mid = (t // (2 * h)) * (2 * h) + h
        sel = (t // (2 * h) == s // (2 * h)) & (t >= mid) & (s < mid)
        level[sel & (level == -2)] = l
    return level


SUBLANES = 8


def _prefix_sum_rows(x, row):
    k = 1
    while k < x.shape[0]:
        x = x + jnp.where(row >= k, pltpu.roll(x, k, axis=0), 0.0)
        k *= 2
    return x


def _hgrn_sign_table(c):
    t = np.arange(c)[:, None]
    halves = [c >> (l + 1) for l in range(int(math.log2(c)))]
    sign = [np.where((t % (2 * h)) >= h, 1.0, -1.0) * np.ones((1, HG_DK)) for h in halves]
    return np.stack(sign).astype(np.float32)


def _midpoint_rows(b, h):
    c, w = b.shape
    if h >= SUBLANES:
        pieces = [jnp.broadcast_to(b[j * 2 * h + h - 1:j * 2 * h + h, :], (2 * h, w))
                  for j in range(c // (2 * h))]
        return pieces[0] if len(pieces) == 1 else jnp.concatenate(pieces, axis=0)
    groups = c // SUBLANES
    b3 = b.reshape(groups, SUBLANES, w)
    sub = lax.broadcasted_iota(jnp.int32, (1, SUBLANES, w), 1)
    r3 = None
    for j in range(SUBLANES // (2 * h)):
        src = j * 2 * h + h - 1
        piece = jnp.broadcast_to(b3[:, src:src + 1, :], (groups, SUBLANES, w))
        r3 = piece if r3 is None else jnp.where(sub >= j * 2 * h, piece, r3)
    return r3.reshape(c, w)


def _hgrn_kernel(q_ref, f_ref, i_ref, lb_ref, lvl_ref, sign_ref, o_ref, st_ref, *, chunk, layer):
    @pl.when(pl.program_id(2) == 0)
    def _():
        st_ref[...] = jnp.zeros_like(st_ref)

    c = chunk
    n_lvl = int(math.log2(c))
    ts, width = q_ref.shape
    n_chunks = ts // c
    logits = lb_ref[...]
    e = jnp.exp(logits - jnp.max(logits, axis=0, keepdims=True))
    lb = jnp.sum(e[0:layer + 1, :], axis=0, keepdims=True) / jnp.sum(e, axis=0, keepdims=True)
    lvl = lvl_ref[...]

    tiles = [(pl.ds(ci * c, c), slice(hd * HG_DK, (hd + 1) * HG_DK))
             for hd in range(width // HG_DK) for ci in range(n_chunks)]
    chunks = range(len(tiles))

    row = lax.broadcasted_iota(jnp.int32, (c, HG_DK), 0)
    ks, bs = [], []
    for rows, cols in tiles:
        f = lb[:, cols] + (1.0 - lb[:, cols]) * _sigmoid(f_ref[rows, cols])
        bs.append(_prefix_sum_rows(jnp.log2(f), row))
        ks.append(1.0 - f)

    q16s = [q_ref[rows, cols] for rows, cols in tiles]
    vs = [i_ref[rows, cols] for rows, cols in tiles]
    k16s = [k.astype(BF16) for k in ks]

    a = [jnp.where(lvl == -1, lax.dot_general(q16s[ci], k16s[ci], NT_DIMS,
                                              preferred_element_type=F32), 0.0) for ci in chunks]
    for l in range(n_lvl):
        h = c >> (l + 1)
        for ci in chunks:
            d_l = (bs[ci] - _midpoint_rows(bs[ci], h)) * sign_ref[l]
            e_l = jnp.exp2(d_l).astype(BF16)
            p = lax.dot_general(q16s[ci] * e_l, k16s[ci] * e_l, NT_DIMS,
                                preferred_element_type=F32)
            a[ci] = jnp.where(lvl == l, p, a[ci])

    o_intra = [_dot(a[ci].astype(BF16), vs[ci]) for ci in chunks]
    upds = [lax.dot_general(vs[ci], (ks[ci] * jnp.exp2(bs[ci][c - 1:c, :] - bs[ci])).astype(BF16),
                            TN_DIMS, preferred_element_type=F32) for ci in chunks]

    for hd in range(width // HG_DK):
        st = st_ref[hd]
        for ci in range(hd * n_chunks, (hd + 1) * n_chunks):
            rows, cols = tiles[ci]
            o = o_intra[ci] + lax.dot_general(q16s[ci] * jnp.exp2(bs[ci]).astype(BF16),
                                              st.astype(BF16), NT_DIMS,
                                              preferred_element_type=F32)
            o_ref[rows, cols] = o.astype(o_ref.dtype)
            st = st * jnp.exp2(bs[ci][c - 1:c, :]) + upds[ci]
        st_ref[hd] = st


def _hgrn2(q_act, f_logit, inp, lb_logits, layer, batch, ts=1024, heads_per_step=2, chunk=128):
    m = q_act.shape[0]
    ns = m // batch // ts
    level = _hgrn_level_table(chunk)
    sign = _hgrn_sign_table(chunk)
    nl = lb_logits.shape[0]
    width = heads_per_step * HG_DK
    blk = pl.BlockSpec((ts, width), lambda b, h, s: (b * ns + s, h))
    return pl.pallas_call(
        functools.partial(_hgrn_kernel, chunk=chunk, layer=layer),
        grid=(batch, HG_HEADS // heads_per_step, ns),
        in_specs=[
            blk, blk, blk,
            pl.BlockSpec((nl, width), lambda b, h, s: (0, h)),
            _resident(level.shape), _resident(sign.shape),
        ],
        out_specs=blk,
        out_shape=jax.ShapeDtypeStruct((m, HG_HEADS * HG_DK), BF16),
        scratch_shapes=[pltpu.VMEM((heads_per_step, HG_DK, HG_DK), F32)],
        compiler_params=_params(("parallel", "parallel", "arbitrary")),
        name="hgrn2",
    )(q_act, f_logit, inp, lb_logits.reshape(nl, HG_HEADS * HG_DK), jnp.asarray(level),
      jnp.asarray(sign))


LOG2E = 1.4426950408889634
FOX_FOLD = 8
FOX_SUM_ROWS = 16


def _decay_features_t(heads, h):
    rows = heads.shape[1]
    sub = lax.broadcasted_iota(jnp.int32, (SUBLANES, rows), 0)
    c_row = jnp.sum(jnp.where(sub == h, heads, 0.0), axis=0, keepdims=True)
    hi, mid, lo = (jnp.broadcast_to(part.astype(F32), (SUBLANES, rows))
                   for part in _split3(c_row * LOG2E))

    def place(base, sign):
        ones = (sub >= 3 - base) & (sub < 6 - base)
        feat = jnp.where(sub == base, sign * hi,
                         jnp.where(sub == base + 1, sign * mid,
                                   jnp.where(sub == base + 2, sign * lo,
                                             jnp.where(ones, 1.0, 0.0))))
        pad = jnp.zeros((LANES - SUBLANES, rows), F32)
        return jnp.concatenate([feat, pad], axis=0).astype(BF16)

    return place(3, 1.0), place(0, -1.0)


def _fox_kernel(q_ref, k_ref, v_ref, c_ref, o_ref, kx_ref, qft_ref, vt_ref, qxt_ref,
                s0_ref, s1_ref, x0_ref, x1_ref, m_ref, acc_ref, *, tk):
    h = pl.program_id(1)
    qi = pl.program_id(2)
    tq = q_ref.shape[1]
    nk = kx_ref.shape[0]

    @pl.when(qi == 0)
    def _():
        ones_row = (lax.broadcasted_iota(jnp.int32, (FOX_SUM_ROWS, tk), 0) == 0)
        ones_row = ones_row.astype(F32).astype(BF16)
        for kb in range(nk):
            rows = slice(kb * tk, (kb + 1) * tk)
            q_feat_t, k_feat_t = _decay_features_t(c_ref[:, rows], h)
            kx_ref[kb, :, 0:FOX_DH] = k_ref[rows, :]
            kx_ref[kb, :, FOX_DH:2 * FOX_DH] = k_feat_t.T
            qft_ref[kb] = q_feat_t
            vt_ref[kb, 0:FOX_DH, :] = v_ref[:, rows]
            vt_ref[kb, FOX_DH:FOX_DH + FOX_SUM_ROWS, :] = ones_row

    ratio = tq // tk
    n_full = ratio * qi
    qxt_ref[0:FOX_DH, :] = q_ref[...]
    for j in range(ratio):
        qxt_ref[FOX_DH:2 * FOX_DH, j * tk:(j + 1) * tk] = qft_ref[n_full + j]
    m_ref[...] = jnp.full_like(m_ref, -jnp.inf)
    acc_ref[...] = jnp.zeros_like(acc_ref)

    def fold(x, op):
        part = op(x.reshape(FOX_FOLD, tk // FOX_FOLD, x.shape[1]), axis=0)
        return op(part, axis=0, keepdims=True)

    def scores(kb, s_ref, smax_ref):
        st = _dot(kx_ref[kb], qxt_ref[...])
        s_ref[...] = st
        smax_ref[...] = fold(st, jnp.max)

    def causal(st):
        key = lax.broadcasted_iota(jnp.int32, st.shape, 0)
        qry = lax.broadcasted_iota(jnp.int32, st.shape, 1)
        return jnp.where(key <= qry, st, -jnp.inf)

    def softmax_step(kb, st, blk_max, cols):
        m_prev = m_ref[:, cols]
        m_new = jnp.maximum(m_prev, blk_max)
        alpha = jnp.exp2(m_prev - m_new)
        p = jnp.exp2(st - m_new)
        m_ref[:, cols] = m_new
        acc_ref[:, cols] = alpha * acc_ref[:, cols] + _dot(vt_ref[kb], p.astype(BF16))

    def update(kb, s_ref, smax_ref):
        softmax_step(kb, s_ref[...], smax_ref[...], slice(None))

    def update_diagonal(kb, s_ref):
        st = s_ref[...]
        head = causal(st[:, 0:tk])
        st = head if tq == tk else jnp.concatenate([head, st[:, tk:]], axis=1)
        softmax_step(kb, st, fold(st, jnp.max), slice(None))

    scores(0, s0_ref, x0_ref)

    def pair(kb):
        scores(kb + 1, s1_ref, x1_ref)
        update(kb, s0_ref, x0_ref)
        scores(kb + 2, s0_ref, x0_ref)
        update(kb + 1, s1_ref, x1_ref)

    def body4(j, carry):
        pair(4 * j)
        pair(4 * j + 2)
        return carry

    def body2(j, carry):
        pair(2 * j)
        return carry

    n_pairs = n_full // 2
    lax.fori_loop(0, n_pairs // 2, body4, 0)
    lax.fori_loop(2 * (n_pairs // 2), n_pairs, body2, 0)

    if ratio == 2:
        upper = slice(tk, tq)
        tile = causal(_dot(kx_ref[n_full + 1], qxt_ref[:, upper]))
        update_diagonal(n_full, s0_ref)
        softmax_step(n_full + 1, tile, fold(tile, jnp.max), upper)
    else:
        @pl.when(qi % 2 == 1)
        def _():
            scores(qi, s1_ref, x1_ref)
            update(qi - 1, s0_ref, x0_ref)
            update_diagonal(qi, s1_ref)

        @pl.when(qi % 2 == 0)
        def _():
            update_diagonal(qi, s0_ref)

    out = acc_ref[0:FOX_DH, :] / acc_ref[FOX_DH:FOX_DH + 1, :]
    o_ref[...] = out.astype(o_ref.dtype).T


def _fox(q_t, k, v_t, c, batch, seq, tq=1024, tk=512):
    assert tq in (tk, 2 * tk)
    m = k.shape[0]
    nq = seq // tq
    nk = seq // tk
    return pl.pallas_call(
        functools.partial(_fox_kernel, tk=tk),
        grid=(batch, FOX_HEADS, nq),
        in_specs=[
            pl.BlockSpec((FOX_DH, tq), lambda b, h, i: (h, b * nq + i)),
            pl.BlockSpec((seq, FOX_DH), lambda b, h, i: (b, h)),
            pl.BlockSpec((FOX_DH, seq), lambda b, h, i: (h, b)),
            pl.BlockSpec((None, FOX_HEADS, seq), lambda b, h, i: (b, 0, 0)),
        ],
        out_specs=pl.BlockSpec((tq, FOX_DH), lambda b, h, i: (b * nq + i, h)),
        out_shape=jax.ShapeDtypeStruct((m, FOX_HEADS * FOX_DH), BF16),
        scratch_shapes=[
            pltpu.VMEM((nk, tk, 2 * FOX_DH), BF16),
            pltpu.VMEM((nk, FOX_DH, tk), BF16),
            pltpu.VMEM((nk, FOX_DH + FOX_SUM_ROWS, tk), BF16),
            pltpu.VMEM((2 * FOX_DH, tq), BF16),
            pltpu.VMEM((tk, tq), F32),
            pltpu.VMEM((tk, tq), F32),
            pltpu.VMEM((1, tq), F32),
            pltpu.VMEM((1, tq), F32),
            pltpu.VMEM((1, tq), F32),
            pltpu.VMEM((FOX_DH + FOX_SUM_ROWS, tq), F32),
        ],
        compiler_params=_params(("parallel", "parallel", "arbitrary")),
        name="fox_attention",
    )(q_t, k, v_t, c)


def _merge_body(x, oa_ref, ga_ref, ob_ref, g0_ref, g1_ref, hg_ref, wa_ref, wb_ref, wo_ref, post_ref):
    oa = _rms(oa_ref[...].astype(F32), hg_ref[...]) * ga_ref[...].astype(F32)
    ya = _dot(oa.astype(BF16), wa_ref[...])
    yb = _dot(ob_ref[...], wb_ref[...])
    y = g0_ref[...].astype(F32) * ya + g1_ref[...].astype(F32) * yb
    z = _dot(y.astype(BF16), wo_ref[...])
    return x + _rms(z, post_ref[...])


def _memattn_body(x, pre_ref, kv_ref, wq_ref, wo_ref, post_ref):
    h = _rms(x, pre_ref[...]).astype(BF16)
    q = _dot(h, wq_ref[...]).astype(BF16)
    heads = []
    for hd in range(MEM_HEADS):
        lo = hd * MEM_DH
        kh = kv_ref[:, lo:lo + MEM_DH]
        vh = kv_ref[:, D_MODEL + lo:D_MODEL + lo + MEM_DH]
        s = lax.dot_general(q[:, lo:lo + MEM_DH], kh, NT_DIMS, preferred_element_type=F32)
        p = jnp.exp(s - jnp.max(s, axis=1, keepdims=True))
        inv = 1.0 / jnp.sum(p, axis=1, keepdims=True)
        heads.append((_dot(p.astype(BF16), vh) * inv).astype(BF16))
    o = jnp.concatenate(heads, axis=1)
    return x + _rms(_dot(o, wo_ref[...]), post_ref[...])


def _tail_kernel(x_ref, oa_ref, ga_ref, ob_ref, g0_ref, g1_ref, kv_ref, hg_ref, wa_ref, wb_ref,
                 wo_ref, mix_post_ref, mem_pre_ref, wq_ref, wmo_ref, mem_post_ref,
                 ffn_pre_ref, w_in_ref, w_down_ref, ffn_post_ref, o_ref, acc_ref):
    x = _merge_body(x_ref[...], oa_ref, ga_ref, ob_ref, g0_ref, g1_ref, hg_ref, wa_ref, wb_ref,
                    wo_ref, mix_post_ref)
    x = _memattn_body(x, mem_pre_ref, kv_ref, wq_ref, wmo_ref, mem_post_ref)
    o_ref[...] = _ffn_body(x, ffn_pre_ref, w_in_ref, w_down_ref, ffn_post_ref, acc_ref)


def _tail(x, o_a, g_act, o_b, gates, kv, hg_norm_g, w_a, w_b, w_o, mix_post_g, mem_pre_g, w_q,
          w_mo, mem_post_g, ffn_pre_g, w_in, w_down, ffn_post_g, batch, tm=512):
    m, d = x.shape
    ns = m // batch // tm
    row = lambda c: pl.BlockSpec((tm, d), lambda b, s: (b * ns + s, c))
    vec, mat = _resident((1, d)), _resident((d, d))
    r1 = lambda g: g.reshape(1, d)
    return pl.pallas_call(
        _tail_kernel,
        grid=(batch, ns),
        in_specs=[row(0), row(0), row(0), row(0), row(0), row(1),
                  pl.BlockSpec((MEM_LEN, 2 * d), lambda b, s: (b, 0)),
                  vec, mat, mat, mat, vec, vec, mat, mat, vec,
                  vec, _resident(w_in.shape), _resident(w_down.shape), vec],
        out_specs=row(0),
        out_shape=jax.ShapeDtypeStruct((m, d), F32),
        scratch_shapes=[pltpu.VMEM((tm, d), F32)],
        compiler_params=_params(("parallel", "parallel")),
        name="merge_memattn_ffn",
    )(x, o_a, g_act, o_b, gates, gates, kv, r1(hg_norm_g), w_a, w_b, w_o, r1(mix_post_g),
      r1(mem_pre_g), w_q, w_mo, r1(mem_post_g), r1(ffn_pre_g), w_in, w_down, r1(ffn_post_g))


def kernel(x, mem, ffn1_pre_g, ffn1_w_in, ffn1_w_down, ffn1_post_g, mix_pre_g, w_in, hg_lb_logits, hg_norm_g, fox_f_bias, w_branch_a, w_branch_b, b_gate, w_out, mix_post_g, mem_pre_g, mem_kv_g, w_mq, w_mkv, w_mo, mem_post_g, ffn2_pre_g, ffn2_w_in, ffn2_w_down, ffn2_post_g):
    batch, seq, d = x.shape
    depth = ffn1_w_in.shape[0]
    xf = x.reshape(batch * seq, d)
    memf = mem.reshape(batch * MEM_LEN, d)
    kw = HG_HEADS * HG_DK
    off_b = 4 * kw
    off_f = off_b + 3 * FOX_HEADS * FOX_DH
    off_g = off_f + FOX_HEADS
    for l in range(depth):
        xf = _ffn(xf, ffn1_pre_g[l], ffn1_w_in[l].astype(BF16), ffn1_w_down[l].astype(BF16),
                  ffn1_post_g[l])

        wt = jnp.swapaxes(w_in[l], 0, 1)
        fox_w = FOX_HEADS * FOX_DH
        weights = (wt.astype(BF16),)
        q_scale = LOG2E / math.sqrt(FOX_DH)
        groups = ((0, 0, "silu", 1.0), (0, kw, None, 1.0), (0, 2 * kw, None, 1.0),
                  (0, 3 * kw, "silu", 1.0), (0, off_b, "transpose", q_scale),
                  (0, off_b + fox_w, None, 1.0), (0, off_b + 2 * fox_w, "transpose", 1.0),
                  (0, off_f, "transpose_result", 1.0), (0, off_g, "sigmoid_bias", 1.0))
        q_act, f_logit, inp, g_act, q_t, k_b, v_t, fb_t, gates = _inproj(
            xf, mix_pre_g[l], weights, b_gate[l].reshape(1, 2 * d), groups,
            (kw, kw, kw, kw, fox_w, fox_w, fox_w, LANES, 2 * d),
            (BF16, F32, BF16, BF16, BF16, BF16, BF16, F32, BF16))

        c = _fcum(fb_t, fox_f_bias[l].reshape(FOX_HEADS, 1), batch, seq)

        o_a = _hgrn2(q_act, f_logit, inp, hg_lb_logits, l, batch)
        o_b = _fox(q_t, k_b, v_t, c, batch, seq)

        kv = _norm_matmul(memf, mem_kv_g[l], w_mkv[l].astype(BF16), BF16, MEM_LEN, 1024)
        xf = _tail(xf, o_a, g_act, o_b, gates, kv, hg_norm_g[l],
                   w_branch_a[l].astype(BF16), w_branch_b[l].astype(BF16), w_out[l].astype(BF16),
                   mix_post_g[l], mem_pre_g[l],
                   (w_mq[l] * (1.0 / math.sqrt(MEM_DH))).astype(BF16), w_mo[l].astype(BF16),
                   mem_post_g[l], ffn2_pre_g[l], ffn2_w_in[l].astype(BF16),
                   ffn2_w_down[l].astype(BF16), ffn2_post_g[l], batch)
    return xf.reshape(batch, seq, d)
```

```python
import functools
import math

import jax
import jax.numpy as jnp
import numpy as np
from jax import lax
from jax.experimental import pallas as pl
from jax.experimental.pallas import tpu as pltpu

F32 = jnp.float32
BF16 = jnp.bfloat16

D_MODEL = 1024
HG_HEADS = 8
HG_DK = 128
FOX_HEADS = 8
FOX_DH = 128
MEM_LEN = 256
MEM_HEADS = 4
MEM_DH = D_MODEL // MEM_HEADS
D_FF = 2816
EPS = 1e-6
LANES = 128

VMEM_LIMIT = 56 * 1024 * 1024

NT_DIMS = (((1,), (1,)), ((), ()))
TN_DIMS = (((0,), (0,)), ((), ()))


def _params(sem):
    return pltpu.CompilerParams(dimension_semantics=sem, vmem_limit_bytes=VMEM_LIMIT)


def _rms(x, g):
    ms = jnp.mean(x * x, axis=-1, keepdims=True)
    return x * lax.rsqrt(ms + EPS) * g


def _sigmoid(x):
    return 0.5 * jnp.tanh(0.5 * x) + 0.5


def _dot(a, b):
    return jnp.dot(a, b, preferred_element_type=F32)


def _norm_matmul_kernel(x_ref, g_ref, w_ref, o_ref, h_ref):
    @pl.when(pl.program_id(1) == 0)
    def _():
        h_ref[...] = _rms(x_ref[...], g_ref[...]).astype(BF16)

    o_ref[...] = _dot(h_ref[...], w_ref[...]).astype(o_ref.dtype)


def _norm_matmul(x, g, w, out_dtype, tm, tn):
    m, d = x.shape
    n = w.shape[1]
    return pl.pallas_call(
        _norm_matmul_kernel,
        grid=(m // tm, n // tn),
        in_specs=[
            pl.BlockSpec((tm, d), lambda i, j: (i, 0)),
            pl.BlockSpec((1, d), lambda i, j: (0, 0)),
            pl.BlockSpec((d, tn), lambda i, j: (0, j)),
        ],
        out_specs=pl.BlockSpec((tm, tn), lambda i, j: (i, j)),
        out_shape=jax.ShapeDtypeStruct((m, n), out_dtype),
        scratch_shapes=[pltpu.VMEM((tm, d), BF16)],
        compiler_params=_params(("parallel", "arbitrary")),
        name="norm_matmul",
    )(x, g.reshape(1, d), w)


def _resident(shape):
    return pl.BlockSpec(shape, lambda *_: (0,) * len(shape), pipeline_mode=pl.Buffered(1))


def _inproj_kernel(x_ref, g_ref, *refs, groups, tn):
    n_w = 1 + max(grp[0] for grp in groups)
    wt_refs, bias_ref, out_refs = refs[:n_w], refs[n_w], refs[n_w + 1:]
    h = _rms(x_ref[...], g_ref[...]).astype(BF16)
    for o_ref, (wi, row0, act, scale) in zip(out_refs, groups):
        columns_first = act in ("transpose", "transpose_result")
        n = o_ref.shape[0] if columns_first else o_ref.shape[1]
        for j in range(0, n, tn):
            width = min(tn, n - j)
            w_rows = wt_refs[wi][row0 + j:row0 + j + width, :]
            if act == "transpose":
                y = lax.dot_general(w_rows, h, NT_DIMS, preferred_element_type=F32)
                o_ref[j:j + width, :] = (y * scale).astype(o_ref.dtype)
                continue
            y = lax.dot_general(h, w_rows, NT_DIMS, preferred_element_type=F32)
            if act == "transpose_result":
                o_ref[j:j + width, :] = y.T.astype(o_ref.dtype)
                continue
            if act == "silu":
                y = y * _sigmoid(y)
            elif act == "sigmoid_bias":
                y = _sigmoid(y + bias_ref[:, j:j + width])
            o_ref[:, j:j + width] = y.astype(o_ref.dtype)


def _inproj(x, g, weights, bias, groups, widths, dtypes, tm=512, tn=512):
    m, d = x.shape
    transposed = [grp[2] in ("transpose", "transpose_result") for grp in groups]
    return pl.pallas_call(
        functools.partial(_inproj_kernel, groups=groups, tn=tn),
        grid=(m // tm,),
        in_specs=[pl.BlockSpec((tm, d), lambda i: (i, 0)), _resident((1, d))]
        + [_resident(w.shape) for w in weights] + [_resident(bias.shape)],
        out_specs=[pl.BlockSpec((n, tm), lambda i: (0, i)) if t else
                   pl.BlockSpec((tm, n), lambda i: (i, 0)) for n, t in zip(widths, transposed)],
        out_shape=[jax.ShapeDtypeStruct((n, m) if t else (m, n), dt)
                   for n, dt, t in zip(widths, dtypes, transposed)],
        compiler_params=_params(("parallel",)),
        name="input_projection",
    )(x, g.reshape(1, d), *weights, bias)


FFN_CHUNK = 256


def _ffn_body(x, pre_ref, w_in_ref, w_down_ref, post_ref, acc_ref):
    h = _rms(x, pre_ref[...]).astype(BF16)
    for f in range(D_FF // FFN_CHUNK):
        cols = slice(f * FFN_CHUNK, (f + 1) * FFN_CHUNK)
        gate = _dot(h, w_in_ref[:, cols])
        up = _dot(h, w_in_ref[:, D_FF + f * FFN_CHUNK:D_FF + (f + 1) * FFN_CHUNK])
        act = (gate * _sigmoid(gate) * up).astype(BF16)
        part = _dot(act, w_down_ref[cols, :])
        if f == 0:
            acc_ref[...] = part
        else:
            acc_ref[...] += part
    return x + 0.5 * _rms(acc_ref[...], post_ref[...])


def _ffn_kernel(x_ref, pre_ref, w_in_ref, w_down_ref, post_ref, o_ref, acc_ref):
    o_ref[...] = _ffn_body(x_ref[...], pre_ref, w_in_ref, w_down_ref, post_ref, acc_ref)


def _ffn(x, pre_g, w_in, w_down, post_g, tm=1024):
    m, d = x.shape
    return pl.pallas_call(
        _ffn_kernel,
        grid=(m // tm,),
        in_specs=[
            pl.BlockSpec((tm, d), lambda i: (i, 0)),
            _resident((1, d)),
            _resident(w_in.shape),
            _resident(w_down.shape),
            _resident((1, d)),
        ],
        out_specs=pl.BlockSpec((tm, d), lambda i: (i, 0)),
        out_shape=jax.ShapeDtypeStruct((m, d), F32),
        scratch_shapes=[pltpu.VMEM((tm, d), F32)],
        compiler_params=_params(("parallel",)),
        name="ffn",
    )(x, pre_g.reshape(1, d), w_in, w_down, post_g.reshape(1, d))


def _split3(x):
    hi = x.astype(BF16)
    r = x - hi.astype(F32)
    mid = r.astype(BF16)
    lo = (r - mid.astype(F32)).astype(BF16)
    return hi, mid, lo


def _fcum_kernel(fb_ref, bias_ref, c_ref, carry_ref):
    @pl.when(pl.program_id(1) == 0)
    def _():
        carry_ref[...] = jnp.zeros_like(carry_ref)

    z = fb_ref[0:FOX_HEADS, :] + bias_ref[...]
    ls = jnp.minimum(z, 0.0) - jnp.log(1.0 + jnp.exp(-jnp.abs(z)))
    tb = z.shape[1]
    row = lax.broadcasted_iota(jnp.int32, (tb, tb), 0)
    col = lax.broadcasted_iota(jnp.int32, (tb, tb), 1)
    triu = (row <= col).astype(BF16)
    stacked = jnp.concatenate([p.astype(F32) for p in _split3(ls)], axis=0).astype(BF16)
    parts = _dot(stacked, triu)
    c = (parts[0:FOX_HEADS] + parts[FOX_HEADS:2 * FOX_HEADS] + parts[2 * FOX_HEADS:]
         + carry_ref[...])
    c_ref[...] = c
    carry_ref[...] = jnp.broadcast_to(c[:, tb - 1:tb], carry_ref.shape)


def _fcum(fb_t, bias, batch, seq, tb=512):
    ns = seq // tb
    return pl.pallas_call(
        _fcum_kernel,
        grid=(batch, ns),
        in_specs=[
            pl.BlockSpec((LANES, tb), lambda b, s: (0, b * ns + s)),
            pl.BlockSpec((FOX_HEADS, 1), lambda b, s: (0, 0)),
        ],
        out_specs=pl.BlockSpec((None, FOX_HEADS, tb), lambda b, s: (b, 0, s)),
        out_shape=jax.ShapeDtypeStruct((batch, FOX_HEADS, seq), F32),
        scratch_shapes=[pltpu.VMEM((FOX_HEADS, tb), F32)],
        compiler_params=_params(("parallel", "arbitrary")),
        name="fox_decay_cumsum",
    )(fb_t, bias)


def _hgrn_level_table(c):
    t = np.arange(c)[:, None]
    s = np.arange(c)[None, :]
    level = np.full((c, c), -2, np.int32)
    level[t == s] = -1
    for l in range(int(math.log2(c))):
        h = c >> (l + 1)
        mid = (t // (2 * h)) * (2 * h) + h
        sel = (t // (2 * h) == s // (2 * h)) & (t >= mid) & (s < mid)
        level[sel & (level == -2)] = l
    return level


SUBLANES = 8


def _prefix_sum_rows(x, row):
    k = 1
    while k < x.shape[0]:
        x = x + jnp.where(row >= k, pltpu.roll(x, k, axis=0), 0.0)
        k *= 2
    return x


def _hgrn_sign_table(c):
    t = np.arange(c)[:, None]
    halves = [c >> (l + 1) for l in range(int(math.log2(c)))]
    sign = [np.where((t % (2 * h)) >= h, 1.0, -1.0) * np.ones((1, HG_DK)) for h in halves]
    return np.stack(sign).astype(np.float32)


def _midpoint_rows(b, h):
    c, w = b.shape
    if h >= SUBLANES:
        pieces = [jnp.broadcast_to(b[j * 2 * h + h - 1:j * 2 * h + h, :], (2 * h, w))
                  for j in range(c // (2 * h))]
        return pieces[0] if len(pieces) == 1 else jnp.concatenate(pieces, axis=0)
    groups = c // SUBLANES
    b3 = b.reshape(groups, SUBLANES, w)
    sub = lax.broadcasted_iota(jnp.int32, (1, SUBLANES, w), 1)
    r3 = None
    for j in range(SUBLANES // (2 * h)):
        src = j * 2 * h + h - 1
        piece = jnp.broadcast_to(b3[:, src:src + 1, :], (groups, SUBLANES, w))
        r3 = piece if r3 is None else jnp.where(sub >= j * 2 * h, piece, r3)
    return r3.reshape(c, w)


def _hgrn_kernel(q_ref, f_ref, i_ref, lb_ref, lvl_ref, sign_ref, o_ref, st_ref, *, chunk, layer):
    @pl.when(pl.program_id(2) == 0)
    def _():
        st_ref[...] = jnp.zeros_like(st_ref)

    c = chunk
    n_lvl = int(math.log2(c))
    ts = q_ref.shape[0]
    logits = lb_ref[...]
    e = jnp.exp(logits - jnp.max(logits, axis=0, keepdims=True))
    lb = jnp.sum(e[0:layer + 1, :], axis=0, keepdims=True) / jnp.sum(e, axis=0, keepdims=True)
    lvl = lvl_ref[...]
    st = st_ref[...]

    row = lax.broadcasted_iota(jnp.int32, (c, HG_DK), 0)
    ks, bs = [], []
    for ci in range(ts // c):
        f = lb + (1.0 - lb) * _sigmoid(f_ref[pl.ds(ci * c, c), :])
        bs.append(_prefix_sum_rows(jnp.log2(f), row))
        ks.append(1.0 - f)

    chunks = range(ts // c)
    q16s = [q_ref[pl.ds(ci * c, c), :] for ci in chunks]
    vs = [i_ref[pl.ds(ci * c, c), :] for ci in chunks]
    k16s = [k.astype(BF16) for k in ks]

    a = [jnp.where(lvl == -1, lax.dot_general(q16s[ci], k16s[ci], NT_DIMS,
                                              preferred_element_type=F32), 0.0) for ci in chunks]
    for l in range(n_lvl):
        h = c >> (l + 1)
        for ci in chunks:
            d_l = (bs[ci] - _midpoint_rows(bs[ci], h)) * sign_ref[l]
            e_l = jnp.exp2(d_l).astype(BF16)
            p = lax.dot_general(q16s[ci] * e_l, k16s[ci] * e_l, NT_DIMS,
                                preferred_element_type=F32)
            a[ci] = jnp.where(lvl == l, p, a[ci])

    o_intra = [_dot(a[ci].astype(BF16), vs[ci]) for ci in chunks]
    upds = [lax.dot_general(vs[ci], (ks[ci] * jnp.exp2(bs[ci][c - 1:c, :] - bs[ci])).astype(BF16),
                            TN_DIMS, preferred_element_type=F32) for ci in chunks]

    for ci in chunks:
        o = o_intra[ci] + lax.dot_general(q16s[ci] * jnp.exp2(bs[ci]).astype(BF16),
                                          st.astype(BF16), NT_DIMS, preferred_element_type=F32)
        o_ref[pl.ds(ci * c, c), :] = o.astype(o_ref.dtype)
        st = st * jnp.exp2(bs[ci][c - 1:c, :]) + upds[ci]

    st_ref[...] = st


def _hgrn2(q_act, f_logit, inp, lb_logits, layer, batch, ts=1024, chunk=128):
    m = q_act.shape[0]
    ns = m // batch // ts
    level = _hgrn_level_table(chunk)
    sign = _hgrn_sign_table(chunk)
    nl = lb_logits.shape[0]
    blk = pl.BlockSpec((ts, HG_DK), lambda b, h, s: (b * ns + s, h))
    return pl.pallas_call(
        functools.partial(_hgrn_kernel, chunk=chunk, layer=layer),
        grid=(batch, HG_HEADS, ns),
        in_specs=[
            blk, blk, blk,
            pl.BlockSpec((nl, HG_DK), lambda b, h, s: (0, h)),
            _resident(level.shape), _resident(sign.shape),
        ],
        out_specs=blk,
        out_shape=jax.ShapeDtypeStruct((m, HG_HEADS * HG_DK), BF16),
        scratch_shapes=[pltpu.VMEM((HG_DK, HG_DK), F32)],
        compiler_params=_params(("parallel", "parallel", "arbitrary")),
        name="hgrn2",
    )(q_act, f_logit, inp, lb_logits.reshape(nl, HG_HEADS * HG_DK), jnp.asarray(level),
      jnp.asarray(sign))


LOG2E = 1.4426950408889634
FOX_FOLD = 8
FOX_SUM_ROWS = 16


def _decay_features_t(heads, h):
    rows = heads.shape[1]
    sub = lax.broadcasted_iota(jnp.int32, (SUBLANES, rows), 0)
    c_row = jnp.sum(jnp.where(sub == h, heads, 0.0), axis=0, keepdims=True)
    hi, mid, lo = (jnp.broadcast_to(part.astype(F32), (SUBLANES, rows))
                   for part in _split3(c_row * LOG2E))

    def place(base, sign):
        ones = (sub >= 3 - base) & (sub < 6 - base)
        feat = jnp.where(sub == base, sign * hi,
                         jnp.where(sub == base + 1, sign * mid,
                                   jnp.where(sub == base + 2, sign * lo,
                                             jnp.where(ones, 1.0, 0.0))))
        pad = jnp.zeros((LANES - SUBLANES, rows), F32)
        return jnp.concatenate([feat, pad], axis=0).astype(BF16)

    return place(3, 1.0), place(0, -1.0)


def _fox_kernel(q_ref, k_ref, v_ref, c_ref, o_ref, kx_ref, qft_ref, vt_ref, qxt_ref,
                s0_ref, s1_ref, x0_ref, x1_ref, m_ref, acc_ref, *, tk):
    h = pl.program_id(1)
    qi = pl.program_id(2)
    tq = q_ref.shape[1]
    nk = kx_ref.shape[0]

    @pl.when(qi == 0)
    def _():
        ones_row = (lax.broadcasted_iota(jnp.int32, (FOX_SUM_ROWS, tk), 0) == 0)
        ones_row = ones_row.astype(F32).astype(BF16)
        for kb in range(nk):
            rows = slice(kb * tk, (kb + 1) * tk)
            q_feat_t, k_feat_t = _decay_features_t(c_ref[:, rows], h)
            kx_ref[kb, :, 0:FOX_DH] = k_ref[rows, :]
            kx_ref[kb, :, FOX_DH:2 * FOX_DH] = k_feat_t.T
            qft_ref[kb] = q_feat_t
            vt_ref[kb, 0:FOX_DH, :] = v_ref[:, rows]
            vt_ref[kb, FOX_DH:FOX_DH + FOX_SUM_ROWS, :] = ones_row

    ratio = tq // tk
    n_full = ratio * qi
    qxt_ref[0:FOX_DH, :] = q_ref[...]
    for j in range(ratio):
        qxt_ref[FOX_DH:2 * FOX_DH, j * tk:(j + 1) * tk] = qft_ref[n_full + j]
    m_ref[...] = jnp.full_like(m_ref, -jnp.inf)
    acc_ref[...] = jnp.zeros_like(acc_ref)

    def fold(x, op):
        part = op(x.reshape(FOX_FOLD, tk // FOX_FOLD, x.shape[1]), axis=0)
        return op(part, axis=0, keepdims=True)

    def scores(kb, s_ref, smax_ref):
        st = _dot(kx_ref[kb], qxt_ref[...])
        s_ref[...] = st
        smax_ref[...] = fold(st, jnp.max)

    def causal(st):
        key = lax.broadcasted_iota(jnp.int32, st.shape, 0)
        qry = lax.broadcasted_iota(jnp.int32, st.shape, 1)
        return jnp.where(key <= qry, st, -jnp.inf)

    def softmax_step(kb, st, blk_max, cols):
        m_prev = m_ref[:, cols]
        m_new = jnp.maximum(m_prev, blk_max)
        alpha = jnp.exp2(m_prev - m_new)
        p = jnp.exp2(st - m_new)
        m_ref[:, cols] = m_new
        acc_ref[:, cols] = alpha * acc_ref[:, cols] + _dot(vt_ref[kb], p.astype(BF16))

    def update(kb, s_ref, smax_ref):
        softmax_step(kb, s_ref[...], smax_ref[...], slice(None))

    def update_diagonal(kb, s_ref):
        st = s_ref[...]
        head = causal(st[:, 0:tk])
        st = head if tq == tk else jnp.concatenate([head, st[:, tk:]], axis=1)
        softmax_step(kb, st, fold(st, jnp.max), slice(None))

    scores(0, s0_ref, x0_ref)

    def pair(kb):
        scores(kb + 1, s1_ref, x1_ref)
        update(kb, s0_ref, x0_ref)
        scores(kb + 2, s0_ref, x0_ref)
        update(kb + 1, s1_ref, x1_ref)

    def body4(j, carry):
        pair(4 * j)
        pair(4 * j + 2)
        return carry

    def body2(j, carry):
        pair(2 * j)
        return carry

    n_pairs = n_full // 2
    lax.fori_loop(0, n_pairs // 2, body4, 0)
    lax.fori_loop(2 * (n_pairs // 2), n_pairs, body2, 0)

    if ratio == 2:
        upper = slice(tk, tq)
        tile = causal(_dot(kx_ref[n_full + 1], qxt_ref[:, upper]))
        update_diagonal(n_full, s0_ref)
        softmax_step(n_full + 1, tile, fold(tile, jnp.max), upper)
    else:
        @pl.when(qi % 2 == 1)
        def _():
            scores(qi, s1_ref, x1_ref)
            update(qi - 1, s0_ref, x0_ref)
            update_diagonal(qi, s1_ref)

        @pl.when(qi % 2 == 0)
        def _():
            update_diagonal(qi, s0_ref)

    out = acc_ref[0:FOX_DH, :] / acc_ref[FOX_DH:FOX_DH + 1, :]
    o_ref[...] = out.astype(o_ref.dtype).T


def _fox(q_t, k, v_t, c, batch, seq, tq=1024, tk=512):
    assert tq in (tk, 2 * tk)
    m = k.shape[0]
    nq = seq // tq
    nk = seq // tk
    return pl.pallas_call(
        functools.partial(_fox_kernel, tk=tk),
        grid=(batch, FOX_HEADS, nq),
        in_specs=[
            pl.BlockSpec((FOX_DH, tq), lambda b, h, i: (h, b * nq + i)),
            pl.BlockSpec((seq, FOX_DH), lambda b, h, i: (b, h)),
            pl.BlockSpec((FOX_DH, seq), lambda b, h, i: (h, b)),
            pl.BlockSpec((None, FOX_HEADS, seq), lambda b, h, i: (b, 0, 0)),
        ],
        out_specs=pl.BlockSpec((tq, FOX_DH), lambda b, h, i: (b * nq + i, h)),
        out_shape=jax.ShapeDtypeStruct((m, FOX_HEADS * FOX_DH), BF16),
        scratch_shapes=[
            pltpu.VMEM((nk, tk, 2 * FOX_DH), BF16),
            pltpu.VMEM((nk, FOX_DH, tk), BF16),
            pltpu.VMEM((nk, FOX_DH + FOX_SUM_ROWS, tk), BF16),
            pltpu.VMEM((2 * FOX_DH, tq), BF16),
            pltpu.VMEM((tk, tq), F32),
            pltpu.VMEM((tk, tq), F32),
            pltpu.VMEM((1, tq), F32),
            pltpu.VMEM((1, tq), F32),
            pltpu.VMEM((1, tq), F32),
            pltpu.VMEM((FOX_DH + FOX_SUM_ROWS, tq), F32),
        ],
        compiler_params=_params(("parallel", "parallel", "arbitrary")),
        name="fox_attention",
    )(q_t, k, v_t, c)


def _merge_body(x, oa_ref, ga_ref, ob_ref, g0_ref, g1_ref, hg_ref, wa_ref, wb_ref, wo_ref, post_ref):
    oa = _rms(oa_ref[...].astype(F32), hg_ref[...]) * ga_ref[...].astype(F32)
    ya = _dot(oa.astype(BF16), wa_ref[...])
    yb = _dot(ob_ref[...], wb_ref[...])
    y = g0_ref[...].astype(F32) * ya + g1_ref[...].astype(F32) * yb
    z = _dot(y.astype(BF16), wo_ref[...])
    return x + _rms(z, post_ref[...])


def _memattn_body(x, pre_ref, kv_ref, wq_ref, wo_ref, post_ref):
    h = _rms(x, pre_ref[...]).astype(BF16)
    q = _dot(h, wq_ref[...]).astype(BF16)
    heads = []
    for hd in range(MEM_HEADS):
        lo = hd * MEM_DH
        kh = kv_ref[:, lo:lo + MEM_DH]
        vh = kv_ref[:, D_MODEL + lo:D_MODEL + lo + MEM_DH]
        s = lax.dot_general(q[:, lo:lo + MEM_DH], kh, NT_DIMS, preferred_element_type=F32)
        p = jnp.exp(s - jnp.max(s, axis=1, keepdims=True))
        inv = 1.0 / jnp.sum(p, axis=1, keepdims=True)
        heads.append((_dot(p.astype(BF16), vh) * inv).astype(BF16))
    o = jnp.concatenate(heads, axis=1)
    return x + _rms(_dot(o, wo_ref[...]), post_ref[...])


def _tail_kernel(x_ref, oa_ref, ga_ref, ob_ref, g0_ref, g1_ref, kv_ref, hg_ref, wa_ref, wb_ref,
                 wo_ref, mix_post_ref, mem_pre_ref, wq_ref, wmo_ref, mem_post_ref,
                 ffn_pre_ref, w_in_ref, w_down_ref, ffn_post_ref, o_ref, acc_ref):
    x = _merge_body(x_ref[...], oa_ref, ga_ref, ob_ref, g0_ref, g1_ref, hg_ref, wa_ref, wb_ref,
                    wo_ref, mix_post_ref)
    x = _memattn_body(x, mem_pre_ref, kv_ref, wq_ref, wmo_ref, mem_post_ref)
    o_ref[...] = _ffn_body(x, ffn_pre_ref, w_in_ref, w_down_ref, ffn_post_ref, acc_ref)


def _tail(x, o_a, g_act, o_b, gates, kv, hg_norm_g, w_a, w_b, w_o, mix_post_g, mem_pre_g, w_q,
          w_mo, mem_post_g, ffn_pre_g, w_in, w_down, ffn_post_g, batch, tm=512):
    m, d = x.shape
    ns = m // batch // tm
    row = lambda c: pl.BlockSpec((tm, d), lambda b, s: (b * ns + s, c))
    vec, mat = _resident((1, d)), _resident((d, d))
    r1 = lambda g: g.reshape(1, d)
    return pl.pallas_call(
        _tail_kernel,
        grid=(batch, ns),
        in_specs=[row(0), row(0), row(0), row(0), row(0), row(1),
                  pl.BlockSpec((MEM_LEN, 2 * d), lambda b, s: (b, 0)),
                  vec, mat, mat, mat, vec, vec, mat, mat, vec,
                  vec, _resident(w_in.shape), _resident(w_down.shape), vec],
        out_specs=row(0),
        out_shape=jax.ShapeDtypeStruct((m, d), F32),
        scratch_shapes=[pltpu.VMEM((tm, d), F32)],
        compiler_params=_params(("parallel", "parallel")),
        name="merge_memattn_ffn",
    )(x, o_a, g_act, o_b, gates, gates, kv, r1(hg_norm_g), w_a, w_b, w_o, r1(mix_post_g),
      r1(mem_pre_g), w_q, w_mo, r1(mem_post_g), r1(ffn_pre_g), w_in, w_down, r1(ffn_post_g))


def kernel(x, mem, ffn1_pre_g, ffn1_w_in, ffn1_w_down, ffn1_post_g, mix_pre_g, w_in, hg_lb_logits, hg_norm_g, fox_f_bias, w_branch_a, w_branch_b, b_gate, w_out, mix_post_g, mem_pre_g, mem_kv_g, w_mq, w_mkv, w_mo, mem_post_g, ffn2_pre_g, ffn2_w_in, ffn2_w_down, ffn2_post_g):
    batch, seq, d = x.shape
    depth = ffn1_w_in.shape[0]
    xf = x.reshape(batch * seq, d)
    memf = mem.reshape(batch * MEM_LEN, d)
    kw = HG_HEADS * HG_DK
    off_b = 4 * kw
    off_f = off_b + 3 * FOX_HEADS * FOX_DH
    off_g = off_f + FOX_HEADS
    for l in range(depth):
        xf = _ffn(xf, ffn1_pre_g[l], ffn1_w_in[l].astype(BF16), ffn1_w_down[l].astype(BF16),
                  ffn1_post_g[l])

        wt = jnp.swapaxes(w_in[l], 0, 1)
        fox_w = FOX_HEADS * FOX_DH
        weights = (wt.astype(BF16),)
        q_scale = LOG2E / math.sqrt(FOX_DH)
        groups = ((0, 0, "silu", 1.0), (0, kw, None, 1.0), (0, 2 * kw, None, 1.0),
                  (0, 3 * kw, "silu", 1.0), (0, off_b, "transpose", q_scale),
                  (0, off_b + fox_w, None, 1.0), (0, off_b + 2 * fox_w, "transpose", 1.0),
                  (0, off_f, "transpose_result", 1.0), (0, off_g, "sigmoid_bias", 1.0))
        q_act, f_logit, inp, g_act, q_t, k_b, v_t, fb_t, gates = _inproj(
            xf, mix_pre_g[l], weights, b_gate[l].reshape(1, 2 * d), groups,
            (kw, kw, kw, kw, fox_w, fox_w, fox_w, LANES, 2 * d),
            (BF16, F32, BF16, BF16, BF16, BF16, BF16, F32, BF16))

        c = _fcum(fb_t, fox_f_bias[l].reshape(FOX_HEADS, 1), batch, seq)

        o_a = _hgrn2(q_act, f_logit, inp, hg_lb_logits, l, batch)
        o_b = _fox(q_t, k_b, v_t, c, batch, seq)

        kv = _norm_matmul(memf, mem_kv_g[l], w_mkv[l].astype(BF16), BF16, MEM_LEN, 1024)
        xf = _tail(xf, o_a, g_act, o_b, gates, kv, hg_norm_g[l],
                   w_branch_a[l].astype(BF16), w_branch_b[l].astype(BF16), w_out[l].astype(BF16),
                   mix_post_g[l], mem_pre_g[l],
                   (w_mq[l] * (1.0 / math.sqrt(MEM_DH))).astype(BF16), w_mo[l].astype(BF16),
                   mem_post_g[l], ffn2_pre_g[l], ffn2_w_in[l].astype(BF16),
                   ffn2_w_down[l].astype(BF16), ffn2_post_g[l], batch)
    return xf.reshape(batch, seq, d)
```
